```python
import jax, jax.numpy as jnp
from jax import lax
import numpy as np

D_MODEL = 2048
BATCH = 2
SEQ = 8192
DEPTH = 1

FOX_HEADS = 8
FOX_HEAD_DIM = 128
FOX_W = FOX_HEADS * FOX_HEAD_DIM
Q_BLOCK = 128
HG_HEADS = 8
HG_HEAD_DIM = 128
HG_W = HG_HEADS * HG_HEAD_DIM
HG_CHUNK = 64
MIX_W = FOX_W + HG_W
IN_SPLITS = [FOX_W, FOX_W, FOX_W, FOX_HEADS, HG_W, HG_W, HG_W, HG_W]
IN_COLS = sum(IN_SPLITS)
N_EXPERTS = 32
TOP_K = 4
D_FF = D_MODEL
SWIGLU_ALPHA = 1.702
SWIGLU_LIMIT = 7.0
MOE_BLOCK = 256
N_MOD = 6
NORM_EPS = 1e-6

kernel_name = "fox_hgrn2_hybrid_moe_adaln"


def rms_norm(x, g):
    xf = x.astype(jnp.float32)
    y = xf * lax.rsqrt(jnp.mean(xf * xf, axis=-1, keepdims=True) + NORM_EPS)
    return (y * g.astype(jnp.float32)).astype(x.dtype)


def fox_attention(q, k, v, logf):
    B, S, H, Dh = q.shape
    scale = Dh ** -0.5
    cum = jnp.cumsum(logf, axis=1).transpose(0, 2, 1)
    outs = []
    for qb in range(S // Q_BLOCK):
        q0, q1 = qb * Q_BLOCK, (qb + 1) * Q_BLOCK
        s = jnp.einsum('bqhd,bkhd->bhqk', q[:, q0:q1], k[:, :q1]).astype(jnp.float32) * scale
        s = s + cum[:, :, q0:q1, None] - cum[:, :, None, :q1]
        causal = jnp.arange(q1)[None, :] <= jnp.arange(q0, q1)[:, None]
        s = jnp.where(causal, s, -jnp.inf)
        p = jax.nn.softmax(s, axis=-1).astype(v.dtype)
        outs.append(jnp.einsum('bhqk,bkhd->bqhd', p, v[:, :q1]))
    return jnp.concatenate(outs, axis=1)


def hgrn2_recurrence(q, f_logit, i, lb):
    B, S, H, Dk = q.shape
    nc = S // HG_CHUNK
    lbh = lb.astype(jnp.float32).reshape(H, Dk)
    f = lbh + (1.0 - lbh) * jax.nn.sigmoid(f_logit.astype(jnp.float32))
    logf = jnp.log(f)
    kk = 1.0 - f
    qq = jax.nn.silu(q.astype(jnp.float32))
    vv = i.astype(jnp.float32)

    def chunked(t):
        return t.reshape(B, nc, HG_CHUNK, H, t.shape[-1]).transpose(0, 3, 1, 2, 4)

    qq, kk, vv, logf = chunked(qq), chunked(kk), chunked(vv), chunked(logf)
    b = jnp.cumsum(logf, axis=3)
    b_last = b[:, :, :, -1:, :]
    q_dec = qq * jnp.exp(b)
    k_inv = kk * jnp.exp(-b)
    k_tail = kk * jnp.exp(b_last - b)
    causal = jnp.tril(jnp.ones((HG_CHUNK, HG_CHUNK), dtype=bool))
    attn = jnp.where(causal, jnp.einsum('bhncd,bhnsd->bhncs', q_dec, k_inv), 0.0)
    o_intra = jnp.einsum('bhncs,bhnsv->bhncv', attn, vv)
    upd = jnp.einsum('bhnsd,bhnsv->bhndv', k_tail, vv)
    decay = jnp.exp(b_last[:, :, :, 0, :])

    def step(state, inp):
        dec, u = inp
        return dec[..., None] * state + u, state

    s0 = jnp.zeros((B, H, Dk, vv.shape[-1]), jnp.float32)
    _, s_prev = lax.scan(step, s0, (jnp.moveaxis(decay, 2, 0), jnp.moveaxis(upd, 2, 0)))
    s_prev = jnp.moveaxis(s_prev, 0, 2)
    o = o_intra + jnp.einsum('bhncd,bhndv->bhncv', q_dec, s_prev)
    return o.transpose(0, 2, 3, 1, 4).reshape(B, S, H, vv.shape[-1])


def hybrid_mixer(h, w_in, b_fgate, g_fox_out, lb, g_hg_out, w_out):
    B, S, _ = h.shape
    proj = jnp.einsum('bsd,dn->bsn', h, w_in)
    cuts = [int(v) for v in np.cumsum(IN_SPLITS)[:-1]]
    fq, fk, fv, fgl, hq, hf, hi, hg = jnp.split(proj, cuts, axis=-1)
    heads_a = lambda t: t.reshape(B, S, FOX_HEADS, FOX_HEAD_DIM)
    logf = jax.nn.log_sigmoid((fgl + b_fgate).astype(jnp.float32))
    fox = fox_attention(heads_a(fq), heads_a(fk), heads_a(fv), logf).reshape(B, S, FOX_W)
    fox = rms_norm(fox, g_fox_out)
    heads_b = lambda t: t.reshape(B, S, HG_HEADS, HG_HEAD_DIM)
    o = hgrn2_recurrence(heads_b(hq), heads_b(hf), heads_b(hi), lb)
    o = rms_norm(o, g_hg_out.reshape(HG_HEADS, HG_HEAD_DIM))
    o = (o * jax.nn.silu(heads_b(hg).astype(jnp.float32))).reshape(B, S, HG_W).astype(h.dtype)
    mixed = jnp.concatenate([fox, o], axis=-1)
    return jnp.einsum('bsm,md->bsd', mixed, w_out)


def moe_ffn(h, w_router, b_router, w_gu, b_gu, w_down, b_down):
    B, S, D = h.shape
    T = B * S
    TK = T * TOP_K
    xt = h.reshape(T, D)
    logits = (xt @ w_router + b_router).astype(jnp.float32)
    top_val, top_idx = lax.top_k(logits, TOP_K)
    gates = jax.nn.softmax(top_val, axis=-1)
    e_flat = top_idx.reshape(TK).astype(jnp.int32)
    tok = jnp.arange(TK, dtype=jnp.int32) // TOP_K
    counts = jnp.bincount(e_flat, length=N_EXPERTS).astype(jnp.int32)
    group_start = jnp.cumsum(counts) - counts
    padded = (counts + MOE_BLOCK - 1) // MOE_BLOCK * MOE_BLOCK
    padded_end = jnp.cumsum(padded)
    padded_start = padded_end - padded
    order = jnp.argsort(e_flat, stable=True)
    e_sorted = e_flat[order]
    rank = jnp.arange(TK, dtype=jnp.int32) - group_start[e_sorted]
    pos = jnp.zeros((TK,), jnp.int32).at[order].set((padded_start[e_sorted] + rank).astype(jnp.int32))
    n_blocks = -(-TK // MOE_BLOCK) + N_EXPERTS
    n_rows = n_blocks * MOE_BLOCK
    src = jnp.full((n_rows,), T, jnp.int32).at[pos].set(tok)
    x_pad = jnp.concatenate([xt, jnp.zeros((1, D), xt.dtype)], axis=0)
    buf = x_pad[src].reshape(n_blocks, MOE_BLOCK, D)
    block_e = jnp.minimum(
        jnp.searchsorted(padded_end, jnp.arange(n_blocks, dtype=jnp.int32) * MOE_BLOCK, side='right'),
        N_EXPERTS - 1)

    def expert_block(args):
        xb, e = args
        gu = xb @ w_gu[e] + b_gu[e]
        glu, lin = jnp.split(gu, 2, axis=-1)
        glu = jnp.minimum(glu, SWIGLU_LIMIT)
        lin = jnp.clip(lin, -SWIGLU_LIMIT, SWIGLU_LIMIT)
        act = glu * jax.nn.sigmoid(SWIGLU_ALPHA * glu) * (lin + 1.0)
        return act @ w_down[e] + b_down[e]

    y_buf = lax.map(expert_block, (buf, block_e)).reshape(n_rows, D)
    y = y_buf[pos].reshape(T, TOP_K, D)
    out = jnp.einsum('tk,tkd->td', gates.astype(y.dtype), y)
    return out.reshape(B, S, D)


def setup_inputs(seed: int = 0) -> dict:
    key = jax.random.key(seed)
    ks = jax.random.split(key, 20)
    f32 = jnp.float32
    nrm = lambda k, shape, s: jax.random.normal(k, shape, f32) * s
    D = D_MODEL
    return {
        'x': nrm(ks[0], (BATCH, SEQ, D), 1.0),
        'c': nrm(ks[1], (BATCH, D), 1.0),
        'w_ada': nrm(ks[2], (DEPTH, D, N_MOD * D), 0.5 * D ** -0.5),
        'b_ada': nrm(ks[3], (DEPTH, N_MOD * D), 0.02),
        'g_norm1': 1.0 + nrm(ks[4], (DEPTH, D), 0.02),
        'g_norm2': 1.0 + nrm(ks[5], (DEPTH, D), 0.02),
        'w_in': nrm(ks[6], (DEPTH, D, IN_COLS), D ** -0.5),
        'b_fgate': 2.0 + nrm(ks[7], (DEPTH, FOX_HEADS), 0.1),
        'g_fox_out': 1.0 + nrm(ks[8], (DEPTH, FOX_W), 0.02),
        'lb_logits': nrm(ks[9], (DEPTH + 1, HG_W), 0.1),
        'g_hg_out': 1.0 + nrm(ks[10], (DEPTH, HG_W), 0.02),
        'w_out': nrm(ks[11], (DEPTH, MIX_W, D), MIX_W ** -0.5),
        'w_router': nrm(ks[12], (DEPTH, D, N_EXPERTS), D ** -0.5),
        'b_router': nrm(ks[13], (DEPTH, N_EXPERTS), 0.01),
        'w_gu': nrm(ks[14], (DEPTH, N_EXPERTS, D, 2 * D_FF), D ** -0.5),
        'b_gu': nrm(ks[15], (DEPTH, N_EXPERTS, 2 * D_FF), 0.01),
        'w_down': nrm(ks[16], (DEPTH, N_EXPERTS, D_FF, D), D_FF ** -0.5),
        'b_down': nrm(ks[17], (DEPTH, N_EXPERTS, D), 0.01),
        'g_final': 1.0 + nrm(ks[18], (D,), 0.02),
    }


def reference(x, c, w_ada, b_ada, g_norm1, g_norm2, w_in, b_fgate, g_fox_out, lb_logits, g_hg_out,
              w_out, w_router, b_router, w_gu, b_gu, w_down, b_down, g_final):
    gamma = jnp.cumsum(jax.nn.softmax(lb_logits.astype(jnp.float32), axis=0), axis=0)
    cond = jax.nn.silu(c)
    for l in range(DEPTH):
        mod = (cond @ w_ada[l] + b_ada[l])[:, None, :]
        sh1, sc1, gt1, sh2, sc2, gt2 = jnp.split(mod, N_MOD, axis=-1)
        h = rms_norm(x, g_norm1[l]) * (1.0 + sc1) + sh1
        x = x + gt1 * hybrid_mixer(h, w_in[l], b_fgate[l], g_fox_out[l], gamma[l], g_hg_out[l], w_out[l])
        h = rms_norm(x, g_norm2[l]) * (1.0 + sc2) + sh2
        x = x + gt2 * moe_ffn(h, w_router[l], b_router[l], w_gu[l], b_gu[l], w_down[l], b_down[l])
    return rms_norm(x, g_final)
```

```python
import functools

import jax
import jax.numpy as jnp
from jax import lax
from jax.experimental import pallas as pl
from jax.experimental.pallas import tpu as pltpu

HEAD_DIM = 128
TOP_K = 4
MOE_BLOCK = 256
HG_CHUNK = 64
NORM_EPS = 1e-6
SWIGLU_ALPHA = 1.702
SWIGLU_LIMIT = 7.0
LANES = 128
VMEM_LIMIT = 56 * 1024 * 1024

F32 = jnp.float32
BF16 = jnp.bfloat16
HIGHEST = lax.Precision.HIGHEST
NT_DIMS = (((1,), (1,)), ((), ()))
TN_DIMS = (((0,), (0,)), ((), ()))


def _params(*semantics):
    return pltpu.CompilerParams(dimension_semantics=semantics, vmem_limit_bytes=VMEM_LIMIT)


def _sigmoid(z):
    return 1.0 / (1.0 + jnp.exp(-z))


def _rms(v):
    return v * lax.rsqrt(jnp.mean(v * v, axis=-1, keepdims=True) + NORM_EPS)


def _ada_kernel(ct_ref, w_ref, b_ref, o_ref, *, n_batch, d_model):
    ct = ct_ref[...]
    cond = ct * _sigmoid(ct)
    rows = 256
    for b in range(n_batch):
        acc = jnp.zeros((8, w_ref.shape[1]), F32)
        for i in range(d_model // rows):
            w3 = w_ref[pl.ds(i * rows, rows), :].reshape(rows // 8, 8, -1)
            c3 = cond[i * rows:(i + 1) * rows, b:b + 1].reshape(rows // 8, 8, 1)
            acc = acc + jnp.sum(w3 * c3, axis=0)
        o_ref[pl.ds(b, 1), :] = jnp.sum(acc, axis=0, keepdims=True) + b_ref[...]


def _ada_mod(c, w_ada, b_ada):
    n_batch, d = c.shape
    n = w_ada.shape[1]
    tn = 1024
    return pl.pallas_call(
        functools.partial(_ada_kernel, n_batch=n_batch, d_model=d),
        out_shape=jax.ShapeDtypeStruct((n_batch, n), F32),
        grid=(n // tn,),
        in_specs=[pl.BlockSpec((d, n_batch), lambda j: (0, 0)),
                  pl.BlockSpec((d, tn), lambda j: (0, j)),
                  pl.BlockSpec((1, tn), lambda j: (0, j))],
        out_specs=pl.BlockSpec((n_batch, tn), lambda j: (0, j)),
        compiler_params=_params("arbitrary"),
        name="ada_mod",
    )(c.T, w_ada, b_ada.reshape(1, n))


def _inproj_kernel(x_ref, mod_ref, g_ref, w_ref, wf_ref, bf_ref, proj_ref, cum_ref, h_sc, carry_sc,
                   *, tiles_per_batch, sub):
    i = pl.program_id(0)
    j = pl.program_id(1)

    @pl.when(j == 0)
    def _():
        x = x_ref[...]
        h = _rms(x) * g_ref[...]
        h = h * (1.0 + mod_ref[0, 1:2, :]) + mod_ref[0, 0:1, :]
        hb = h.astype(BF16)
        h_sc[...] = hb
        z = jnp.dot(hb, wf_ref[...], preferred_element_type=F32) + bf_ref[...]
        logf = jnp.minimum(z, 0.0) - jnp.log(1.0 + jnp.exp(-jnp.abs(z)))

        @pl.when(i % tiles_per_batch == 0)
        def _():
            carry_sc[...] = jnp.zeros_like(carry_sc)

        r = lax.broadcasted_iota(jnp.int32, (sub, sub), 0)
        c = lax.broadcasted_iota(jnp.int32, (sub, sub), 1)
        tri = (r >= c).astype(F32)
        carry = carry_sc[...]
        for s in range(x.shape[0] // sub):
            blk = logf[s * sub:(s + 1) * sub, :]
            cs = jnp.dot(tri, blk, preferred_element_type=F32, precision=HIGHEST) + carry
            cum_ref[pl.ds(s * sub, sub), :] = cs
            carry = cs[sub - 1:sub, :]
        carry_sc[...] = carry

    proj_ref[...] = jnp.dot(h_sc[...], w_ref[...], preferred_element_type=F32).astype(BF16)


def _inproj(x2d, mod3, g1, w_main, w_fg, b_fg, seq):
    t, d = x2d.shape
    n = w_main.shape[1]
    tm = min(1024, seq)
    tn = next(c for c in (1024, 512, 256, 128) if n % c == 0)
    return pl.pallas_call(
        functools.partial(_inproj_kernel, tiles_per_batch=seq // tm, sub=256),
        out_shape=(jax.ShapeDtypeStruct((t, n), BF16), jax.ShapeDtypeStruct((t, LANES), F32)),
        grid=(t // tm, n // tn),
        in_specs=[pl.BlockSpec((tm, d), lambda i, j: (i, 0)),
                  pl.BlockSpec((1, mod3.shape[1], d), lambda i, j: (i // (seq // tm), 0, 0)),
                  pl.BlockSpec((1, d), lambda i, j: (0, 0)),
                  pl.BlockSpec((d, tn), lambda i, j: (0, j)),
                  pl.BlockSpec((d, LANES), lambda i, j: (0, 0)),
                  pl.BlockSpec((1, LANES), lambda i, j: (0, 0))],
        out_specs=(pl.BlockSpec((tm, tn), lambda i, j: (i, j)),
                   pl.BlockSpec((tm, LANES), lambda i, j: (i, 0))),
        scratch_shapes=[pltpu.VMEM((tm, d), BF16), pltpu.VMEM((1, LANES), F32)],
        compiler_params=_params("arbitrary", "arbitrary"),
        name="inproj",
    )(x2d, mod3, g1, w_main, w_fg, b_fg)


def _attn_kernel(q_ref, k_ref, v_ref, ck_ref, o_ref, m_sc, l_sc, acc_sc, *, blk, scale):
    i = pl.program_id(2)
    q = q_ref[0]
    m_sc[...] = jnp.full_like(m_sc, -jnp.inf)
    l_sc[...] = jnp.zeros_like(l_sc)
    acc_sc[...] = jnp.zeros_like(acc_sc)

    def step(j, diagonal):
        start = pl.multiple_of(j * blk, blk)
        k = k_ref[0, pl.ds(start, blk), :]
        v = v_ref[0, pl.ds(start, blk), :]
        s = lax.dot_general(q, k, NT_DIMS, preferred_element_type=F32) * scale
        s = s - ck_ref[0, 0, j]
        if diagonal:
            r = lax.broadcasted_iota(jnp.int32, (blk, blk), 0)
            c = lax.broadcasted_iota(jnp.int32, (blk, blk), 1)
            s = jnp.where(c <= r, s, -jnp.inf)
        m_prev = m_sc[...]
        m_new = jnp.maximum(m_prev, jnp.max(s, axis=1, keepdims=True))
        alpha = jnp.exp(m_prev - m_new)
        p = jnp.exp(s - m_new)
        l_sc[...] = alpha * l_sc[...] + jnp.sum(p, axis=1, keepdims=True)
        acc_sc[...] = alpha * acc_sc[...] + jnp.dot(p.astype(BF16), v, preferred_element_type=F32)
        m_sc[...] = m_new

    def off_diagonal(j, carry):
        step(j, False)
        return carry

    lax.fori_loop(0, i, off_diagonal, 0)
    step(i, True)
    o_ref[0] = (acc_sc[...] / l_sc[...]).astype(BF16)


def _fox_attention(proj3, cum_keys, n_heads):
    n_batch, seq, _ = proj3.shape
    blk = min(512, seq)
    return pl.pallas_call(
        functools.partial(_attn_kernel, blk=blk, scale=HEAD_DIM ** -0.5),
        out_shape=jax.ShapeDtypeStruct((n_batch, seq, n_heads * HEAD_DIM), BF16),
        grid=(n_batch, n_heads, seq // blk),
        in_specs=[pl.BlockSpec((1, blk, HEAD_DIM), lambda b, h, i: (b, i, h)),
                  pl.BlockSpec((1, seq, HEAD_DIM), lambda b, h, i: (b, 0, n_heads + h)),
                  pl.BlockSpec((1, seq, HEAD_DIM), lambda b, h, i: (b, 0, 2 * n_heads + h)),
                  pl.BlockSpec((1, 1, seq // blk, 1, blk), lambda b, h, i: (b, h, 0, 0, 0))],
        out_specs=pl.BlockSpec((1, blk, HEAD_DIM), lambda b, h, i: (b, i, h)),
        scratch_shapes=[pltpu.VMEM((blk, 1), F32), pltpu.VMEM((blk, 1), F32),
                        pltpu.VMEM((blk, HEAD_DIM), F32)],
        compiler_params=_params("arbitrary", "arbitrary", "arbitrary"),
        name="fox_attention",
    )(proj3, proj3, proj3, cum_keys)


def _hgrn_kernel(q_ref, f_ref, i_ref, g_ref, lbl_ref, gn_ref, o_ref, st_sc, *, rows, sub, layer):
    @pl.when(pl.program_id(2) == 0)
    def _():
        st_sc[...] = jnp.zeros_like(st_sc)

    lbl = lbl_ref[0]
    e = jnp.exp(lbl - jnp.max(lbl, axis=0, keepdims=True))
    lb = jnp.sum(e[0:layer + 1, :], axis=0, keepdims=True) / jnp.sum(e, axis=0, keepdims=True)
    f = lb + (1.0 - lb) * _sigmoid(f_ref[0].astype(F32))
    logf = jnp.log(f)
    kk = 1.0 - f
    qf = q_ref[0].astype(F32)
    qq = qf * _sigmoid(qf)

    r = lax.broadcasted_iota(jnp.int32, (sub, sub), 0)
    c = lax.broadcasted_iota(jnp.int32, (sub, sub), 1)
    tri = ((r >= c) & (r // HG_CHUNK == c // HG_CHUNK)).astype(F32)
    rc = lax.broadcasted_iota(jnp.int32, (HG_CHUNK, HG_CHUNK), 0)
    cc = lax.broadcasted_iota(jnp.int32, (HG_CHUNK, HG_CHUNK), 1)
    causal = rc >= cc

    for s in range(rows // sub):
        bsub = jnp.dot(tri, logf[s * sub:(s + 1) * sub, :], preferred_element_type=F32, precision=HIGHEST)
        for n in range(sub // HG_CHUNK):
            lo = s * sub + n * HG_CHUNK
            b = bsub[n * HG_CHUNK:(n + 1) * HG_CHUNK, :]
            b_last = b[HG_CHUNK - 1:HG_CHUNK, :]
            kc = kk[lo:lo + HG_CHUNK, :]
            q_dec = (qq[lo:lo + HG_CHUNK, :] * jnp.exp(b)).astype(BF16)
            k_inv = (kc * jnp.exp(-b)).astype(BF16)
            k_tail = (kc * jnp.exp(b_last - b)).astype(BF16)
            vv = i_ref[0, pl.ds(lo, HG_CHUNK), :]
            attn = lax.dot_general(q_dec, k_inv, NT_DIMS, preferred_element_type=F32)
            attn = jnp.where(causal, attn, 0.0).astype(BF16)
            st = st_sc[...]
            o = jnp.dot(attn, vv, preferred_element_type=F32)
            o = o + lax.dot_general(q_dec, st.astype(BF16), NT_DIMS, preferred_element_type=F32)
            upd_t = lax.dot_general(vv, k_tail, TN_DIMS, preferred_element_type=F32)
            st_sc[...] = jnp.exp(b_last) * st + upd_t
            y = _rms(o) * gn_ref[0]
            gf = g_ref[0, pl.ds(lo, HG_CHUNK), :].astype(F32)
            o_ref[0, pl.ds(lo, HG_CHUNK), :] = (y * (gf * _sigmoid(gf))).astype(BF16)


def _hgrn2(proj3, lb_logits3, g_hg3, n_heads, col0, layer):
    n_batch, seq, _ = proj3.shape
    rows = min(512, seq)
    spec = lambda off: pl.BlockSpec((1, rows, HEAD_DIM), lambda b, h, r: (b, r, col0 + off * n_heads + h))
    return pl.pallas_call(
        functools.partial(_hgrn_kernel, rows=rows, sub=min(256, rows), layer=layer),
        out_shape=jax.ShapeDtypeStruct((n_batch, seq, n_heads * HEAD_DIM), BF16),
        grid=(n_batch, n_heads, seq // rows),
        in_specs=[spec(0), spec(1), spec(2), spec(3),
                  pl.BlockSpec((1, lb_logits3.shape[1], HEAD_DIM), lambda b, h, r: (h, 0, 0)),
                  pl.BlockSpec((1, 1, HEAD_DIM), lambda b, h, r: (h, 0, 0))],
        out_specs=pl.BlockSpec((1, rows, HEAD_DIM), lambda b, h, r: (b, r, h)),
        scratch_shapes=[pltpu.VMEM((HEAD_DIM, HEAD_DIM), F32)],
        compiler_params=_params("arbitrary", "arbitrary", "arbitrary"),
        name="hgrn2",
    )(proj3, proj3, proj3, proj3, lb_logits3, g_hg3)


def _outproj_kernel(fox_ref, hg_ref, x_ref, mod_ref, gfox_ref, g2_ref, w_ref, wr_ref, br_ref,
                    x1_ref, h2_ref, lg_ref, *, fox_w):
    fox = _rms(fox_ref[...].astype(F32)) * gfox_ref[...]
    mix = jnp.dot(fox.astype(BF16), w_ref[pl.ds(0, fox_w), :], preferred_element_type=F32)
    mix = mix + jnp.dot(hg_ref[...], w_ref[pl.ds(fox_w, w_ref.shape[0] - fox_w), :],
                        preferred_element_type=F32)
    x1 = x_ref[...] + mod_ref[0, 2:3, :] * mix
    x1_ref[...] = x1
    h2 = _rms(x1) * g2_ref[...]
    h2 = h2 * (1.0 + mod_ref[0, 4:5, :]) + mod_ref[0, 3:4, :]
    h2_ref[...] = h2
    lg_ref[...] = lax.dot_general(wr_ref[...], h2, NT_DIMS, preferred_element_type=F32,
                                  precision=HIGHEST) + br_ref[...]


def _outproj(fox2d, hg2d, x2d, mod3, g_fox, g2, w_out, w_router_t, b_router, seq):
    t, d = x2d.shape
    fox_w = fox2d.shape[1]
    n_exp = w_router_t.shape[0]
    tm = 256
    row = lambda i: (i, 0)
    const = lambda i: (0, 0)
    return pl.pallas_call(
        functools.partial(_outproj_kernel, fox_w=fox_w),
        out_shape=(jax.ShapeDtypeStruct((t, d), F32), jax.ShapeDtypeStruct((t, d), F32),
                   jax.ShapeDtypeStruct((n_exp, t), F32)),
        grid=(t // tm,),
        in_specs=[pl.BlockSpec((tm, fox_w), row),
                  pl.BlockSpec((tm, hg2d.shape[1]), row),
                  pl.BlockSpec((tm, d), row),
                  pl.BlockSpec((1, mod3.shape[1], d), lambda i: (i // (seq // tm), 0, 0)),
                  pl.BlockSpec((1, fox_w), const),
                  pl.BlockSpec((1, d), const),
                  pl.BlockSpec(w_out.shape, const),
                  pl.BlockSpec((n_exp, d), const),
                  pl.BlockSpec((n_exp, 1), const)],
        out_specs=(pl.BlockSpec((tm, d), row), pl.BlockSpec((tm, d), row),
                   pl.BlockSpec((n_exp, tm), lambda i: (0, i))),
        compiler_params=_params("arbitrary"),
        name="outproj_router",
    )(fox2d, hg2d, x2d, mod3, g_fox, g2, w_out, w_router_t, b_router)


def _route_kernel(lg_ref, pos_ref, gate_ref, cnt_ref, cnt_sc, run_sc, *, n_exp, tb):
    phase = pl.program_id(0)
    i = pl.program_id(1)

    @pl.when((phase == 0) & (i == 0))
    def _():
        cnt_sc[...] = jnp.zeros_like(cnt_sc)
        run_sc[...] = jnp.zeros_like(run_sc)

    logits = lg_ref[...]
    eidx = lax.broadcasted_iota(jnp.int32, (n_exp, tb), 0).astype(F32)
    work = logits
    vals, hots = [], []
    for _ in range(TOP_K):
        m = jnp.max(work, axis=0, keepdims=True)
        first = jnp.min(jnp.where(work == m, eidx, float(n_exp)), axis=0, keepdims=True)
        hot = eidx == first
        vals.append(m)
        hots.append(hot)
        work = jnp.where(hot, -jnp.inf, work)
    sel = hots[0] | hots[1] | hots[2] | hots[3]
    self32 = sel.astype(F32)

    @pl.when(phase == 0)
    def _():
        cnt_sc[...] += jnp.sum(self32, axis=1, keepdims=True)

    @pl.when(phase == 1)
    def _():
        cnt = cnt_sc[...]
        padded = jnp.ceil(cnt / MOE_BLOCK) * MOE_BLOCK
        r = lax.broadcasted_iota(jnp.int32, (n_exp, n_exp), 0)
        c = lax.broadcasted_iota(jnp.int32, (n_exp, n_exp), 1)
        strict = (c < r).astype(F32)
        pstart = jnp.dot(strict, jnp.broadcast_to(padded, (n_exp, LANES)), preferred_element_type=F32,
                         precision=HIGHEST)[:, 0:1]
        tr = lax.broadcasted_iota(jnp.int32, (tb, tb), 0)
        tc = lax.broadcasted_iota(jnp.int32, (tb, tb), 1)
        upper = (tr < tc).astype(BF16)
        rank = jnp.dot(sel.astype(BF16), upper, preferred_element_type=F32)
        base = pstart + run_sc[...] + rank
        exps = [jnp.exp(v - vals[0]) for v in vals]
        denom = exps[0] + exps[1] + exps[2] + exps[3]
        for k in range(TOP_K):
            pos_k = jnp.sum(jnp.where(hots[k], base, 0.0), axis=0, keepdims=True)
            pos_ref[pl.ds(k, 1), :] = pos_k.astype(jnp.int32)
            gate_ref[pl.ds(k, 1), :] = exps[k] / denom
        run_sc[...] += jnp.sum(self32, axis=1, keepdims=True)
        cnt_ref[...] = cnt


def _route(logits_t):
    n_exp, t = logits_t.shape
    tb = min(512, t)
    return pl.pallas_call(
        functools.partial(_route_kernel, n_exp=n_exp, tb=tb),
        out_shape=(jax.ShapeDtypeStruct((TOP_K, t), jnp.int32), jax.ShapeDtypeStruct((TOP_K, t), F32),
                   jax.ShapeDtypeStruct((n_exp, 1), F32)),
        grid=(2, t // tb),
        in_specs=[pl.BlockSpec((n_exp, tb), lambda p, i: (0, i))],
        out_specs=(pl.BlockSpec((TOP_K, tb), lambda p, i: (0, i * p)),
                   pl.BlockSpec((TOP_K, tb), lambda p, i: (0, i * p)),
                   pl.BlockSpec((n_exp, 1), lambda p, i: (0, 0))),
        scratch_shapes=[pltpu.VMEM((n_exp, 1), F32), pltpu.VMEM((n_exp, 1), F32)],
        compiler_params=_params("arbitrary", "arbitrary"),
        name="route_topk",
    )(logits_t)


def _row_copy(src_hbm, row, dst_vmem, slot, sem):
    return pltpu.make_async_copy(src_hbm.at[pl.ds(row, 1), :], dst_vmem.at[pl.ds(slot, 1), :], sem)


def _dispatch_kernel(nvalid_ref, src_ref, h_hbm, o_ref, buf, sem):
    b = pl.program_id(0)

    def issue(r, carry):
        _row_copy(h_hbm, src_ref[0, 0, r], buf, r, sem).start()
        return carry

    lax.fori_loop(0, MOE_BLOCK, issue, 0)

    def drain(r, carry):
        _row_copy(h_hbm, 0, buf, r, sem).wait()
        return carry

    lax.fori_loop(0, MOE_BLOCK, drain, 0)
    rows = lax.broadcasted_iota(jnp.int32, (MOE_BLOCK, 1), 0)
    o_ref[...] = jnp.where(rows < nvalid_ref[b], buf[...], 0.0).astype(BF16)


def _dispatch(h2, src3, nvalid):
    n_blocks = src3.shape[0]
    d = h2.shape[1]
    return pl.pallas_call(
        _dispatch_kernel,
        out_shape=jax.ShapeDtypeStruct((n_blocks * MOE_BLOCK, d), BF16),
        grid_spec=pltpu.PrefetchScalarGridSpec(
            num_scalar_prefetch=1,
            grid=(n_blocks,),
            in_specs=[pl.BlockSpec((1, 1, MOE_BLOCK), lambda b, nv: (b, 0, 0), memory_space=pltpu.SMEM),
                      pl.BlockSpec(memory_space=pl.ANY)],
            out_specs=pl.BlockSpec((MOE_BLOCK, d), lambda b, nv: (b, 0)),
            scratch_shapes=[pltpu.VMEM((MOE_BLOCK, d), F32), pltpu.SemaphoreType.DMA(())]),
        compiler_params=_params("arbitrary"),
        name="moe_dispatch",
    )(nvalid, src3, h2)


def _new_expert(be_ref, b):
    return (b == 0) | (be_ref[b] != be_ref[jnp.maximum(b - 1, 0)])


def _gateup_kernel(be_ref, nused_ref, x_ref, wg_ref, wl_ref, bg_ref, bl_ref, o_ref, wg_sc, wl_sc):
    b = pl.program_id(1)

    @pl.when(_new_expert(be_ref, b))
    def _():
        wg_sc[...] = wg_ref[0].astype(BF16)
        wl_sc[...] = wl_ref[0].astype(BF16)

    @pl.when(b < nused_ref[0])
    def _():
        x = x_ref[...]
        glu = jnp.dot(x, wg_sc[...], preferred_element_type=F32) + bg_ref[0]
        lin = jnp.dot(x, wl_sc[...], preferred_element_type=F32) + bl_ref[0]
        glu = jnp.minimum(glu, SWIGLU_LIMIT)
        lin = jnp.clip(lin, -SWIGLU_LIMIT, SWIGLU_LIMIT)
        o_ref[...] = (glu * _sigmoid(SWIGLU_ALPHA * glu) * (lin + 1.0)).astype(BF16)

    @pl.when(b >= nused_ref[0])
    def _():
        o_ref[...] = jnp.zeros_like(o_ref)


def _gateup(block_e, n_used, xs, w_gu, b_gu3):
    n_rows, d = xs.shape
    n_blocks = n_rows // MOE_BLOCK
    d_ff = w_gu.shape[2] // 2
    tf = min(512, d_ff)
    nf = d_ff // tf
    return pl.pallas_call(
        _gateup_kernel,
        out_shape=jax.ShapeDtypeStruct((n_rows, d_ff), BF16),
        grid_spec=pltpu.PrefetchScalarGridSpec(
            num_scalar_prefetch=2,
            grid=(nf, n_blocks),
            in_specs=[pl.BlockSpec((MOE_BLOCK, d), lambda c, b, be, nu: (b, 0)),
                      pl.BlockSpec((1, d, tf), lambda c, b, be, nu: (be[b], 0, c)),
                      pl.BlockSpec((1, d, tf), lambda c, b, be, nu: (be[b], 0, nf + c)),
                      pl.BlockSpec((1, 1, tf), lambda c, b, be, nu: (be[b], 0, c)),
                      pl.BlockSpec((1, 1, tf), lambda c, b, be, nu: (be[b], 0, nf + c))],
            out_specs=pl.BlockSpec((MOE_BLOCK, tf), lambda c, b, be, nu: (b, c)),
            scratch_shapes=[pltpu.VMEM((d, tf), BF16), pltpu.VMEM((d, tf), BF16)]),
        compiler_params=_params("arbitrary", "arbitrary"),
        name="moe_gateup",
    )(block_e, n_used, xs, w_gu, w_gu, b_gu3, b_gu3)


def _down_kernel(be_ref, nused_ref, a_ref, w_ref, b_ref, o_ref, w_sc):
    b = pl.program_id(1)

    @pl.when(_new_expert(be_ref, b))
    def _():
        w_sc[...] = w_ref[0].astype(BF16)

    @pl.when(b < nused_ref[0])
    def _():
        o_ref[...] = jnp.dot(a_ref[...], w_sc[...], preferred_element_type=F32) + b_ref[0]

    @pl.when(b >= nused_ref[0])
    def _():
        o_ref[...] = jnp.zeros_like(o_ref)


def _down(block_e, n_used, act, w_down, b_down3):
    n_rows, d_ff = act.shape
    n_blocks = n_rows // MOE_BLOCK
    d = w_down.shape[2]
    tn = min(1024, d)
    return pl.pallas_call(
        _down_kernel,
        out_shape=jax.ShapeDtypeStruct((n_rows, d), F32),
        grid_spec=pltpu.PrefetchScalarGridSpec(
            num_scalar_prefetch=2,
            grid=(d // tn, n_blocks),
            in_specs=[pl.BlockSpec((MOE_BLOCK, d_ff), lambda c, b, be, nu: (b, 0)),
                      pl.BlockSpec((1, d_ff, tn), lambda c, b, be, nu: (be[b], 0, c)),
                      pl.BlockSpec((1, 1, tn), lambda c, b, be, nu: (be[b], 0, c))],
            out_specs=pl.BlockSpec((MOE_BLOCK, tn), lambda c, b, be, nu: (b, c)),
            scratch_shapes=[pltpu.VMEM((d_ff, tn), BF16)]),
        compiler_params=_params("arbitrary", "arbitrary"),
        name="moe_down",
    )(block_e, n_used, act, w_down, b_down3)


def _combine_kernel(pos_ref, y_hbm, gate_ref, x1_ref, mod_ref, gfin_ref, o_ref, buf, sem, *, tb, last_layer):
    def issue(r, carry):
        for k in range(TOP_K):
            _row_copy(y_hbm, pos_ref[0, 0, r * TOP_K + k], buf.at[k], r, sem).start()
        return carry

    lax.fori_loop(0, tb, issue, 0)

    def drain(r, carry):
        for k in range(TOP_K):
            _row_copy(y_hbm, 0, buf.at[k], r, sem).wait()
        return carry

    lax.fori_loop(0, tb, drain, 0)
    gates = gate_ref[...]
    moe = gates[:, 0:1] * buf[0]
    for k in range(1, TOP_K):
        moe = moe + gates[:, k:k + 1] * buf[k]
    x2 = x1_ref[...] + mod_ref[0, 5:6, :] * moe
    o_ref[...] = _rms(x2) * gfin_ref[...] if last_layer else x2


def _combine(pos3, y_buf, gates, x1, mod3, g_final, seq, last_layer):
    t, d = x1.shape
    tb = pos3.shape[2] // TOP_K
    return pl.pallas_call(
        functools.partial(_combine_kernel, tb=tb, last_layer=last_layer),
        out_shape=jax.ShapeDtypeStruct((t, d), F32),
        grid=(t // tb,),
        in_specs=[pl.BlockSpec((1, 1, tb * TOP_K), lambda i: (i, 0, 0), memory_space=pltpu.SMEM),
                  pl.BlockSpec(memory_space=pl.ANY),
                  pl.BlockSpec((tb, TOP_K), lambda i: (i, 0)),
                  pl.BlockSpec((tb, d), lambda i: (i, 0)),
                  pl.BlockSpec((1, mod3.shape[1], d), lambda i: (i // (seq // tb), 0, 0)),
                  pl.BlockSpec((1, d), lambda i: (0, 0))],
        out_specs=pl.BlockSpec((tb, d), lambda i: (i, 0)),
        scratch_shapes=[pltpu.VMEM((TOP_K, tb, d), F32), pltpu.SemaphoreType.DMA(())],
        compiler_params=_params("arbitrary"),
        name="moe_combine",
    )(pos3, y_buf, gates, x1, mod3, g_final)


def kernel(x, c, w_ada, b_ada, g_norm1, g_norm2, w_in, b_fgate, g_fox_out, lb_logits, g_hg_out,
           w_out, w_router, b_router, w_gu, b_gu, w_down, b_down, g_final):
    n_batch, seq, d = x.shape
    t = n_batch * seq
    depth = w_ada.shape[0]
    fox_heads = b_fgate.shape[1]
    fox_w = g_fox_out.shape[1]
    hg_w = g_hg_out.shape[1]
    hg_heads = hg_w // HEAD_DIM
    n_exp = w_router.shape[2]
    assert fox_w == fox_heads * HEAD_DIM and fox_heads <= LANES
    n_blocks = -(-t * TOP_K // MOE_BLOCK) + n_exp
    n_mod = w_ada.shape[2] // d

    x2d = x.reshape(t, d)
    for l in range(depth):
        mod3 = _ada_mod(c, w_ada[l], b_ada[l]).reshape(n_batch, n_mod, d)

        w_l = w_in[l]
        w_main = jnp.concatenate([w_l[:, :3 * fox_w], w_l[:, 3 * fox_w + fox_heads:]], axis=1).astype(BF16)
        w_fg = jnp.pad(w_l[:, 3 * fox_w:3 * fox_w + fox_heads], ((0, 0), (0, LANES - fox_heads))).astype(BF16)
        b_fg = jnp.pad(b_fgate[l], (0, LANES - fox_heads)).reshape(1, LANES)
        proj, cum = _inproj(x2d, mod3, g_norm1[l].reshape(1, d), w_main, w_fg, b_fg, seq)
        proj3 = proj.reshape(n_batch, seq, -1)

        blk = min(512, seq)
        cum_keys = cum[:, :fox_heads].reshape(n_batch, seq, fox_heads).transpose(0, 2, 1)
        cum_keys = cum_keys.reshape(n_batch, fox_heads, seq // blk, 1, blk)
        fox = _fox_attention(proj3, cum_keys, fox_heads)

        lb3 = lb_logits.reshape(lb_logits.shape[0], hg_heads, HEAD_DIM).transpose(1, 0, 2)
        hg = _hgrn2(proj3, lb3, g_hg_out[l].reshape(hg_heads, 1, HEAD_DIM), hg_heads, 3 * fox_heads, l)

        x1, h2, logits_t = _outproj(
            fox.reshape(t, fox_w), hg.reshape(t, hg_w), x2d, mod3, g_fox_out[l].reshape(1, fox_w),
            g_norm2[l].reshape(1, d), w_out[l].astype(BF16), w_router[l].T, b_router[l].reshape(n_exp, 1), seq)

        pos_t, gates_t, counts = _route(logits_t)

        pos_flat = pos_t.T.reshape(t * TOP_K)
        tok = jnp.arange(t * TOP_K, dtype=jnp.int32) // TOP_K
        src = jnp.zeros((n_blocks * MOE_BLOCK,), jnp.int32).at[pos_flat].set(tok)
        cnt = counts[:, 0].astype(jnp.int32)
        padded = (cnt + MOE_BLOCK - 1) // MOE_BLOCK * MOE_BLOCK
        padded_end = jnp.cumsum(padded)
        block_row0 = jnp.arange(n_blocks, dtype=jnp.int32) * MOE_BLOCK
        block_e = jnp.minimum(jnp.searchsorted(padded_end, block_row0, side='right'), n_exp - 1).astype(jnp.int32)
        valid_end = (padded_end - padded + cnt)[block_e]
        nvalid = jnp.clip(valid_end - block_row0, 0, MOE_BLOCK).astype(jnp.int32)
        nvalid = jnp.where(block_row0 < padded_end[-1], nvalid, 0)
        n_used = (padded_end[-1:] // MOE_BLOCK).astype(jnp.int32)

        xs = _dispatch(h2, src.reshape(n_blocks, 1, MOE_BLOCK), nvalid)
        act = _gateup(block_e, n_used, xs, w_gu[l], b_gu[l].reshape(n_exp, 1, -1))
        y_buf = _down(block_e, n_used, act, w_down[l], b_down[l].reshape(n_exp, 1, d))

        tb = min(128, seq)
        x2d = _combine(pos_flat.reshape(t // tb, 1, tb * TOP_K), y_buf, gates_t.T, x1, mod3,
                       g_final.reshape(1, d), seq, l == depth - 1)
    return x2d.reshape(n_batch, seq, d)
```

```python
import functools

import jax
import jax.numpy as jnp
from jax import lax
from jax.experimental import pallas as pl
from jax.experimental.pallas import tpu as pltpu

HEAD_DIM = 128
TOP_K = 4
MOE_BLOCK = 256
HG_CHUNK = 64
NORM_EPS = 1e-6
SWIGLU_ALPHA = 1.702
SWIGLU_LIMIT = 7.0
LANES = 128
VMEM_LIMIT = 56 * 1024 * 1024

F32 = jnp.float32
BF16 = jnp.bfloat16
HIGHEST = lax.Precision.HIGHEST
NT_DIMS = (((1,), (1,)), ((), ()))
TN_DIMS = (((0,), (0,)), ((), ()))


def _params(*semantics):
    return pltpu.CompilerParams(dimension_semantics=semantics, vmem_limit_bytes=VMEM_LIMIT)


def _sigmoid(z):
    return 1.0 / (1.0 + jnp.exp(-z))


def _rms(v):
    return v * lax.rsqrt(jnp.mean(v * v, axis=-1, keepdims=True) + NORM_EPS)


def _ada_kernel(ct_ref, w_ref, b_ref, o_ref, *, n_batch, d_model):
    ct = ct_ref[...]
    cond = ct * _sigmoid(ct)
    rows = 256
    for b in range(n_batch):
        acc = jnp.zeros((8, w_ref.shape[1]), F32)
        for i in range(d_model // rows):
            w3 = w_ref[pl.ds(i * rows, rows), :].reshape(rows // 8, 8, -1)
            c3 = cond[i * rows:(i + 1) * rows, b:b + 1].reshape(rows // 8, 8, 1)
            acc = acc + jnp.sum(w3 * c3, axis=0)
        o_ref[pl.ds(b, 1), :] = jnp.sum(acc, axis=0, keepdims=True) + b_ref[...]


def _ada_mod(c, w_ada, b_ada):
    n_batch, d = c.shape
    n = w_ada.shape[1]
    tn = 1024
    return pl.pallas_call(
        functools.partial(_ada_kernel, n_batch=n_batch, d_model=d),
        out_shape=jax.ShapeDtypeStruct((n_batch, n), F32),
        grid=(n // tn,),
        in_specs=[pl.BlockSpec((d, n_batch), lambda j: (0, 0)),
                  pl.BlockSpec((d, tn), lambda j: (0, j)),
                  pl.BlockSpec((1, tn), lambda j: (0, j))],
        out_specs=pl.BlockSpec((n_batch, tn), lambda j: (0, j)),
        compiler_params=_params("arbitrary"),
        name="ada_mod",
    )(c.T, w_ada, b_ada.reshape(1, n))


def _inproj_kernel(x_ref, mod_ref, g_ref, w_ref, wf_ref, bf_ref, proj_ref, cum_ref, h_sc, carry_sc,
                   *, tiles_per_batch, sub):
    i = pl.program_id(0)
    j = pl.program_id(1)

    @pl.when(j == 0)
    def _():
        x = x_ref[...]
        h = _rms(x) * g_ref[...]
        h = h * (1.0 + mod_ref[0, 1:2, :]) + mod_ref[0, 0:1, :]
        hb = h.astype(BF16)
        h_sc[...] = hb
        z = jnp.dot(hb, wf_ref[...], preferred_element_type=F32) + bf_ref[...]
        logf = jnp.minimum(z, 0.0) - jnp.log(1.0 + jnp.exp(-jnp.abs(z)))

        @pl.when(i % tiles_per_batch == 0)
        def _():
            carry_sc[...] = jnp.zeros_like(carry_sc)

        r = lax.broadcasted_iota(jnp.int32, (sub, sub), 0)
        c = lax.broadcasted_iota(jnp.int32, (sub, sub), 1)
        tri = (r >= c).astype(F32)
        carry = carry_sc[...]
        for s in range(x.shape[0] // sub):
            blk = logf[s * sub:(s + 1) * sub, :]
            cs = jnp.dot(tri, blk, preferred_element_type=F32, precision=HIGHEST) + carry
            cum_ref[pl.ds(s * sub, sub), :] = cs
            carry = cs[sub - 1:sub, :]
        carry_sc[...] = carry

    proj_ref[...] = jnp.dot(h_sc[...], w_ref[...], preferred_element_type=F32).astype(BF16)


def _inproj(x2d, mod3, g1, w_main, w_fg, b_fg, seq):
    t, d = x2d.shape
    n = w_main.shape[1]
    tm = min(1024, seq)
    tn = next(c for c in (1024, 512, 256, 128) if n % c == 0)
    return pl.pallas_call(
        functools.partial(_inproj_kernel, tiles_per_batch=seq // tm, sub=256),
        out_shape=(jax.ShapeDtypeStruct((t, n), BF16), jax.ShapeDtypeStruct((t, LANES), F32)),
        grid=(t // tm, n // tn),
        in_specs=[pl.BlockSpec((tm, d), lambda i, j: (i, 0)),
                  pl.BlockSpec((1, mod3.shape[1], d), lambda i, j: (i // (seq // tm), 0, 0)),
                  pl.BlockSpec((1, d), lambda i, j: (0, 0)),
                  pl.BlockSpec((d, tn), lambda i, j: (0, j)),
                  pl.BlockSpec((d, LANES), lambda i, j: (0, 0)),
                  pl.BlockSpec((1, LANES), lambda i, j: (0, 0))],
        out_specs=(pl.BlockSpec((tm, tn), lambda i, j: (i, j)),
                   pl.BlockSpec((tm, LANES), lambda i, j: (i, 0))),
        scratch_shapes=[pltpu.VMEM((tm, d), BF16), pltpu.VMEM((1, LANES), F32)],
        compiler_params=_params("arbitrary", "arbitrary"),
        name="inproj",
    )(x2d, mod3, g1, w_main, w_fg, b_fg)


N_BIAS = 3
LOG2E = 1.4426950408889634


def _attn_kernel(q_ref, k_ref, v_ref, cum_ref, o_ref, kaug_sc, vt_sc, qt_sc, sa_sc, sb_sc, m_sc, l_sc, acc_sc,
                 *, blk, n_kv):
    h = pl.program_id(1)
    i = pl.program_id(2)

    @pl.when(i == 0)
    def _():
        lane = lax.broadcasted_iota(jnp.int32, (blk, LANES), 1)

        def prep(j, carry):
            start = pl.multiple_of(j * blk, blk)
            cum = cum_ref[0, pl.ds(start, blk), :]
            rest = jnp.sum(jnp.where(lane == h, cum, 0.0), axis=1, keepdims=True) * LOG2E
            bias = jnp.zeros((blk, LANES), F32)
            for piece in range(N_BIAS):
                part = rest.astype(BF16).astype(F32)
                bias = jnp.where(lane == piece, part, bias)
                rest = rest - part
            kaug_sc[j, :, 0:HEAD_DIM] = k_ref[0, pl.ds(start, blk), :]
            kaug_sc[j, :, HEAD_DIM:2 * HEAD_DIM] = bias.astype(BF16)
            vt_sc[j] = v_ref[0, pl.ds(start, blk), :].astype(F32).T.astype(BF16)
            return carry

        lax.fori_loop(0, n_kv, prep, 0)

    qs = q_ref[0].astype(F32) * (HEAD_DIM ** -0.5 * LOG2E)
    qt_sc[0:HEAD_DIM, :] = qs.T.astype(BF16)
    row = lax.broadcasted_iota(jnp.int32, (HEAD_DIM, blk), 0)
    qt_sc[HEAD_DIM:2 * HEAD_DIM, :] = jnp.where(row < N_BIAS, -1.0, 0.0).astype(BF16)
    m_sc[...] = jnp.full_like(m_sc, -jnp.inf)
    l_sc[...] = jnp.zeros_like(l_sc)
    acc_sc[...] = jnp.zeros_like(acc_sc)

    def scores(j):
        return jnp.dot(kaug_sc[j], qt_sc[...], preferred_element_type=F32)

    def update(j, s):
        m_prev = m_sc[...]
        m_new = jnp.maximum(m_prev, jnp.max(s, axis=0, keepdims=True))
        alpha = jnp.exp2(m_prev - m_new)
        p = jnp.exp2(s - m_new)
        l_sc[...] = alpha * l_sc[...] + jnp.sum(p, axis=0, keepdims=True)
        acc_sc[...] = alpha * acc_sc[...] + jnp.dot(vt_sc[j], p.astype(BF16), preferred_element_type=F32)
        m_sc[...] = m_new

    def causal(s):
        key = lax.broadcasted_iota(jnp.int32, (blk, blk), 0)
        qry = lax.broadcasted_iota(jnp.int32, (blk, blk), 1)
        return jnp.where(key <= qry, s, -jnp.inf)

    sa_sc[...] = scores(0)

    def pair(jj, carry):
        j = 2 * jj
        sb_sc[...] = scores(j + 1)
        update(j, sa_sc[...])
        sa_sc[...] = scores(j + 2)
        update(j + 1, sb_sc[...])
        return carry

    lax.fori_loop(0, i // 2, pair, 0)

    @pl.when(i % 2 == 0)
    def _():
        update(i, causal(sa_sc[...]))

    @pl.when(i % 2 == 1)
    def _():
        sb_sc[...] = scores(i)
        update(i - 1, sa_sc[...])
        update(i, causal(sb_sc[...]))

    o_ref[0] = (acc_sc[...] / l_sc[...]).T.astype(BF16)


def _fox_attention(proj3, cum3, n_heads):
    n_batch, seq, _ = proj3.shape
    blk = min(512, seq)
    n_kv = seq // blk
    return pl.pallas_call(
        functools.partial(_attn_kernel, blk=blk, n_kv=n_kv),
        out_shape=jax.ShapeDtypeStruct((n_batch, seq, n_heads * HEAD_DIM), BF16),
        grid=(n_batch, n_heads, n_kv),
        in_specs=[pl.BlockSpec((1, blk, HEAD_DIM), lambda b, h, i: (b, i, h)),
                  pl.BlockSpec((1, seq, HEAD_DIM), lambda b, h, i: (b, 0, n_heads + h)),
                  pl.BlockSpec((1, seq, HEAD_DIM), lambda b, h, i: (b, 0, 2 * n_heads + h)),
                  pl.BlockSpec((1, seq, LANES), lambda b, h, i: (b, 0, 0))],
        out_specs=pl.BlockSpec((1, blk, HEAD_DIM), lambda b, h, i: (b, i, h)),
        scratch_shapes=[pltpu.VMEM((n_kv, blk, 2 * HEAD_DIM), BF16),
                        pltpu.VMEM((n_kv, HEAD_DIM, blk), BF16),
                        pltpu.VMEM((2 * HEAD_DIM, blk), BF16),
                        pltpu.VMEM((blk, blk), F32), pltpu.VMEM((blk, blk), F32),
                        pltpu.VMEM((1, blk), F32), pltpu.VMEM((1, blk), F32),
                        pltpu.VMEM((HEAD_DIM, blk), F32)],
        compiler_params=_params("arbitrary", "arbitrary", "arbitrary"),
        name="fox_attention",
    )(proj3, proj3, proj3, cum3)


def _hgrn_kernel(q_ref, f_ref, i_ref, g_ref, lbl_ref, gn_ref, o_ref, st_sc, *, rows, sub, layer):
    @pl.when(pl.program_id(2) == 0)
    def _():
        st_sc[...] = jnp.zeros_like(st_sc)

    lbl = lbl_ref[0]
    e = jnp.exp(lbl - jnp.max(lbl, axis=0, keepdims=True))
    lb = jnp.sum(e[0:layer + 1, :], axis=0, keepdims=True) / jnp.sum(e, axis=0, keepdims=True)
    f = lb + (1.0 - lb) * _sigmoid(f_ref[0].astype(F32))
    logf = jnp.log(f)
    kk = 1.0 - f
    qf = q_ref[0].astype(F32)
    qq = qf * _sigmoid(qf)

    r = lax.broadcasted_iota(jnp.int32, (sub, sub), 0)
    c = lax.broadcasted_iota(jnp.int32, (sub, sub), 1)
    tri = ((r >= c) & (r // HG_CHUNK == c // HG_CHUNK)).astype(F32)
    rc = lax.broadcasted_iota(jnp.int32, (HG_CHUNK, HG_CHUNK), 0)
    cc = lax.broadcasted_iota(jnp.int32, (HG_CHUNK, HG_CHUNK), 1)
    causal = rc >= cc

    for s in range(rows // sub):
        bsub = jnp.dot(tri, logf[s * sub:(s + 1) * sub, :], preferred_element_type=F32, precision=HIGHEST)
        for n in range(sub // HG_CHUNK):
            lo = s * sub + n * HG_CHUNK
            b = bsub[n * HG_CHUNK:(n + 1) * HG_CHUNK, :]
            b_last = b[HG_CHUNK - 1:HG_CHUNK, :]
            kc = kk[lo:lo + HG_CHUNK, :]
            q_dec = (qq[lo:lo + HG_CHUNK, :] * jnp.exp(b)).astype(BF16)
            k_inv = (kc * jnp.exp(-b)).astype(BF16)
            k_tail = (kc * jnp.exp(b_last - b)).astype(BF16)
            vv = i_ref[0, pl.ds(lo, HG_CHUNK), :]
            attn = lax.dot_general(q_dec, k_inv, NT_DIMS, preferred_element_type=F32)
            attn = jnp.where(causal, attn, 0.0).astype(BF16)
            st = st_sc[...]
            o = jnp.dot(attn, vv, preferred_element_type=F32)
            o = o + lax.dot_general(q_dec, st.astype(BF16), NT_DIMS, preferred_element_type=F32)
            upd_t = lax.dot_general(vv, k_tail, TN_DIMS, preferred_element_type=F32)
            st_sc[...] = jnp.exp(b_last) * st + upd_t
            y = _rms(o) * gn_ref[0]
            gf = g_ref[0, pl.ds(lo, HG_CHUNK), :].astype(F32)
            o_ref[0, pl.ds(lo, HG_CHUNK), :] = (y * (gf * _sigmoid(gf))).astype(BF16)


def _hgrn2(proj3, lb_logits3, g_hg3, n_heads, col0, layer):
    n_batch, seq, _ = proj3.shape
    rows = min(512, seq)
    spec = lambda off: pl.BlockSpec((1, rows, HEAD_DIM), lambda b, h, r: (b, r, col0 + off * n_heads + h))
    return pl.pallas_call(
        functools.partial(_hgrn_kernel, rows=rows, sub=min(256, rows), layer=layer),
        out_shape=jax.ShapeDtypeStruct((n_batch, seq, n_heads * HEAD_DIM), BF16),
        grid=(n_batch, n_heads, seq // rows),
        in_specs=[spec(0), spec(1), spec(2), spec(3),
                  pl.BlockSpec((1, lb_logits3.shape[1], HEAD_DIM), lambda b, h, r: (h, 0, 0)),
                  pl.BlockSpec((1, 1, HEAD_DIM), lambda b, h, r: (h, 0, 0))],
        out_specs=pl.BlockSpec((1, rows, HEAD_DIM), lambda b, h, r: (b, r, h)),
        scratch_shapes=[pltpu.VMEM((HEAD_DIM, HEAD_DIM), F32)],
        compiler_params=_params("arbitrary", "arbitrary", "arbitrary"),
        name="hgrn2",
    )(proj3, proj3, proj3, proj3, lb_logits3, g_hg3)


def _outproj_kernel(fox_ref, hg_ref, x_ref, mod_ref, gfox_ref, g2_ref, w_ref, wr_ref, br_ref,
                    x1_ref, h2_ref, lg_ref, *, fox_w):
    fox = _rms(fox_ref[...].astype(F32)) * gfox_ref[...]
    mix = jnp.dot(fox.astype(BF16), w_ref[pl.ds(0, fox_w), :], preferred_element_type=F32)
    mix = mix + jnp.dot(hg_ref[...], w_ref[pl.ds(fox_w, w_ref.shape[0] - fox_w), :],
                        preferred_element_type=F32)
    x1 = x_ref[...] + mod_ref[0, 2:3, :] * mix
    x1_ref[...] = x1
    h2 = _rms(x1) * g2_ref[...]
    h2 = h2 * (1.0 + mod_ref[0, 4:5, :]) + mod_ref[0, 3:4, :]
    h2_ref[...] = h2
    lg_ref[...] = lax.dot_general(wr_ref[...], h2, NT_DIMS, preferred_element_type=F32,
                                  precision=HIGHEST) + br_ref[...]


def _outproj(fox2d, hg2d, x2d, mod3, g_fox, g2, w_out, w_router_t, b_router, seq):
    t, d = x2d.shape
    fox_w = fox2d.shape[1]
    n_exp = w_router_t.shape[0]
    tm = 256
    row = lambda i: (i, 0)
    const = lambda i: (0, 0)
    return pl.pallas_call(
        functools.partial(_outproj_kernel, fox_w=fox_w),
        out_shape=(jax.ShapeDtypeStruct((t, d), F32), jax.ShapeDtypeStruct((t, d), F32),
                   jax.ShapeDtypeStruct((n_exp, t), F32)),
        grid=(t // tm,),
        in_specs=[pl.BlockSpec((tm, fox_w), row),
                  pl.BlockSpec((tm, hg2d.shape[1]), row),
                  pl.BlockSpec((tm, d), row),
                  pl.BlockSpec((1, mod3.shape[1], d), lambda i: (i // (seq // tm), 0, 0)),
                  pl.BlockSpec((1, fox_w), const),
                  pl.BlockSpec((1, d), const),
                  pl.BlockSpec(w_out.shape, const),
                  pl.BlockSpec((n_exp, d), const),
                  pl.BlockSpec((n_exp, 1), const)],
        out_specs=(pl.BlockSpec((tm, d), row), pl.BlockSpec((tm, d), row),
                   pl.BlockSpec((n_exp, tm), lambda i: (0, i))),
        compiler_params=_params("arbitrary"),
        name="outproj_router",
    )(fox2d, hg2d, x2d, mod3, g_fox, g2, w_out, w_router_t, b_router)


def _route_kernel(lg_ref, pos_ref, gate_ref, cnt_ref, cnt_sc, run_sc, *, n_exp, tb):
    phase = pl.program_id(0)
    i = pl.program_id(1)

    @pl.when((phase == 0) & (i == 0))
    def _():
        cnt_sc[...] = jnp.zeros_like(cnt_sc)
        run_sc[...] = jnp.zeros_like(run_sc)

    logits = lg_ref[...]
    eidx = lax.broadcasted_iota(jnp.int32, (n_exp, tb), 0).astype(F32)
    work = logits
    vals, hots = [], []
    for _ in range(TOP_K):
        m = jnp.max(work, axis=0, keepdims=True)
        first = jnp.min(jnp.where(work == m, eidx, float(n_exp)), axis=0, keepdims=True)
        hot = eidx == first
        vals.append(m)
        hots.append(hot)
        work = jnp.where(hot, -jnp.inf, work)
    sel = hots[0] | hots[1] | hots[2] | hots[3]
    self32 = sel.astype(F32)

    @pl.when(phase == 0)
    def _():
        cnt_sc[...] += jnp.sum(self32, axis=1, keepdims=True)

    @pl.when(phase == 1)
    def _():
        cnt = cnt_sc[...]
        padded = jnp.ceil(cnt / MOE_BLOCK) * MOE_BLOCK
        r = lax.broadcasted_iota(jnp.int32, (n_exp, n_exp), 0)
        c = lax.broadcasted_iota(jnp.int32, (n_exp, n_exp), 1)
        strict = (c < r).astype(F32)
        pstart = jnp.dot(strict, jnp.broadcast_to(padded, (n_exp, LANES)), preferred_element_type=F32,
                         precision=HIGHEST)[:, 0:1]
        tr = lax.broadcasted_iota(jnp.int32, (tb, tb), 0)
        tc = lax.broadcasted_iota(jnp.int32, (tb, tb), 1)
        upper = (tr < tc).astype(BF16)
        rank = jnp.dot(sel.astype(BF16), upper, preferred_element_type=F32)
        base = pstart + run_sc[...] + rank
        exps = [jnp.exp(v - vals[0]) for v in vals]
        denom = exps[0] + exps[1] + exps[2] + exps[3]
        for k in range(TOP_K):
            pos_k = jnp.sum(jnp.where(hots[k], base, 0.0), axis=0, keepdims=True)
            pos_ref[pl.ds(k, 1), :] = pos_k.astype(jnp.int32)
            gate_ref[pl.ds(k, 1), :] = exps[k] / denom
        run_sc[...] += jnp.sum(self32, axis=1, keepdims=True)
        cnt_ref[...] = cnt


def _route(logits_t):
    n_exp, t = logits_t.shape
    tb = min(512, t)
    return pl.pallas_call(
        functools.partial(_route_kernel, n_exp=n_exp, tb=tb),
        out_shape=(jax.ShapeDtypeStruct((TOP_K, t), jnp.int32), jax.ShapeDtypeStruct((TOP_K, t), F32),
                   jax.ShapeDtypeStruct((n_exp, 1), F32)),
        grid=(2, t // tb),
        in_specs=[pl.BlockSpec((n_exp, tb), lambda p, i: (0, i))],
        out_specs=(pl.BlockSpec((TOP_K, tb), lambda p, i: (0, i * p)),
                   pl.BlockSpec((TOP_K, tb), lambda p, i: (0, i * p)),
                   pl.BlockSpec((n_exp, 1), lambda p, i: (0, 0))),
        scratch_shapes=[pltpu.VMEM((n_exp, 1), F32), pltpu.VMEM((n_exp, 1), F32)],
        compiler_params=_params("arbitrary", "arbitrary"),
        name="route_topk",
    )(logits_t)


def _row_copy(src_hbm, row, dst_vmem, slot, sem):
    return pltpu.make_async_copy(src_hbm.at[pl.ds(row, 1), :], dst_vmem.at[pl.ds(slot, 1), :], sem)


def _gather_rows(src_hbm, idx_ref, n_rows, dst_vmem, sem):
    def issue(r, carry):
        _row_copy(src_hbm, idx_ref[0, 0, r], dst_vmem, r, sem).start()
        return carry

    lax.fori_loop(0, n_rows, issue, 0, unroll=8)


def _wait_rows(src_hbm, n_rows, dst_vmem, sem):
    pltpu.make_async_copy(src_hbm.at[pl.ds(0, n_rows), :], dst_vmem, sem).wait()


def _dispatch_kernel(nvalid_ref, src_ref, src_next_ref, h_hbm, o_ref, buf, sem, *, n_blocks):
    b = pl.program_id(0)
    slot = b % 2

    @pl.when(b == 0)
    def _():
        _gather_rows(h_hbm, src_ref, MOE_BLOCK, buf.at[0], sem.at[0])

    @pl.when(b + 1 < n_blocks)
    def _():
        _gather_rows(h_hbm, src_next_ref, MOE_BLOCK, buf.at[1 - slot], sem.at[1 - slot])

    _wait_rows(h_hbm, MOE_BLOCK, buf.at[slot], sem.at[slot])
    rows = lax.broadcasted_iota(jnp.int32, (MOE_BLOCK, 1), 0)
    o_ref[...] = jnp.where(rows < nvalid_ref[b], buf[slot], 0.0).astype(BF16)


def _dispatch(h2, src3, nvalid):
    n_blocks = src3.shape[0]
    d = h2.shape[1]
    idx_spec = lambda step: pl.BlockSpec((1, 1, MOE_BLOCK), lambda b, nv: (jnp.minimum(b + step, n_blocks - 1), 0, 0),
                                         memory_space=pltpu.SMEM)
    return pl.pallas_call(
        functools.partial(_dispatch_kernel, n_blocks=n_blocks),
        out_shape=jax.ShapeDtypeStruct((n_blocks * MOE_BLOCK, d), BF16),
        grid_spec=pltpu.PrefetchScalarGridSpec(
            num_scalar_prefetch=1,
            grid=(n_blocks,),
            in_specs=[idx_spec(0), idx_spec(1), pl.BlockSpec(memory_space=pl.ANY)],
            out_specs=pl.BlockSpec((MOE_BLOCK, d), lambda b, nv: (b, 0)),
            scratch_shapes=[pltpu.VMEM((2, MOE_BLOCK, d), F32), pltpu.SemaphoreType.DMA((2,))]),
        compiler_params=_params("arbitrary"),
        name="moe_dispatch",
    )(nvalid, src3, src3, h2)


def _new_expert(be_ref, b):
    return (b == 0) | (be_ref[b] != be_ref[jnp.maximum(b - 1, 0)])


def _gateup_kernel(be_ref, nused_ref, x_ref, wg_ref, wl_ref, bg_ref, bl_ref, o_ref, wg_sc, wl_sc):
    b = pl.program_id(1)

    @pl.when(_new_expert(be_ref, b))
    def _():
        wg_sc[...] = wg_ref[0].astype(BF16)
        wl_sc[...] = wl_ref[0].astype(BF16)

    @pl.when(b < nused_ref[0])
    def _():
        x = x_ref[...]
        glu = jnp.dot(x, wg_sc[...], preferred_element_type=F32) + bg_ref[0]
        lin = jnp.dot(x, wl_sc[...], preferred_element_type=F32) + bl_ref[0]
        glu = jnp.minimum(glu, SWIGLU_LIMIT)
        lin = jnp.clip(lin, -SWIGLU_LIMIT, SWIGLU_LIMIT)
        o_ref[...] = (glu * _sigmoid(SWIGLU_ALPHA * glu) * (lin + 1.0)).astype(BF16)

    @pl.when(b >= nused_ref[0])
    def _():
        o_ref[...] = jnp.zeros_like(o_ref)


def _gateup(block_e, n_used, xs, w_gu, b_gu3):
    n_rows, d = xs.shape
    n_blocks = n_rows // MOE_BLOCK
    d_ff = w_gu.shape[2] // 2
    tf = min(1024, d_ff)
    nf = d_ff // tf
    return pl.pallas_call(
        _gateup_kernel,
        out_shape=jax.ShapeDtypeStruct((n_rows, d_ff), BF16),
        grid_spec=pltpu.PrefetchScalarGridSpec(
            num_scalar_prefetch=2,
            grid=(nf, n_blocks),
            in_specs=[pl.BlockSpec((MOE_BLOCK, d), lambda c, b, be, nu: (b, 0)),
                      pl.BlockSpec((1, d, tf), lambda c, b, be, nu: (be[b], 0, c)),
                      pl.BlockSpec((1, d, tf), lambda c, b, be, nu: (be[b], 0, nf + c)),
                      pl.BlockSpec((1, 1, tf), lambda c, b, be, nu: (be[b], 0, c)),
                      pl.BlockSpec((1, 1, tf), lambda c, b, be, nu: (be[b], 0, nf + c))],
            out_specs=pl.BlockSpec((MOE_BLOCK, tf), lambda c, b, be, nu: (b, c)),
            scratch_shapes=[pltpu.VMEM((d, tf), BF16), pltpu.VMEM((d, tf), BF16)]),
        compiler_params=_params("arbitrary", "arbitrary"),
        name="moe_gateup",
    )(block_e, n_used, xs, w_gu, w_gu, b_gu3, b_gu3)


def _down_kernel(be_ref, nused_ref, a_ref, w_ref, b_ref, o_ref, w_sc):
    b = pl.program_id(1)

    @pl.when(_new_expert(be_ref, b))
    def _():
        w_sc[...] = w_ref[0].astype(BF16)

    @pl.when(b < nused_ref[0])
    def _():
        o_ref[...] = jnp.dot(a_ref[...], w_sc[...], preferred_element_type=F32) + b_ref[0]

    @pl.when(b >= nused_ref[0])
    def _():
        o_ref[...] = jnp.zeros_like(o_ref)


def _down(block_e, n_used, act, w_down, b_down3):
    n_rows, d_ff = act.shape
    n_blocks = n_rows // MOE_BLOCK
    d = w_down.shape[2]
    tn = min(2048, d)
    return pl.pallas_call(
        _down_kernel,
        out_shape=jax.ShapeDtypeStruct((n_rows, d), F32),
        grid_spec=pltpu.PrefetchScalarGridSpec(
            num_scalar_prefetch=2,
            grid=(d // tn, n_blocks),
            in_specs=[pl.BlockSpec((MOE_BLOCK, d_ff), lambda c, b, be, nu: (b, 0)),
                      pl.BlockSpec((1, d_ff, tn), lambda c, b, be, nu: (be[b], 0, c)),
                      pl.BlockSpec((1, 1, tn), lambda c, b, be, nu: (be[b], 0, c))],
            out_specs=pl.BlockSpec((MOE_BLOCK, tn), lambda c, b, be, nu: (b, c)),
            scratch_shapes=[pltpu.VMEM((d_ff, tn), BF16)]),
        compiler_params=_params("arbitrary", "arbitrary"),
        name="moe_down",
    )(block_e, n_used, act, w_down, b_down3)


def _combine_kernel(pos_ref, pos_next_ref, y_hbm, gate_ref, x1_ref, mod_ref, gfin_ref, o_ref, buf, sem,
                    *, tb, n_tiles, last_layer):
    i = pl.program_id(0)
    slot = i % 2

    @pl.when(i == 0)
    def _():
        _gather_rows(y_hbm, pos_ref, TOP_K * tb, buf.at[0], sem.at[0])

    @pl.when(i + 1 < n_tiles)
    def _():
        _gather_rows(y_hbm, pos_next_ref, TOP_K * tb, buf.at[1 - slot], sem.at[1 - slot])

    _wait_rows(y_hbm, TOP_K * tb, buf.at[slot], sem.at[slot])
    gates = gate_ref[...]
    moe = gates[:, 0:1] * buf[slot, pl.ds(0, tb), :]
    for k in range(1, TOP_K):
        moe = moe + gates[:, k:k + 1] * buf[slot, pl.ds(k * tb, tb), :]
    x2 = x1_ref[...] + mod_ref[0, 5:6, :] * moe
    o_ref[...] = _rms(x2) * gfin_ref[...] if last_layer else x2


def _combine(pos3, y_buf, gates, x1, mod3, g_final, seq, last_layer):
    t, d = x1.shape
    n_tiles = pos3.shape[0]
    tb = pos3.shape[2] // TOP_K
    idx_spec = lambda step: pl.BlockSpec((1, 1, tb * TOP_K), lambda i: (jnp.minimum(i + step, n_tiles - 1), 0, 0),
                                         memory_space=pltpu.SMEM)
    return pl.pallas_call(
        functools.partial(_combine_kernel, tb=tb, n_tiles=n_tiles, last_layer=last_layer),
        out_shape=jax.ShapeDtypeStruct((t, d), F32),
        grid=(n_tiles,),
        in_specs=[idx_spec(0), idx_spec(1),
                  pl.BlockSpec(memory_space=pl.ANY),
                  pl.BlockSpec((tb, TOP_K), lambda i: (i, 0)),
                  pl.BlockSpec((tb, d), lambda i: (i, 0)),
                  pl.BlockSpec((1, mod3.shape[1], d), lambda i: (i // (seq // tb), 0, 0)),
                  pl.BlockSpec((1, d), lambda i: (0, 0))],
        out_specs=pl.BlockSpec((tb, d), lambda i: (i, 0)),
        scratch_shapes=[pltpu.VMEM((2, TOP_K * tb, d), F32), pltpu.SemaphoreType.DMA((2,))],
        compiler_params=_params("arbitrary"),
        name="moe_combine",
    )(pos3, pos3, y_buf, gates, x1, mod3, g_final)


def kernel(x, c, w_ada, b_ada, g_norm1, g_norm2, w_in, b_fgate, g_fox_out, lb_logits, g_hg_out,
           w_out, w_router, b_router, w_gu, b_gu, w_down, b_down, g_final):
    n_batch, seq, d = x.shape
    t = n_batch * seq
    depth = w_ada.shape[0]
    fox_heads = b_fgate.shape[1]
    fox_w = g_fox_out.shape[1]
    hg_w = g_hg_out.shape[1]
    hg_heads = hg_w // HEAD_DIM
    n_exp = w_router.shape[2]
    assert fox_w == fox_heads * HEAD_DIM and fox_heads <= LANES
    n_blocks = -(-t * TOP_K // MOE_BLOCK) + n_exp
    n_mod = w_ada.shape[2] // d

    x2d = x.reshape(t, d)
    for l in range(depth):
        mod3 = _ada_mod(c, w_ada[l], b_ada[l]).reshape(n_batch, n_mod, d)

        w_l = w_in[l]
        w_main = jnp.concatenate([w_l[:, :3 * fox_w], w_l[:, 3 * fox_w + fox_heads:]], axis=1).astype(BF16)
        w_fg = jnp.pad(w_l[:, 3 * fox_w:3 * fox_w + fox_heads], ((0, 0), (0, LANES - fox_heads))).astype(BF16)
        b_fg = jnp.pad(b_fgate[l], (0, LANES - fox_heads)).reshape(1, LANES)
        proj, cum = _inproj(x2d, mod3, g_norm1[l].reshape(1, d), w_main, w_fg, b_fg, seq)
        proj3 = proj.reshape(n_batch, seq, -1)

        fox = _fox_attention(proj3, cum.reshape(n_batch, seq, LANES), fox_heads)

        lb3 = lb_logits.reshape(lb_logits.shape[0], hg_heads, HEAD_DIM).transpose(1, 0, 2)
        hg = _hgrn2(proj3, lb3, g_hg_out[l].reshape(hg_heads, 1, HEAD_DIM), hg_heads, 3 * fox_heads, l)

        x1, h2, logits_t = _outproj(
            fox.reshape(t, fox_w), hg.reshape(t, hg_w), x2d, mod3, g_fox_out[l].reshape(1, fox_w),
            g_norm2[l].reshape(1, d), w_out[l].astype(BF16), w_router[l].T, b_router[l].reshape(n_exp, 1), seq)

        pos_t, gates_t, counts = _route(logits_t)

        tok = jnp.tile(jnp.arange(t, dtype=jnp.int32), TOP_K)
        src = jnp.zeros((n_blocks * MOE_BLOCK,), jnp.int32).at[pos_t.reshape(-1)].set(
            tok, unique_indices=True, indices_are_sorted=False)
        cnt = counts[:, 0].astype(jnp.int32)
        padded = (cnt + MOE_BLOCK - 1) // MOE_BLOCK * MOE_BLOCK
        padded_end = jnp.cumsum(padded)
        block_row0 = jnp.arange(n_blocks, dtype=jnp.int32) * MOE_BLOCK
        block_e = jnp.sum((block_row0[:, None] >= padded_end[None, :]).astype(jnp.int32), axis=1)
        block_e = jnp.minimum(block_e, n_exp - 1)
        valid_end = jnp.sum(jnp.where(block_e[:, None] == jnp.arange(n_exp)[None, :],
                                      (padded_end - padded + cnt)[None, :], 0), axis=1)
        nvalid = jnp.clip(valid_end - block_row0, 0, MOE_BLOCK).astype(jnp.int32)
        nvalid = jnp.where(block_row0 < padded_end[-1], nvalid, 0)
        n_used = (padded_end[-1:] // MOE_BLOCK).astype(jnp.int32)

        xs = _dispatch(h2, src.reshape(n_blocks, 1, MOE_BLOCK), nvalid)
        act = _gateup(block_e, n_used, xs, w_gu[l], b_gu[l].reshape(n_exp, 1, -1))
        y_buf = _down(block_e, n_used, act, w_down[l], b_down[l].reshape(n_exp, 1, d))

        tb = min(128, seq)
        pos3 = pos_t.reshape(TOP_K, t // tb, tb).transpose(1, 0, 2).reshape(t // tb, 1, TOP_K * tb)
        x2d = _combine(pos3, y_buf, gates_t.T, x1, mod3, g_final.reshape(1, d), seq, l == depth - 1)
    return x2d.reshape(n_batch, seq, d)
```

```python
import functools

import jax
import jax.numpy as jnp
from jax import lax
from jax.experimental import pallas as pl
from jax.experimental.pallas import tpu as pltpu

HEAD_DIM = 128
TOP_K = 4
MOE_BLOCK = 256
HG_CHUNK = 64
NORM_EPS = 1e-6
SWIGLU_ALPHA = 1.702
SWIGLU_LIMIT = 7.0
LANES = 128
VMEM_LIMIT = 56 * 1024 * 1024

F32 = jnp.float32
BF16 = jnp.bfloat16
HIGHEST = lax.Precision.HIGHEST
NT_DIMS = (((1,), (1,)), ((), ()))
TN_DIMS = (((0,), (0,)), ((), ()))


def _params(*semantics):
    return pltpu.CompilerParams(dimension_semantics=semantics, vmem_limit_bytes=VMEM_LIMIT)


def _sigmoid(z):
    return 1.0 / (1.0 + jnp.exp(-z))


def _rms(v):
    return v * lax.rsqrt(jnp.mean(v * v, axis=-1, keepdims=True) + NORM_EPS)


def _ada_kernel(ct_ref, w_ref, b_ref, o_ref, *, n_batch, d_model):
    ct = ct_ref[...]
    cond = ct * _sigmoid(ct)
    rows = 256
    for b in range(n_batch):
        acc = jnp.zeros((8, w_ref.shape[1]), F32)
        for i in range(d_model // rows):
            w3 = w_ref[pl.ds(i * rows, rows), :].reshape(rows // 8, 8, -1)
            c3 = cond[i * rows:(i + 1) * rows, b:b + 1].reshape(rows // 8, 8, 1)
            acc = acc + jnp.sum(w3 * c3, axis=0)
        o_ref[pl.ds(b, 1), :] = jnp.sum(acc, axis=0, keepdims=True) + b_ref[...]


def _ada_mod(c, w_ada, b_ada):
    n_batch, d = c.shape
    n = w_ada.shape[1]
    tn = 1024
    return pl.pallas_call(
        functools.partial(_ada_kernel, n_batch=n_batch, d_model=d),
        out_shape=jax.ShapeDtypeStruct((n_batch, n), F32),
        grid=(n // tn,),
        in_specs=[pl.BlockSpec((d, n_batch), lambda j: (0, 0)),
                  pl.BlockSpec((d, tn), lambda j: (0, j)),
                  pl.BlockSpec((1, tn), lambda j: (0, j))],
        out_specs=pl.BlockSpec((n_batch, tn), lambda j: (0, j)),
        compiler_params=_params("arbitrary"),
        name="ada_mod",
    )(c.T, w_ada, b_ada.reshape(1, n))


def _inproj_kernel(x_ref, mod_ref, g_ref, w_ref, wf_ref, bf_ref, proj_ref, cum_ref, h_sc, carry_sc,
                   *, tiles_per_batch, sub):
    i = pl.program_id(0)
    j = pl.program_id(1)

    @pl.when(j == 0)
    def _():
        x = x_ref[...]
        h = _rms(x) * g_ref[...]
        h = h * (1.0 + mod_ref[0, 1:2, :]) + mod_ref[0, 0:1, :]
        hb = h.astype(BF16)
        h_sc[...] = hb
        z = jnp.dot(hb, wf_ref[...], preferred_element_type=F32) + bf_ref[...]
        logf = jnp.minimum(z, 0.0) - jnp.log(1.0 + jnp.exp(-jnp.abs(z)))

        @pl.when(i % tiles_per_batch == 0)
        def _():
            carry_sc[...] = jnp.zeros_like(carry_sc)

        r = lax.broadcasted_iota(jnp.int32, (sub, sub), 0)
        c = lax.broadcasted_iota(jnp.int32, (sub, sub), 1)
        tri = (r >= c).astype(F32)
        carry = carry_sc[...]
        for s in range(x.shape[0] // sub):
            blk = logf[s * sub:(s + 1) * sub, :]
            cs = jnp.dot(tri, blk, preferred_element_type=F32, precision=HIGHEST) + carry
            cum_ref[pl.ds(s * sub, sub), :] = cs
            carry = cs[sub - 1:sub, :]
        carry_sc[...] = carry

    proj_ref[...] = jnp.dot(h_sc[...], w_ref[...], preferred_element_type=F32).astype(BF16)


def _inproj(x2d, mod3, g1, w_main, w_fg, b_fg, seq):
    t, d = x2d.shape
    n = w_main.shape[1]
    tm = min(1024, seq)
    tn = next(c for c in (1024, 512, 256, 128) if n % c == 0)
    return pl.pallas_call(
        functools.partial(_inproj_kernel, tiles_per_batch=seq // tm, sub=256),
        out_shape=(jax.ShapeDtypeStruct((t, n), BF16), jax.ShapeDtypeStruct((t, LANES), F32)),
        grid=(t // tm, n // tn),
        in_specs=[pl.BlockSpec((tm, d), lambda i, j: (i, 0)),
                  pl.BlockSpec((1, mod3.shape[1], d), lambda i, j: (i // (seq // tm), 0, 0)),
                  pl.BlockSpec((1, d), lambda i, j: (0, 0)),
                  pl.BlockSpec((d, tn), lambda i, j: (0, j)),
                  pl.BlockSpec((d, LANES), lambda i, j: (0, 0)),
                  pl.BlockSpec((1, LANES), lambda i, j: (0, 0))],
        out_specs=(pl.BlockSpec((tm, tn), lambda i, j: (i, j)),
                   pl.BlockSpec((tm, LANES), lambda i, j: (i, 0))),
        scratch_shapes=[pltpu.VMEM((tm, d), BF16), pltpu.VMEM((1, LANES), F32)],
        compiler_params=_params("arbitrary", "arbitrary"),
        name="inproj",
    )(x2d, mod3, g1, w_main, w_fg, b_fg)


N_BIAS = 3
LOG2E = 1.4426950408889634


def _attn_kernel(q_ref, k_ref, v_ref, cum_ref, o_ref, kaug_sc, vt_sc, qt_sc, sa_sc, sb_sc, m_sc, l_sc, acc_sc,
                 *, blk, n_kv):
    h = pl.program_id(1)
    i = pl.program_id(2)

    @pl.when(i == 0)
    def _():
        lane = lax.broadcasted_iota(jnp.int32, (blk, LANES), 1)

        def prep(j, carry):
            start = pl.multiple_of(j * blk, blk)
            cum = cum_ref[0, pl.ds(start, blk), :]
            rest = jnp.sum(jnp.where(lane == h, cum, 0.0), axis=1, keepdims=True) * LOG2E
            bias = jnp.zeros((blk, LANES), F32)
            for piece in range(N_BIAS):
                part = rest.astype(BF16).astype(F32)
                bias = jnp.where(lane == piece, part, bias)
                rest = rest - part
            kaug_sc[j, :, 0:HEAD_DIM] = k_ref[0, pl.ds(start, blk), :]
            kaug_sc[j, :, HEAD_DIM:2 * HEAD_DIM] = bias.astype(BF16)
            vt_sc[j] = v_ref[0, pl.ds(start, blk), :].astype(F32).T.astype(BF16)
            return carry

        lax.fori_loop(0, n_kv, prep, 0)

    qs = q_ref[0].astype(F32) * (HEAD_DIM ** -0.5 * LOG2E)
    qt_sc[0:HEAD_DIM, :] = qs.T.astype(BF16)
    row = lax.broadcasted_iota(jnp.int32, (HEAD_DIM, blk), 0)
    qt_sc[HEAD_DIM:2 * HEAD_DIM, :] = jnp.where(row < N_BIAS, -1.0, 0.0).astype(BF16)
    m_sc[...] = jnp.full_like(m_sc, -jnp.inf)
    l_sc[...] = jnp.zeros_like(l_sc)
    acc_sc[...] = jnp.zeros_like(acc_sc)

    def scores(j):
        return jnp.dot(kaug_sc[j], qt_sc[...], preferred_element_type=F32)

    def update(j, s):
        m_prev = m_sc[...]
        m_new = jnp.maximum(m_prev, jnp.max(s, axis=0, keepdims=True))
        alpha = jnp.exp2(m_prev - m_new)
        p = jnp.exp2(s - m_new)
        l_sc[...] = alpha * l_sc[...] + jnp.sum(p, axis=0, keepdims=True)
        acc_sc[...] = alpha * acc_sc[...] + jnp.dot(vt_sc[j], p.astype(BF16), preferred_element_type=F32)
        m_sc[...] = m_new

    def causal(s):
        key = lax.broadcasted_iota(jnp.int32, (blk, blk), 0)
        qry = lax.broadcasted_iota(jnp.int32, (blk, blk), 1)
        return jnp.where(key <= qry, s, -jnp.inf)

    sa_sc[...] = scores(0)

    def pair(jj, carry):
        j = 2 * jj
        sb_sc[...] = scores(j + 1)
        update(j, sa_sc[...])
        sa_sc[...] = scores(j + 2)
        update(j + 1, sb_sc[...])
        return carry

    lax.fori_loop(0, i // 2, pair, 0)

    @pl.when(i % 2 == 0)
    def _():
        update(i, causal(sa_sc[...]))

    @pl.when(i % 2 == 1)
    def _():
        sb_sc[...] = scores(i)
        update(i - 1, sa_sc[...])
        update(i, causal(sb_sc[...]))

    o_ref[0] = (acc_sc[...] / l_sc[...]).T.astype(BF16)


def _fox_attention(proj3, cum3, n_heads):
    n_batch, seq, _ = proj3.shape
    blk = min(512, seq)
    n_kv = seq // blk
    return pl.pallas_call(
        functools.partial(_attn_kernel, blk=blk, n_kv=n_kv),
        out_shape=jax.ShapeDtypeStruct((n_batch, seq, n_heads * HEAD_DIM), BF16),
        grid=(n_batch, n_heads, n_kv),
        in_specs=[pl.BlockSpec((1, blk, HEAD_DIM), lambda b, h, i: (b, i, h)),
                  pl.BlockSpec((1, seq, HEAD_DIM), lambda b, h, i: (b, 0, n_heads + h)),
                  pl.BlockSpec((1, seq, HEAD_DIM), lambda b, h, i: (b, 0, 2 * n_heads + h)),
                  pl.BlockSpec((1, seq, LANES), lambda b, h, i: (b, 0, 0))],
        out_specs=pl.BlockSpec((1, blk, HEAD_DIM), lambda b, h, i: (b, i, h)),
        scratch_shapes=[pltpu.VMEM((n_kv, blk, 2 * HEAD_DIM), BF16),
                        pltpu.VMEM((n_kv, HEAD_DIM, blk), BF16),
                        pltpu.VMEM((2 * HEAD_DIM, blk), BF16),
                        pltpu.VMEM((blk, blk), F32), pltpu.VMEM((blk, blk), F32),
                        pltpu.VMEM((1, blk), F32), pltpu.VMEM((1, blk), F32),
                        pltpu.VMEM((HEAD_DIM, blk), F32)],
        compiler_params=_params("arbitrary", "arbitrary", "arbitrary"),
        name="fox_attention",
    )(proj3, proj3, proj3, cum3)


def _hgrn_kernel(q_ref, f_ref, i_ref, g_ref, lbl_ref, gn_ref, o_ref, st_sc, *, rows, sub, layer):
    @pl.when(pl.program_id(2) == 0)
    def _():
        st_sc[...] = jnp.zeros_like(st_sc)

    lbl = lbl_ref[0]
    e = jnp.exp(lbl - jnp.max(lbl, axis=0, keepdims=True))
    lb = jnp.sum(e[0:layer + 1, :], axis=0, keepdims=True) / jnp.sum(e, axis=0, keepdims=True)
    f = lb + (1.0 - lb) * _sigmoid(f_ref[0].astype(F32))
    logf = jnp.log(f)
    kk = 1.0 - f
    qf = q_ref[0].astype(F32)
    qq = qf * _sigmoid(qf)

    n_ch = sub // HG_CHUNK
    r = lax.broadcasted_iota(jnp.int32, (sub, sub), 0)
    c = lax.broadcasted_iota(jnp.int32, (sub, sub), 1)
    within = (r >= c) & (r // HG_CHUNK == c // HG_CHUNK)
    tri = within.astype(BF16)
    rw = lax.broadcasted_iota(jnp.int32, (sub, n_ch * HEAD_DIM), 0)
    cw = lax.broadcasted_iota(jnp.int32, (sub, n_ch * HEAD_DIM), 1)
    own_block = rw // HG_CHUNK == cw // HEAD_DIM

    st = st_sc[...]
    for s in range(rows // sub):
        rs = slice(s * sub, (s + 1) * sub)
        lf = logf[rs, :]
        hi = lf.astype(BF16)
        rest = lf - hi.astype(F32)
        mid = rest.astype(BF16)
        low = (rest - mid.astype(F32)).astype(BF16)
        two = jnp.dot(tri, jnp.concatenate([hi, mid], axis=1), preferred_element_type=F32)
        b = two[:, :HEAD_DIM] + two[:, HEAD_DIM:] + jnp.dot(tri, low, preferred_element_type=F32)
        b_last = [b[(n + 1) * HG_CHUNK - 1:(n + 1) * HG_CHUNK, :] for n in range(n_ch)]
        b_last_rows = jnp.concatenate([jnp.broadcast_to(bl, (HG_CHUNK, HEAD_DIM)) for bl in b_last], axis=0)
        q_dec = (qq[rs, :] * jnp.exp(b)).astype(BF16)
        k_inv = (kk[rs, :] * jnp.exp(-b)).astype(BF16)
        k_tail = (kk[rs, :] * jnp.exp(b_last_rows - b)).astype(BF16)
        vv = i_ref[0, pl.ds(s * sub, sub), :]
        attn = lax.dot_general(q_dec, k_inv, NT_DIMS, preferred_element_type=F32)
        o = jnp.dot(jnp.where(within, attn, 0.0).astype(BF16), vv, preferred_element_type=F32)
        zero = jnp.zeros((), BF16)
        k_blocks = jnp.where(own_block, jnp.concatenate([k_tail] * n_ch, axis=1), zero)
        upd = lax.dot_general(vv, k_blocks, TN_DIMS, preferred_element_type=F32)
        states = []
        for n in range(n_ch):
            states.append(st.astype(BF16))
            st = jnp.exp(b_last[n]) * st + upd[:, n * HEAD_DIM:(n + 1) * HEAD_DIM]
        q_blocks = jnp.where(own_block, jnp.concatenate([q_dec] * n_ch, axis=1), zero)
        o = o + lax.dot_general(q_blocks, jnp.concatenate(states, axis=1), NT_DIMS, preferred_element_type=F32)
        y = _rms(o) * gn_ref[0]
        gf = g_ref[0, pl.ds(s * sub, sub), :].astype(F32)
        o_ref[0, pl.ds(s * sub, sub), :] = (y * (gf * _sigmoid(gf))).astype(BF16)
    st_sc[...] = st


def _hgrn2(proj3, lb_logits3, g_hg3, n_heads, col0, layer):
    n_batch, seq, _ = proj3.shape
    rows = min(512, seq)
    spec = lambda off: pl.BlockSpec((1, rows, HEAD_DIM), lambda b, h, r: (b, r, col0 + off * n_heads + h))
    return pl.pallas_call(
        functools.partial(_hgrn_kernel, rows=rows, sub=min(256, rows), layer=layer),
        out_shape=jax.ShapeDtypeStruct((n_batch, seq, n_heads * HEAD_DIM), BF16),
        grid=(n_batch, n_heads, seq // rows),
        in_specs=[spec(0), spec(1), spec(2), spec(3),
                  pl.BlockSpec((1, lb_logits3.shape[1], HEAD_DIM), lambda b, h, r: (h, 0, 0)),
                  pl.BlockSpec((1, 1, HEAD_DIM), lambda b, h, r: (h, 0, 0))],
        out_specs=pl.BlockSpec((1, rows, HEAD_DIM), lambda b, h, r: (b, r, h)),
        scratch_shapes=[pltpu.VMEM((HEAD_DIM, HEAD_DIM), F32)],
        compiler_params=_params("arbitrary", "arbitrary", "arbitrary"),
        name="hgrn2",
    )(proj3, proj3, proj3, proj3, lb_logits3, g_hg3)


def _pack_halves(v):
    n = v.shape[1] // 2
    lo = lax.bitcast_convert_type(v[:, :n].astype(BF16).astype(F32), jnp.uint32) >> 16
    hi = lax.bitcast_convert_type(v[:, n:].astype(BF16).astype(F32), jnp.uint32) & jnp.uint32(0xFFFF0000)
    return lo | hi


def _unpack_halves(p):
    lo = lax.bitcast_convert_type(p << 16, F32).astype(BF16)
    hi = lax.bitcast_convert_type(p & jnp.uint32(0xFFFF0000), F32).astype(BF16)
    return lo, hi


def _outproj_kernel(fox_ref, hg_ref, x_ref, mod_ref, gfox_ref, g2_ref, w_ref, wr_ref, br_ref,
                    x1_ref, h2_ref, lg_ref, *, n_exp):
    fox = _rms(fox_ref[...].astype(F32)) * gfox_ref[...]
    mixed = jnp.concatenate([fox.astype(BF16), hg_ref[...]], axis=1)
    mix = jnp.dot(mixed, w_ref[...], preferred_element_type=F32)
    x1 = x_ref[...] + mod_ref[0, 2:3, :] * mix
    x1_ref[...] = x1
    h2 = _rms(x1) * g2_ref[...]
    h2 = h2 * (1.0 + mod_ref[0, 4:5, :]) + mod_ref[0, 3:4, :]
    h2_ref[...] = _pack_halves(h2)
    h_hi = h2.astype(BF16)
    h_lo = (h2 - h_hi.astype(F32)).astype(BF16)
    part = jnp.dot(h_hi, wr_ref[...], preferred_element_type=F32)
    part = part + jnp.dot(h_lo, wr_ref[...], preferred_element_type=F32)
    logits = part + pltpu.roll(part, LANES - n_exp, axis=1)
    lg_ref[...] = logits.T[0:n_exp, :] + br_ref[...]


def _outproj(fox2d, hg2d, x2d, mod3, g_fox, g2, w_out, w_router, b_router, seq):
    t, d = x2d.shape
    fox_w = fox2d.shape[1]
    n_exp = w_router.shape[1]
    assert 2 * n_exp <= LANES
    wr_hi = w_router.astype(BF16)
    wr_lo = (w_router - wr_hi.astype(F32)).astype(BF16)
    wr_cat = jnp.pad(jnp.concatenate([wr_hi, wr_lo], axis=1), ((0, 0), (0, LANES - 2 * n_exp)))
    tm = 256
    row = lambda i: (i, 0)
    const = lambda i: (0, 0)
    return pl.pallas_call(
        functools.partial(_outproj_kernel, n_exp=n_exp),
        out_shape=(jax.ShapeDtypeStruct((t, d), F32), jax.ShapeDtypeStruct((t, d // 2), jnp.uint32),
                   jax.ShapeDtypeStruct((n_exp, t), F32)),
        grid=(t // tm,),
        in_specs=[pl.BlockSpec((tm, fox_w), row),
                  pl.BlockSpec((tm, hg2d.shape[1]), row),
                  pl.BlockSpec((tm, d), row),
                  pl.BlockSpec((1, mod3.shape[1], d), lambda i: (i // (seq // tm), 0, 0)),
                  pl.BlockSpec((1, fox_w), const),
                  pl.BlockSpec((1, d), const),
                  pl.BlockSpec(w_out.shape, const),
                  pl.BlockSpec((d, LANES), const),
                  pl.BlockSpec((n_exp, 1), const)],
        out_specs=(pl.BlockSpec((tm, d), row), pl.BlockSpec((tm, d // 2), row),
                   pl.BlockSpec((n_exp, tm), lambda i: (0, i))),
        compiler_params=_params("arbitrary"),
        name="outproj_router",
    )(fox2d, hg2d, x2d, mod3, g_fox, g2, w_out, wr_cat, b_router)


def _route_kernel(lg_ref, pos_ref, gate_ref, cnt_ref, cnt_sc, run_sc, *, n_exp, tb):
    phase = pl.program_id(0)
    i = pl.program_id(1)

    @pl.when((phase == 0) & (i == 0))
    def _():
        cnt_sc[...] = jnp.zeros_like(cnt_sc)
        run_sc[...] = jnp.zeros_like(run_sc)

    logits = lg_ref[...]
    eidx = lax.broadcasted_iota(jnp.int32, (n_exp, tb), 0).astype(F32)
    work = logits
    vals, hots = [], []
    for _ in range(TOP_K):
        m = jnp.max(work, axis=0, keepdims=True)
        first = jnp.min(jnp.where(work == m, eidx, float(n_exp)), axis=0, keepdims=True)
        hot = eidx == first
        vals.append(m)
        hots.append(hot)
        work = jnp.where(hot, -jnp.inf, work)
    sel = hots[0] | hots[1] | hots[2] | hots[3]
    self32 = sel.astype(F32)

    @pl.when(phase == 0)
    def _():
        cnt_sc[...] += jnp.sum(self32, axis=1, keepdims=True)

    @pl.when(phase == 1)
    def _():
        cnt = cnt_sc[...]
        padded = jnp.ceil(cnt / MOE_BLOCK) * MOE_BLOCK
        r = lax.broadcasted_iota(jnp.int32, (n_exp, n_exp), 0)
        c = lax.broadcasted_iota(jnp.int32, (n_exp, n_exp), 1)
        strict = (c < r).astype(F32)
        pstart = jnp.dot(strict, jnp.broadcast_to(padded, (n_exp, LANES)), preferred_element_type=F32,
                         precision=HIGHEST)[:, 0:1]
        tr = lax.broadcasted_iota(jnp.int32, (tb, tb), 0)
        tc = lax.broadcasted_iota(jnp.int32, (tb, tb), 1)
        upper = (tr < tc).astype(BF16)
        rank = jnp.dot(sel.astype(BF16), upper, preferred_element_type=F32)
        base = pstart + run_sc[...] + rank
        exps = [jnp.exp(v - vals[0]) for v in vals]
        denom = exps[0] + exps[1] + exps[2] + exps[3]
        for k in range(TOP_K):
            pos_k = jnp.sum(jnp.where(hots[k], base, 0.0), axis=0, keepdims=True)
            pos_ref[pl.ds(k, 1), :] = pos_k.astype(jnp.int32)
            gate_ref[pl.ds(k, 1), :] = exps[k] / denom
        run_sc[...] += jnp.sum(self32, axis=1, keepdims=True)
        cnt_ref[...] = cnt


def _route(logits_t):
    n_exp, t = logits_t.shape
    tb = min(512, t)
    return pl.pallas_call(
        functools.partial(_route_kernel, n_exp=n_exp, tb=tb),
        out_shape=(jax.ShapeDtypeStruct((TOP_K, t), jnp.int32), jax.ShapeDtypeStruct((TOP_K, t), F32),
                   jax.ShapeDtypeStruct((n_exp, 1), F32)),
        grid=(2, t // tb),
        in_specs=[pl.BlockSpec((n_exp, tb), lambda p, i: (0, i))],
        out_specs=(pl.BlockSpec((TOP_K, tb), lambda p, i: (0, i * p)),
                   pl.BlockSpec((TOP_K, tb), lambda p, i: (0, i * p)),
                   pl.BlockSpec((n_exp, 1), lambda p, i: (0, 0))),
        scratch_shapes=[pltpu.VMEM((n_exp, 1), F32), pltpu.VMEM((n_exp, 1), F32)],
        compiler_params=_params("arbitrary", "arbitrary"),
        name="route_topk",
    )(logits_t)


def _row_copy(src_hbm, row, dst_vmem, slot, sem):
    return pltpu.make_async_copy(src_hbm.at[pl.ds(row, 1), :], dst_vmem.at[pl.ds(slot, 1), :], sem)


def _gather_rows(src_hbm, idx_ref, n_rows, dst_vmem, sem):
    def issue(r, carry):
        _row_copy(src_hbm, idx_ref[0, 0, r], dst_vmem, r, sem).start()
        return carry

    lax.fori_loop(0, n_rows, issue, 0, unroll=8)


def _wait_rows(src_hbm, n_rows, dst_vmem, sem):
    pltpu.make_async_copy(src_hbm.at[pl.ds(0, n_rows), :], dst_vmem, sem).wait()


def _dispatch_kernel(pos_ref, h_ref, zero_hbm, xs_hbm, sem, *, tb):
    del zero_hbm

    def issue(r, carry):
        for k in range(TOP_K):
            pltpu.make_async_copy(h_ref.at[pl.ds(r, 1), :], xs_hbm.at[pl.ds(pos_ref[0, 0, k * tb + r], 1), :],
                                  sem).start()
        return carry

    lax.fori_loop(0, tb, issue, 0, unroll=2)
    for k in range(TOP_K):
        pltpu.make_async_copy(h_ref, xs_hbm.at[pl.ds(0, tb), :], sem).wait()


def _dispatch(h2p, pos3, n_rows):
    t, dp = h2p.shape
    n_tiles = pos3.shape[0]
    tb = pos3.shape[2] // TOP_K
    return pl.pallas_call(
        functools.partial(_dispatch_kernel, tb=tb),
        out_shape=jax.ShapeDtypeStruct((n_rows, dp), h2p.dtype),
        grid=(n_tiles,),
        in_specs=[pl.BlockSpec((1, 1, TOP_K * tb), lambda i: (i, 0, 0), memory_space=pltpu.SMEM),
                  pl.BlockSpec((tb, dp), lambda i: (i, 0)),
                  pl.BlockSpec(memory_space=pl.ANY)],
        out_specs=pl.BlockSpec(memory_space=pl.ANY),
        scratch_shapes=[pltpu.SemaphoreType.DMA(())],
        input_output_aliases={2: 0},
        compiler_params=_params("arbitrary"),
        name="moe_dispatch",
    )(pos3, h2p, jnp.zeros((n_rows, dp), h2p.dtype))


def _new_expert(be_ref, b):
    return (b == 0) | (be_ref[b] != be_ref[jnp.maximum(b - 1, 0)])


W_PIECES = 4


def _weight_copies(w_hbm, e, col0, stage, sem):
    rows = stage.shape[0] // W_PIECES
    return [pltpu.make_async_copy(w_hbm.at[e, pl.ds(i * rows, rows), pl.ds(col0, stage.shape[1])],
                                  stage.at[pl.ds(i * rows, rows), :], sem) for i in range(W_PIECES)]


def _gateup_kernel(be_ref, nxt_ref, last_ref, nused_ref, x_ref, w_hbm, bg_ref, bl_ref, o_ref,
                   stg_g, stg_l, wg_sc, wl_sc, sem, *, nf, tf):
    c = pl.program_id(0)
    b = pl.program_id(1)

    def slab(e, chunk):
        col = pl.multiple_of(chunk * tf, tf)
        return _weight_copies(w_hbm, e, col, stg_g, sem) + _weight_copies(w_hbm, e, col + nf * tf, stg_l, sem)

    @pl.when((c == 0) & (b == 0))
    def _():
        for cp in slab(be_ref[0], 0):
            cp.start()

    @pl.when(_new_expert(be_ref, b))
    def _():
        for cp in slab(0, 0):
            cp.wait()
        wg_sc[...] = stg_g[...].astype(BF16)
        wl_sc[...] = stg_l[...].astype(BF16)
        last_run = last_ref[b] == 1

        @pl.when(jnp.logical_not(last_run))
        def _():
            for cp in slab(nxt_ref[b], c):
                cp.start()

        @pl.when(last_run & (c + 1 < nf))
        def _():
            for cp in slab(be_ref[0], c + 1):
                cp.start()

    @pl.when(b < nused_ref[0])
    def _():
        x = jnp.concatenate(_unpack_halves(x_ref[...]), axis=1)
        glu = jnp.dot(x, wg_sc[...], preferred_element_type=F32) + bg_ref[0]
        lin = jnp.dot(x, wl_sc[...], preferred_element_type=F32) + bl_ref[0]
        glu = jnp.minimum(glu, SWIGLU_LIMIT)
        lin = jnp.clip(lin, -SWIGLU_LIMIT, SWIGLU_LIMIT)
        o_ref[...] = (glu * _sigmoid(SWIGLU_ALPHA * glu) * (lin + 1.0)).astype(BF16)

    @pl.when(b >= nused_ref[0])
    def _():
        o_ref[...] = jnp.zeros_like(o_ref)


def _gateup(runs, n_used, xs, w_gu, b_gu3):
    n_rows, dp = xs.shape
    d = 2 * dp
    n_blocks = n_rows // MOE_BLOCK
    d_ff = w_gu.shape[2] // 2
    tf = min(1024, d_ff)
    nf = d_ff // tf
    return pl.pallas_call(
        functools.partial(_gateup_kernel, nf=nf, tf=tf),
        out_shape=jax.ShapeDtypeStruct((n_rows, d_ff), BF16),
        grid_spec=pltpu.PrefetchScalarGridSpec(
            num_scalar_prefetch=4,
            grid=(nf, n_blocks),
            in_specs=[pl.BlockSpec((MOE_BLOCK, dp), lambda c, b, be, nx, la, nu: (b, 0)),
                      pl.BlockSpec(memory_space=pl.ANY),
                      pl.BlockSpec((1, 1, tf), lambda c, b, be, nx, la, nu: (be[b], 0, c)),
                      pl.BlockSpec((1, 1, tf), lambda c, b, be, nx, la, nu: (be[b], 0, nf + c))],
            out_specs=pl.BlockSpec((MOE_BLOCK, tf), lambda c, b, be, nx, la, nu: (b, c)),
            scratch_shapes=[pltpu.VMEM((d, tf), F32), pltpu.VMEM((d, tf), F32),
                            pltpu.VMEM((d, tf), BF16), pltpu.VMEM((d, tf), BF16),
                            pltpu.SemaphoreType.DMA(())]),
        compiler_params=_params("arbitrary", "arbitrary"),
        name="moe_gateup",
    )(*runs, n_used, xs, w_gu, b_gu3, b_gu3)


def _down_kernel(be_ref, nxt_ref, last_ref, nused_ref, a_ref, w_hbm, b_ref, o_ref, stg, w_sc, sem):
    b = pl.program_id(0)

    @pl.when(b == 0)
    def _():
        for cp in _weight_copies(w_hbm, be_ref[0], 0, stg, sem):
            cp.start()

    @pl.when(_new_expert(be_ref, b))
    def _():
        for cp in _weight_copies(w_hbm, 0, 0, stg, sem):
            cp.wait()
        w_sc[...] = stg[...].astype(BF16)

        @pl.when(last_ref[b] == 0)
        def _():
            for cp in _weight_copies(w_hbm, nxt_ref[b], 0, stg, sem):
                cp.start()

    @pl.when(b < nused_ref[0])
    def _():
        y = jnp.dot(a_ref[...], w_sc[...], preferred_element_type=F32) + b_ref[0]
        o_ref[...] = _pack_halves(y)

    @pl.when(b >= nused_ref[0])
    def _():
        o_ref[...] = jnp.zeros_like(o_ref)


def _down(runs, n_used, act, w_down, b_down3):
    n_rows, d_ff = act.shape
    n_blocks = n_rows // MOE_BLOCK
    d = w_down.shape[2]
    return pl.pallas_call(
        _down_kernel,
        out_shape=jax.ShapeDtypeStruct((n_rows, d // 2), jnp.uint32),
        grid_spec=pltpu.PrefetchScalarGridSpec(
            num_scalar_prefetch=4,
            grid=(n_blocks,),
            in_specs=[pl.BlockSpec((MOE_BLOCK, d_ff), lambda b, be, nx, la, nu: (b, 0)),
                      pl.BlockSpec(memory_space=pl.ANY),
                      pl.BlockSpec((1, 1, d), lambda b, be, nx, la, nu: (be[b], 0, 0))],
            out_specs=pl.BlockSpec((MOE_BLOCK, d // 2), lambda b, be, nx, la, nu: (b, 0)),
            scratch_shapes=[pltpu.VMEM((d_ff, d), F32), pltpu.VMEM((d_ff, d), BF16),
                            pltpu.SemaphoreType.DMA(())]),
        compiler_params=_params("arbitrary"),
        name="moe_down",
    )(*runs, n_used, act, w_down, b_down3)


def _combine_kernel(pos_ref, pos_next_ref, y_hbm, gate_ref, x1_ref, mod_ref, gfin_ref, o_ref, buf, sem,
                    *, tb, n_tiles, last_layer):
    i = pl.program_id(0)
    slot = i % 2

    @pl.when(i == 0)
    def _():
        _gather_rows(y_hbm, pos_ref, TOP_K * tb, buf.at[0], sem.at[0])

    @pl.when(i + 1 < n_tiles)
    def _():
        _gather_rows(y_hbm, pos_next_ref, TOP_K * tb, buf.at[1 - slot], sem.at[1 - slot])

    _wait_rows(y_hbm, TOP_K * tb, buf.at[slot], sem.at[slot])
    gates = gate_ref[...]
    moe_lo = moe_hi = None
    for k in range(TOP_K):
        lo, hi = _unpack_halves(buf[slot, pl.ds(k * tb, tb), :])
        g = gates[:, k:k + 1]
        moe_lo = g * lo.astype(F32) if k == 0 else moe_lo + g * lo.astype(F32)
        moe_hi = g * hi.astype(F32) if k == 0 else moe_hi + g * hi.astype(F32)
    moe = jnp.concatenate([moe_lo, moe_hi], axis=1)
    x2 = x1_ref[...] + mod_ref[0, 5:6, :] * moe
    o_ref[...] = _rms(x2) * gfin_ref[...] if last_layer else x2


def _combine(pos3, y_buf, gates, x1, mod3, g_final, seq, last_layer):
    t, d = x1.shape
    n_tiles = pos3.shape[0]
    tb = pos3.shape[2] // TOP_K
    idx_spec = lambda step: pl.BlockSpec((1, 1, tb * TOP_K), lambda i: (jnp.minimum(i + step, n_tiles - 1), 0, 0),
                                         memory_space=pltpu.SMEM)
    return pl.pallas_call(
        functools.partial(_combine_kernel, tb=tb, n_tiles=n_tiles, last_layer=last_layer),
        out_shape=jax.ShapeDtypeStruct((t, d), F32),
        grid=(n_tiles,),
        in_specs=[idx_spec(0), idx_spec(1),
                  pl.BlockSpec(memory_space=pl.ANY),
                  pl.BlockSpec((tb, TOP_K), lambda i: (i, 0)),
                  pl.BlockSpec((tb, d), lambda i: (i, 0)),
                  pl.BlockSpec((1, mod3.shape[1], d), lambda i: (i // (seq // tb), 0, 0)),
                  pl.BlockSpec((1, d), lambda i: (0, 0))],
        out_specs=pl.BlockSpec((tb, d), lambda i: (i, 0)),
        scratch_shapes=[pltpu.VMEM((2, TOP_K * tb, y_buf.shape[1]), y_buf.dtype), pltpu.SemaphoreType.DMA((2,))],
        compiler_params=_params("arbitrary"),
        name="moe_combine",
    )(pos3, pos3, y_buf, gates, x1, mod3, g_final)


def kernel(x, c, w_ada, b_ada, g_norm1, g_norm2, w_in, b_fgate, g_fox_out, lb_logits, g_hg_out,
           w_out, w_router, b_router, w_gu, b_gu, w_down, b_down, g_final):
    n_batch, seq, d = x.shape
    t = n_batch * seq
    depth = w_ada.shape[0]
    fox_heads = b_fgate.shape[1]
    fox_w = g_fox_out.shape[1]
    hg_w = g_hg_out.shape[1]
    hg_heads = hg_w // HEAD_DIM
    n_exp = w_router.shape[2]
    assert fox_w == fox_heads * HEAD_DIM and fox_heads <= LANES
    n_blocks = -(-t * TOP_K // MOE_BLOCK) + n_exp
    n_mod = w_ada.shape[2] // d

    x2d = x.reshape(t, d)
    for l in range(depth):
        mod3 = _ada_mod(c, w_ada[l], b_ada[l]).reshape(n_batch, n_mod, d)

        w_l = w_in[l]
        w_main = jnp.concatenate([w_l[:, :3 * fox_w], w_l[:, 3 * fox_w + fox_heads:]], axis=1).astype(BF16)
        w_fg = jnp.pad(w_l[:, 3 * fox_w:3 * fox_w + fox_heads], ((0, 0), (0, LANES - fox_heads))).astype(BF16)
        b_fg = jnp.pad(b_fgate[l], (0, LANES - fox_heads)).reshape(1, LANES)
        proj, cum = _inproj(x2d, mod3, g_norm1[l].reshape(1, d), w_main, w_fg, b_fg, seq)
        proj3 = proj.reshape(n_batch, seq, -1)

        fox = _fox_attention(proj3, cum.reshape(n_batch, seq, LANES), fox_heads)

        lb3 = lb_logits.reshape(lb_logits.shape[0], hg_heads, HEAD_DIM).transpose(1, 0, 2)
        hg = _hgrn2(proj3, lb3, g_hg_out[l].reshape(hg_heads, 1, HEAD_DIM), hg_heads, 3 * fox_heads, l)

        x1, h2p, logits_t = _outproj(
            fox.reshape(t, fox_w), hg.reshape(t, hg_w), x2d, mod3, g_fox_out[l].reshape(1, fox_w),
            g_norm2[l].reshape(1, d), w_out[l].astype(BF16), w_router[l], b_router[l].reshape(n_exp, 1), seq)

        pos_t, gates_t, counts = _route(logits_t)

        cnt = counts[:, 0].astype(jnp.int32)
        padded_end = jnp.cumsum((cnt + MOE_BLOCK - 1) // MOE_BLOCK * MOE_BLOCK)
        block_row0 = jnp.arange(n_blocks, dtype=jnp.int32) * MOE_BLOCK
        block_e = jnp.sum((block_row0[:, None] >= padded_end[None, :]).astype(jnp.int32), axis=1)
        n_used = (padded_end[-1:] // MOE_BLOCK).astype(jnp.int32)
        block_id = jnp.arange(n_blocks, dtype=jnp.int32)
        block_e = jnp.minimum(block_e, jnp.sum(jnp.where(block_id == n_used - 1, block_e, 0)))
        change_at = jnp.where(block_e != jnp.roll(block_e, 1), block_id, n_blocks).at[0].set(n_blocks)
        next_change = jnp.flip(lax.cummin(jnp.flip(jnp.roll(change_at, -1).at[-1].set(n_blocks))))
        last_run = (next_change >= n_blocks).astype(jnp.int32)
        next_e = block_e[jnp.minimum(next_change, n_blocks - 1)]
        runs = (block_e, next_e, last_run)

        def tile_major(tb):
            return pos_t.reshape(TOP_K, t // tb, tb).transpose(1, 0, 2).reshape(t // tb, 1, TOP_K * tb)

        xs = _dispatch(h2p, tile_major(min(256, seq)), n_blocks * MOE_BLOCK)
        act = _gateup(runs, n_used, xs, w_gu[l], b_gu[l].reshape(n_exp, 1, -1))
        y_buf = _down(runs, n_used, act, w_down[l], b_down[l].reshape(n_exp, 1, d))
        x2d = _combine(tile_major(min(128, seq)), y_buf, gates_t.T, x1, mod3, g_final.reshape(1, d), seq,
                       l == depth - 1)
    return x2d.reshape(n_batch, seq, d)
```

```python
import functools

import jax
import jax.numpy as jnp
from jax import lax
from jax.experimental import pallas as pl
from jax.experimental.pallas import tpu as pltpu

HEAD_DIM = 128
TOP_K = 4
MOE_BLOCK = 256
HG_CHUNK = 64
NORM_EPS = 1e-6
SWIGLU_ALPHA = 1.702
SWIGLU_LIMIT = 7.0
LANES = 128
VMEM_LIMIT = 56 * 1024 * 1024

F32 = jnp.float32
BF16 = jnp.bfloat16
HIGHEST = lax.Precision.HIGHEST
NT_DIMS = (((1,), (1,)), ((), ()))
TN_DIMS = (((0,), (0,)), ((), ()))


def _params(*semantics):
    return pltpu.CompilerParams(dimension_semantics=semantics, vmem_limit_bytes=VMEM_LIMIT)


def _sigmoid(z):
    return 1.0 / (1.0 + jnp.exp(-z))


def _rms(v):
    return v * lax.rsqrt(jnp.mean(v * v, axis=-1, keepdims=True) + NORM_EPS)


def _ada_kernel(ct_ref, w_ref, b_ref, o_ref, *, n_batch, d_model):
    ct = ct_ref[...]
    cond = ct * _sigmoid(ct)
    rows = 256
    for b in range(n_batch):
        acc = jnp.zeros((8, w_ref.shape[1]), F32)
        for i in range(d_model // rows):
            w3 = w_ref[pl.ds(i * rows, rows), :].reshape(rows // 8, 8, -1)
            c3 = cond[i * rows:(i + 1) * rows, b:b + 1].reshape(rows // 8, 8, 1)
            acc = acc + jnp.sum(w3 * c3, axis=0)
        o_ref[pl.ds(b, 1), :] = jnp.sum(acc, axis=0, keepdims=True) + b_ref[...]


def _ada_mod(c, w_ada, b_ada):
    n_batch, d = c.shape
    n = w_ada.shape[1]
    tn = 1024
    return pl.pallas_call(
        functools.partial(_ada_kernel, n_batch=n_batch, d_model=d),
        out_shape=jax.ShapeDtypeStruct((n_batch, n), F32),
        grid=(n // tn,),
        in_specs=[pl.BlockSpec((d, n_batch), lambda j: (0, 0)),
                  pl.BlockSpec((d, tn), lambda j: (0, j)),
                  pl.BlockSpec((1, tn), lambda j: (0, j))],
        out_specs=pl.BlockSpec((n_batch, tn), lambda j: (0, j)),
        compiler_params=_params("arbitrary"),
        name="ada_mod",
    )(c.T, w_ada, b_ada.reshape(1, n))


def _inproj_kernel(x_ref, mod_ref, g_ref, w_ref, wf_ref, bf_ref, proj_ref, cum_ref, h_sc, carry_sc,
                   *, tiles_per_batch, sub):
    i = pl.program_id(0)
    j = pl.program_id(1)

    @pl.when(j == 0)
    def _():
        x = x_ref[...]
        h = _rms(x) * g_ref[...]
        h = h * (1.0 + mod_ref[0, 1:2, :]) + mod_ref[0, 0:1, :]
        hb = h.astype(BF16)
        h_sc[...] = hb
        z = jnp.dot(hb, wf_ref[...], preferred_element_type=F32) + bf_ref[...]
        logf = jnp.minimum(z, 0.0) - jnp.log(1.0 + jnp.exp(-jnp.abs(z)))

        @pl.when(i % tiles_per_batch == 0)
        def _():
            carry_sc[...] = jnp.zeros_like(carry_sc)

        r = lax.broadcasted_iota(jnp.int32, (sub, sub), 0)
        c = lax.broadcasted_iota(jnp.int32, (sub, sub), 1)
        tri = (r >= c).astype(F32)
        carry = carry_sc[...]
        for s in range(x.shape[0] // sub):
            blk = logf[s * sub:(s + 1) * sub, :]
            cs = jnp.dot(tri, blk, preferred_element_type=F32, precision=HIGHEST) + carry
            cum_ref[pl.ds(s * sub, sub), :] = cs
            carry = cs[sub - 1:sub, :]
        carry_sc[...] = carry

    proj_ref[...] = jnp.dot(h_sc[...], w_ref[...], preferred_element_type=F32).astype(BF16)


def _inproj(x2d, mod3, g1, w_main, w_fg, b_fg, seq):
    t, d = x2d.shape
    n = w_main.shape[1]
    tm = min(1024, seq)
    tn = next(c for c in (1024, 512, 256, 128) if n % c == 0)
    return pl.pallas_call(
        functools.partial(_inproj_kernel, tiles_per_batch=seq // tm, sub=256),
        out_shape=(jax.ShapeDtypeStruct((t, n), BF16), jax.ShapeDtypeStruct((t, LANES), F32)),
        grid=(t // tm, n // tn),
        in_specs=[pl.BlockSpec((tm, d), lambda i, j: (i, 0)),
                  pl.BlockSpec((1, mod3.shape[1], d), lambda i, j: (i // (seq // tm), 0, 0)),
                  pl.BlockSpec((1, d), lambda i, j: (0, 0)),
                  pl.BlockSpec((d, tn), lambda i, j: (0, j)),
                  pl.BlockSpec((d, LANES), lambda i, j: (0, 0)),
                  pl.BlockSpec((1, LANES), lambda i, j: (0, 0))],
        out_specs=(pl.BlockSpec((tm, tn), lambda i, j: (i, j)),
                   pl.BlockSpec((tm, LANES), lambda i, j: (i, 0))),
        scratch_shapes=[pltpu.VMEM((tm, d), BF16), pltpu.VMEM((1, LANES), F32)],
        compiler_params=_params("arbitrary", "arbitrary"),
        name="inproj",
    )(x2d, mod3, g1, w_main, w_fg, b_fg)


N_BIAS = 3
LOG2E = 1.4426950408889634


def _attn_kernel(q_ref, k_ref, v_ref, cum_ref, o_ref, kaug_sc, vt_sc, qt_sc, sa_sc, sb_sc, m_sc, l_sc, acc_sc,
                 *, blk, n_kv):
    h = pl.program_id(1)
    i = pl.program_id(2)

    @pl.when(i == 0)
    def _():
        lane = lax.broadcasted_iota(jnp.int32, (blk, LANES), 1)

        def prep(j, carry):
            start = pl.multiple_of(j * blk, blk)
            cum = cum_ref[0, pl.ds(start, blk), :]
            rest = jnp.sum(jnp.where(lane == h, cum, 0.0), axis=1, keepdims=True) * LOG2E
            bias = jnp.zeros((blk, LANES), F32)
            for piece in range(N_BIAS):
                part = rest.astype(BF16).astype(F32)
                bias = jnp.where(lane == piece, part, bias)
                rest = rest - part
            kaug_sc[j, :, 0:HEAD_DIM] = k_ref[0, pl.ds(start, blk), :]
            kaug_sc[j, :, HEAD_DIM:2 * HEAD_DIM] = bias.astype(BF16)
            vt_sc[j] = v_ref[0, pl.ds(start, blk), :].astype(F32).T.astype(BF16)
            return carry

        lax.fori_loop(0, n_kv, prep, 0)

    qs = q_ref[0].astype(F32) * (HEAD_DIM ** -0.5 * LOG2E)
    qt_sc[0:HEAD_DIM, :] = qs.T.astype(BF16)
    row = lax.broadcasted_iota(jnp.int32, (HEAD_DIM, blk), 0)
    qt_sc[HEAD_DIM:2 * HEAD_DIM, :] = jnp.where(row < N_BIAS, -1.0, 0.0).astype(BF16)
    m_sc[...] = jnp.full_like(m_sc, -jnp.inf)
    l_sc[...] = jnp.zeros_like(l_sc)
    acc_sc[...] = jnp.zeros_like(acc_sc)

    def scores(j):
        return jnp.dot(kaug_sc[j], qt_sc[...], preferred_element_type=F32)

    def update(j, s):
        m_prev = m_sc[...]
        m_new = jnp.maximum(m_prev, jnp.max(s, axis=0, keepdims=True))
        alpha = jnp.exp2(m_prev - m_new)
        p = jnp.exp2(s - m_new)
        l_sc[...] = alpha * l_sc[...] + jnp.sum(p, axis=0, keepdims=True)
        acc_sc[...] = alpha * acc_sc[...] + jnp.dot(vt_sc[j], p.astype(BF16), preferred_element_type=F32)
        m_sc[...] = m_new

    def causal(s):
        key = lax.broadcasted_iota(jnp.int32, (blk, blk), 0)
        qry = lax.broadcasted_iota(jnp.int32, (blk, blk), 1)
        return jnp.where(key <= qry, s, -jnp.inf)

    sa_sc[...] = scores(0)

    def pair(jj, carry):
        j = 2 * jj
        sb_sc[...] = scores(j + 1)
        update(j, sa_sc[...])
        sa_sc[...] = scores(j + 2)
        update(j + 1, sb_sc[...])
        return carry

    lax.fori_loop(0, i // 2, pair, 0)

    @pl.when(i % 2 == 0)
    def _():
        update(i, causal(sa_sc[...]))

    @pl.when(i % 2 == 1)
    def _():
        sb_sc[...] = scores(i)
        update(i - 1, sa_sc[...])
        update(i, causal(sb_sc[...]))

    o_ref[0] = (acc_sc[...] / l_sc[...]).T.astype(BF16)


def _fox_attention(proj3, cum3, n_heads):
    n_batch, seq, _ = proj3.shape
    blk = min(512, seq)
    n_kv = seq // blk
    return pl.pallas_call(
        functools.partial(_attn_kernel, blk=blk, n_kv=n_kv),
        out_shape=jax.ShapeDtypeStruct((n_batch, seq, n_heads * HEAD_DIM), BF16),
        grid=(n_batch, n_heads, n_kv),
        in_specs=[pl.BlockSpec((1, blk, HEAD_DIM), lambda b, h, i: (b, i, h)),
                  pl.BlockSpec((1, seq, HEAD_DIM), lambda b, h, i: (b, 0, n_heads + h)),
                  pl.BlockSpec((1, seq, HEAD_DIM), lambda b, h, i: (b, 0, 2 * n_heads + h)),
                  pl.BlockSpec((1, seq, LANES), lambda b, h, i: (b, 0, 0))],
        out_specs=pl.BlockSpec((1, blk, HEAD_DIM), lambda b, h, i: (b, i, h)),
        scratch_shapes=[pltpu.VMEM((n_kv, blk, 2 * HEAD_DIM), BF16),
                        pltpu.VMEM((n_kv, HEAD_DIM, blk), BF16),
                        pltpu.VMEM((2 * HEAD_DIM, blk), BF16),
                        pltpu.VMEM((blk, blk), F32), pltpu.VMEM((blk, blk), F32),
                        pltpu.VMEM((1, blk), F32), pltpu.VMEM((1, blk), F32),
                        pltpu.VMEM((HEAD_DIM, blk), F32)],
        compiler_params=_params("arbitrary", "arbitrary", "arbitrary"),
        name="fox_attention",
    )(proj3, proj3, proj3, cum3)


def _hgrn_kernel(q_ref, f_ref, i_ref, g_ref, lbl_ref, gn_ref, o_ref, st_sc, *, rows, sub, layer):
    @pl.when(pl.program_id(2) == 0)
    def _():
        st_sc[...] = jnp.zeros_like(st_sc)

    lbl = lbl_ref[0]
    e = jnp.exp(lbl - jnp.max(lbl, axis=0, keepdims=True))
    lb = jnp.sum(e[0:layer + 1, :], axis=0, keepdims=True) / jnp.sum(e, axis=0, keepdims=True)
    f = lb + (1.0 - lb) * _sigmoid(f_ref[0].astype(F32))
    logf = jnp.log(f)
    kk = 1.0 - f
    qf = q_ref[0].astype(F32)
    qq = qf * _sigmoid(qf)

    n_ch = sub // HG_CHUNK
    r = lax.broadcasted_iota(jnp.int32, (sub, sub), 0)
    c = lax.broadcasted_iota(jnp.int32, (sub, sub), 1)
    within = (r >= c) & (r // HG_CHUNK == c // HG_CHUNK)
    tri = within.astype(BF16)
    rw = lax.broadcasted_iota(jnp.int32, (sub, n_ch * HEAD_DIM), 0)
    cw = lax.broadcasted_iota(jnp.int32, (sub, n_ch * HEAD_DIM), 1)
    own_block = rw // HG_CHUNK == cw // HEAD_DIM

    st = st_sc[...]
    for s in range(rows // sub):
        rs = slice(s * sub, (s + 1) * sub)
        lf = logf[rs, :]
        hi = lf.astype(BF16)
        rest = lf - hi.astype(F32)
        mid = rest.astype(BF16)
        low = (rest - mid.astype(F32)).astype(BF16)
        two = jnp.dot(tri, jnp.concatenate([hi, mid], axis=1), preferred_element_type=F32)
        b = two[:, :HEAD_DIM] + two[:, HEAD_DIM:] + jnp.dot(tri, low, preferred_element_type=F32)
        b_last = [b[(n + 1) * HG_CHUNK - 1:(n + 1) * HG_CHUNK, :] for n in range(n_ch)]
        b_last_rows = jnp.concatenate([jnp.broadcast_to(bl, (HG_CHUNK, HEAD_DIM)) for bl in b_last], axis=0)
        q_dec = (qq[rs, :] * jnp.exp(b)).astype(BF16)
        k_inv = (kk[rs, :] * jnp.exp(-b)).astype(BF16)
        k_tail = (kk[rs, :] * jnp.exp(b_last_rows - b)).astype(BF16)
        vv = i_ref[0, pl.ds(s * sub, sub), :]
        attn = lax.dot_general(q_dec, k_inv, NT_DIMS, preferred_element_type=F32)
        o = jnp.dot(jnp.where(within, attn, 0.0).astype(BF16), vv, preferred_element_type=F32)
        zero = jnp.zeros((), BF16)
        k_blocks = jnp.where(own_block, jnp.concatenate([k_tail] * n_ch, axis=1), zero)
        upd = lax.dot_general(vv, k_blocks, TN_DIMS, preferred_element_type=F32)
        states = []
        for n in range(n_ch):
            states.append(st.astype(BF16))
            st = jnp.exp(b_last[n]) * st + upd[:, n * HEAD_DIM:(n + 1) * HEAD_DIM]
        q_blocks = jnp.where(own_block, jnp.concatenate([q_dec] * n_ch, axis=1), zero)
        o = o + lax.dot_general(q_blocks, jnp.concatenate(states, axis=1), NT_DIMS, preferred_element_type=F32)
        y = _rms(o) * gn_ref[0]
        gf = g_ref[0, pl.ds(s * sub, sub), :].astype(F32)
        o_ref[0, pl.ds(s * sub, sub), :] = (y * (gf * _sigmoid(gf))).astype(BF16)
    st_sc[...] = st


def _hgrn2(proj3, lb_logits3, g_hg3, n_heads, col0, layer):
    n_batch, seq, _ = proj3.shape
    rows = min(512, seq)
    spec = lambda off: pl.BlockSpec((1, rows, HEAD_DIM), lambda b, h, r: (b, r, col0 + off * n_heads + h))
    return pl.pallas_call(
        functools.partial(_hgrn_kernel, rows=rows, sub=min(256, rows), layer=layer),
        out_shape=jax.ShapeDtypeStruct((n_batch, seq, n_heads * HEAD_DIM), BF16),
        grid=(n_batch, n_heads, seq // rows),
        in_specs=[spec(0), spec(1), spec(2), spec(3),
                  pl.BlockSpec((1, lb_logits3.shape[1], HEAD_DIM), lambda b, h, r: (h, 0, 0)),
                  pl.BlockSpec((1, 1, HEAD_DIM), lambda b, h, r: (h, 0, 0))],
        out_specs=pl.BlockSpec((1, rows, HEAD_DIM), lambda b, h, r: (b, r, h)),
        scratch_shapes=[pltpu.VMEM((HEAD_DIM, HEAD_DIM), F32)],
        compiler_params=_params("arbitrary", "arbitrary", "arbitrary"),
        name="hgrn2",
    )(proj3, proj3, proj3, proj3, lb_logits3, g_hg3)


def _pack_halves(v):
    n = v.shape[1] // 2
    lo = lax.bitcast_convert_type(v[:, :n].astype(BF16).astype(F32), jnp.uint32) >> 16
    hi = lax.bitcast_convert_type(v[:, n:].astype(BF16).astype(F32), jnp.uint32) & jnp.uint32(0xFFFF0000)
    return lo | hi


def _unpack_halves(p):
    lo = lax.bitcast_convert_type(p << 16, F32).astype(BF16)
    hi = lax.bitcast_convert_type(p & jnp.uint32(0xFFFF0000), F32).astype(BF16)
    return lo, hi


def _outproj_kernel(fox_ref, hg_ref, x_ref, mod_ref, gfox_ref, g2_ref, w_ref, wr_ref, br_ref,
                    x1_ref, h2_ref, lg_ref, *, n_exp):
    fox = _rms(fox_ref[...].astype(F32)) * gfox_ref[...]
    mixed = jnp.concatenate([fox.astype(BF16), hg_ref[...]], axis=1)
    mix = jnp.dot(mixed, w_ref[...], preferred_element_type=F32)
    x1 = x_ref[...] + mod_ref[0, 2:3, :] * mix
    x1_ref[...] = x1
    h2 = _rms(x1) * g2_ref[...]
    h2 = h2 * (1.0 + mod_ref[0, 4:5, :]) + mod_ref[0, 3:4, :]
    h2_ref[...] = _pack_halves(h2)
    h_hi = h2.astype(BF16)
    h_lo = (h2 - h_hi.astype(F32)).astype(BF16)
    part = jnp.dot(h_hi, wr_ref[...], preferred_element_type=F32)
    part = part + jnp.dot(h_lo, wr_ref[...], preferred_element_type=F32)
    logits = part + pltpu.roll(part, LANES - n_exp, axis=1)
    lg_ref[...] = logits.T[0:n_exp, :] + br_ref[...]


def _outproj(fox2d, hg2d, x2d, mod3, g_fox, g2, w_out, w_router, b_router, seq):
    t, d = x2d.shape
    fox_w = fox2d.shape[1]
    n_exp = w_router.shape[1]
    assert 2 * n_exp <= LANES
    wr_hi = w_router.astype(BF16)
    wr_lo = (w_router - wr_hi.astype(F32)).astype(BF16)
    wr_cat = jnp.pad(jnp.concatenate([wr_hi, wr_lo], axis=1), ((0, 0), (0, LANES - 2 * n_exp)))
    tm = 256
    row = lambda i: (i, 0)
    const = lambda i: (0, 0)
    return pl.pallas_call(
        functools.partial(_outproj_kernel, n_exp=n_exp),
        out_shape=(jax.ShapeDtypeStruct((t, d), F32), jax.ShapeDtypeStruct((t, d // 2), jnp.uint32),
                   jax.ShapeDtypeStruct((n_exp, t), F32)),
        grid=(t // tm,),
        in_specs=[pl.BlockSpec((tm, fox_w), row),
                  pl.BlockSpec((tm, hg2d.shape[1]), row),
                  pl.BlockSpec((tm, d), row),
                  pl.BlockSpec((1, mod3.shape[1], d), lambda i: (i // (seq // tm), 0, 0)),
                  pl.BlockSpec((1, fox_w), const),
                  pl.BlockSpec((1, d), const),
                  pl.BlockSpec(w_out.shape, const),
                  pl.BlockSpec((d, LANES), const),
                  pl.BlockSpec((n_exp, 1), const)],
        out_specs=(pl.BlockSpec((tm, d), row), pl.BlockSpec((tm, d // 2), row),
                   pl.BlockSpec((n_exp, tm), lambda i: (0, i))),
        compiler_params=_params("arbitrary"),
        name="outproj_router",
    )(fox2d, hg2d, x2d, mod3, g_fox, g2, w_out, wr_cat, b_router)


def _route_kernel(lg_ref, pos_ref, gate_ref, cnt_ref, cnt_sc, run_sc, *, n_exp, tb):
    phase = pl.program_id(0)
    i = pl.program_id(1)

    @pl.when((phase == 0) & (i == 0))
    def _():
        cnt_sc[...] = jnp.zeros_like(cnt_sc)
        run_sc[...] = jnp.zeros_like(run_sc)

    logits = lg_ref[...]
    eidx = lax.broadcasted_iota(jnp.int32, (n_exp, tb), 0).astype(F32)
    work = logits
    vals, hots = [], []
    for _ in range(TOP_K):
        m = jnp.max(work, axis=0, keepdims=True)
        first = jnp.min(jnp.where(work == m, eidx, float(n_exp)), axis=0, keepdims=True)
        hot = eidx == first
        vals.append(m)
        hots.append(hot)
        work = jnp.where(hot, -jnp.inf, work)
    sel = hots[0] | hots[1] | hots[2] | hots[3]
    self32 = sel.astype(F32)

    @pl.when(phase == 0)
    def _():
        cnt_sc[...] += jnp.sum(self32, axis=1, keepdims=True)

    @pl.when(phase == 1)
    def _():
        cnt = cnt_sc[...]
        padded = jnp.ceil(cnt / MOE_BLOCK) * MOE_BLOCK
        r = lax.broadcasted_iota(jnp.int32, (n_exp, n_exp), 0)
        c = lax.broadcasted_iota(jnp.int32, (n_exp, n_exp), 1)
        strict = (c < r).astype(F32)
        pstart = jnp.dot(strict, jnp.broadcast_to(padded, (n_exp, LANES)), preferred_element_type=F32,
                         precision=HIGHEST)[:, 0:1]
        tr = lax.broadcasted_iota(jnp.int32, (tb, tb), 0)
        tc = lax.broadcasted_iota(jnp.int32, (tb, tb), 1)
        upper = (tr < tc).astype(BF16)
        rank = jnp.dot(sel.astype(BF16), upper, preferred_element_type=F32)
        base = pstart + run_sc[...] + rank
        exps = [jnp.exp(v - vals[0]) for v in vals]
        denom = exps[0] + exps[1] + exps[2] + exps[3]
        for k in range(TOP_K):
            pos_k = jnp.sum(jnp.where(hots[k], base, 0.0), axis=0, keepdims=True)
            pos_ref[pl.ds(k, 1), :] = pos_k.astype(jnp.int32)
            gate_ref[pl.ds(k, 1), :] = exps[k] / denom
        run_sc[...] += jnp.sum(self32, axis=1, keepdims=True)
        cnt_ref[...] = cnt


def _route(logits_t):
    n_exp, t = logits_t.shape
    tb = min(512, t)
    return pl.pallas_call(
        functools.partial(_route_kernel, n_exp=n_exp, tb=tb),
        out_shape=(jax.ShapeDtypeStruct((TOP_K, t), jnp.int32), jax.ShapeDtypeStruct((TOP_K, t), F32),
                   jax.ShapeDtypeStruct((n_exp, 1), F32)),
        grid=(2, t // tb),
        in_specs=[pl.BlockSpec((n_exp, tb), lambda p, i: (0, i))],
        out_specs=(pl.BlockSpec((TOP_K, tb), lambda p, i: (0, i * p)),
                   pl.BlockSpec((TOP_K, tb), lambda p, i: (0, i * p)),
                   pl.BlockSpec((n_exp, 1), lambda p, i: (0, 0))),
        scratch_shapes=[pltpu.VMEM((n_exp, 1), F32), pltpu.VMEM((n_exp, 1), F32)],
        compiler_params=_params("arbitrary", "arbitrary"),
        name="route_topk",
    )(logits_t)


def _row_copy(src_hbm, row, dst_vmem, slot, sem):
    return pltpu.make_async_copy(src_hbm.at[pl.ds(row, 1), :], dst_vmem.at[pl.ds(slot, 1), :], sem)


def _gather_rows(src_hbm, idx_ref, n_rows, dst_vmem, sem):
    for r in range(n_rows):
        _row_copy(src_hbm, idx_ref[0, 0, r], dst_vmem, r, sem).start()


def _wait_rows(src_hbm, n_rows, dst_vmem, sem):
    pltpu.make_async_copy(src_hbm.at[pl.ds(0, n_rows), :], dst_vmem, sem).wait()


def _dispatch_kernel(pad0_ref, padn_ref, nused_ref, pos_ref, h_ref, xs_hbm, zeros, sem, zsem,
                     *, tb, n_exp, n_blocks):
    @pl.when(pl.program_id(0) == 0)
    def _():
        zeros[...] = jnp.zeros_like(zeros)

        def each_fill(fn):
            def per_expert(e, carry):
                def per_row(r, inner):
                    fn(pltpu.make_async_copy(zeros.at[pl.ds(0, 1), :], xs_hbm.at[pl.ds(pad0_ref[e] + r, 1), :], zsem))
                    return inner

                lax.fori_loop(0, padn_ref[e], per_row, 0)
                return carry

            lax.fori_loop(0, n_exp, per_expert, 0)

            def per_block(b, carry):
                row0 = pl.multiple_of(b * MOE_BLOCK, MOE_BLOCK)
                fn(pltpu.make_async_copy(zeros, xs_hbm.at[pl.ds(row0, MOE_BLOCK), :], zsem))
                return carry

            lax.fori_loop(nused_ref[0], n_blocks, per_block, 0)

        each_fill(lambda cp: cp.start())
        each_fill(lambda cp: cp.wait())

    for r in range(tb):
        for k in range(TOP_K):
            pltpu.make_async_copy(h_ref.at[pl.ds(r, 1), :], xs_hbm.at[pl.ds(pos_ref[0, 0, k * tb + r], 1), :],
                                  sem).start()
    for k in range(TOP_K):
        pltpu.make_async_copy(h_ref, xs_hbm.at[pl.ds(0, tb), :], sem).wait()


def _dispatch(h2p, pos3, pad_start, pad_count, n_used, n_rows):
    t, dp = h2p.shape
    n_tiles = pos3.shape[0]
    tb = pos3.shape[2] // TOP_K
    return pl.pallas_call(
        functools.partial(_dispatch_kernel, tb=tb, n_exp=pad_start.shape[0], n_blocks=n_rows // MOE_BLOCK),
        out_shape=jax.ShapeDtypeStruct((n_rows, dp), h2p.dtype),
        grid_spec=pltpu.PrefetchScalarGridSpec(
            num_scalar_prefetch=3,
            grid=(n_tiles,),
            in_specs=[pl.BlockSpec((1, 1, TOP_K * tb), lambda i, p0, pn, nu: (i, 0, 0), memory_space=pltpu.SMEM),
                      pl.BlockSpec((tb, dp), lambda i, p0, pn, nu: (i, 0))],
            out_specs=pl.BlockSpec(memory_space=pl.ANY),
            scratch_shapes=[pltpu.VMEM((MOE_BLOCK, dp), h2p.dtype), pltpu.SemaphoreType.DMA(()),
                            pltpu.SemaphoreType.DMA(())]),
        compiler_params=_params("arbitrary"),
        name="moe_dispatch",
    )(pad_start, pad_count, n_used, pos3, h2p)


def _new_expert(be_ref, b):
    return (b == 0) | (be_ref[b] != be_ref[jnp.maximum(b - 1, 0)])


W_PIECES = 4


def _weight_copies(w_hbm, e, col0, stage, sem):
    rows = stage.shape[0] // W_PIECES
    return [pltpu.make_async_copy(w_hbm.at[e, pl.ds(i * rows, rows), pl.ds(col0, stage.shape[1])],
                                  stage.at[pl.ds(i * rows, rows), :], sem) for i in range(W_PIECES)]


def _convert_rows(src_f32, dst_bf16):
    rows = 16

    def body(i, carry):
        r0 = pl.multiple_of(i * rows, rows)
        dst_bf16[pl.ds(r0, rows), :] = src_f32[pl.ds(r0, rows), :].astype(BF16)
        return carry

    lax.fori_loop(0, src_f32.shape[0] // rows, body, 0, unroll=4)


def _gateup_kernel(be_ref, nxt_ref, last_ref, nused_ref, x_ref, w_hbm, bg_ref, bl_ref, o_ref,
                   stg_g, stg_l, wg_sc, wl_sc, sem, *, nf, tf):
    c = pl.program_id(0)
    b = pl.program_id(1)

    def slab(e, chunk):
        col = pl.multiple_of(chunk * tf, tf)
        return _weight_copies(w_hbm, e, col, stg_g, sem) + _weight_copies(w_hbm, e, col + nf * tf, stg_l, sem)

    @pl.when((c == 0) & (b == 0))
    def _():
        for cp in slab(be_ref[0], 0):
            cp.start()

    @pl.when(_new_expert(be_ref, b))
    def _():
        for cp in slab(0, 0):
            cp.wait()
        _convert_rows(stg_g, wg_sc)
        _convert_rows(stg_l, wl_sc)
        last_run = last_ref[b] == 1

        @pl.when(jnp.logical_not(last_run))
        def _():
            for cp in slab(nxt_ref[b], c):
                cp.start()

        @pl.when(last_run & (c + 1 < nf))
        def _():
            for cp in slab(be_ref[0], c + 1):
                cp.start()

    @pl.when(b < nused_ref[0])
    def _():
        x = jnp.concatenate(_unpack_halves(x_ref[...]), axis=1)
        glu = jnp.dot(x, wg_sc[...], preferred_element_type=F32) + bg_ref[0]
        lin = jnp.dot(x, wl_sc[...], preferred_element_type=F32) + bl_ref[0]
        glu = jnp.minimum(glu, SWIGLU_LIMIT)
        lin = jnp.clip(lin, -SWIGLU_LIMIT, SWIGLU_LIMIT)
        o_ref[...] = (glu * _sigmoid(SWIGLU_ALPHA * glu) * (lin + 1.0)).astype(BF16)

    @pl.when(b >= nused_ref[0])
    def _():
        o_ref[...] = jnp.zeros_like(o_ref)


def _gateup(runs, n_used, xs, w_gu, b_gu3):
    n_rows, dp = xs.shape
    d = 2 * dp
    n_blocks = n_rows // MOE_BLOCK
    d_ff = w_gu.shape[2] // 2
    tf = min(1024, d_ff)
    nf = d_ff // tf
    return pl.pallas_call(
        functools.partial(_gateup_kernel, nf=nf, tf=tf),
        out_shape=jax.ShapeDtypeStruct((n_rows, d_ff), BF16),
        grid_spec=pltpu.PrefetchScalarGridSpec(
            num_scalar_prefetch=4,
            grid=(nf, n_blocks),
            in_specs=[pl.BlockSpec((MOE_BLOCK, dp), lambda c, b, be, nx, la, nu: (jnp.minimum(b, nu[0] - 1), 0)),
                      pl.BlockSpec(memory_space=pl.ANY),
                      pl.BlockSpec((1, 1, tf), lambda c, b, be, nx, la, nu: (be[b], 0, c)),
                      pl.BlockSpec((1, 1, tf), lambda c, b, be, nx, la, nu: (be[b], 0, nf + c))],
            out_specs=pl.BlockSpec((MOE_BLOCK, tf), lambda c, b, be, nx, la, nu: (b, c)),
            scratch_shapes=[pltpu.VMEM((d, tf), F32), pltpu.VMEM((d, tf), F32),
                            pltpu.VMEM((d, tf), BF16), pltpu.VMEM((d, tf), BF16),
                            pltpu.SemaphoreType.DMA(())]),
        compiler_params=_params("arbitrary", "arbitrary"),
        name="moe_gateup",
    )(*runs, n_used, xs, w_gu, b_gu3, b_gu3)


def _down_kernel(be_ref, nxt_ref, last_ref, nused_ref, a_ref, w_hbm, b_ref, o_ref, stg, w_sc, sem):
    b = pl.program_id(0)

    @pl.when(b == 0)
    def _():
        for cp in _weight_copies(w_hbm, be_ref[0], 0, stg, sem):
            cp.start()

    @pl.when(_new_expert(be_ref, b))
    def _():
        for cp in _weight_copies(w_hbm, 0, 0, stg, sem):
            cp.wait()
        _convert_rows(stg, w_sc)

        @pl.when(last_ref[b] == 0)
        def _():
            for cp in _weight_copies(w_hbm, nxt_ref[b], 0, stg, sem):
                cp.start()

    @pl.when(b < nused_ref[0])
    def _():
        y = jnp.dot(a_ref[...], w_sc[...], preferred_element_type=F32) + b_ref[0]
        o_ref[...] = _pack_halves(y)

    @pl.when(b >= nused_ref[0])
    def _():
        o_ref[...] = jnp.zeros_like(o_ref)


def _down(runs, n_used, act, w_down, b_down3):
    n_rows, d_ff = act.shape
    n_blocks = n_rows // MOE_BLOCK
    d = w_down.shape[2]
    return pl.pallas_call(
        _down_kernel,
        out_shape=jax.ShapeDtypeStruct((n_rows, d // 2), jnp.uint32),
        grid_spec=pltpu.PrefetchScalarGridSpec(
            num_scalar_prefetch=4,
            grid=(n_blocks,),
            in_specs=[pl.BlockSpec((MOE_BLOCK, d_ff), lambda b, be, nx, la, nu: (b, 0)),
                      pl.BlockSpec(memory_space=pl.ANY),
                      pl.BlockSpec((1, 1, d), lambda b, be, nx, la, nu: (be[b], 0, 0))],
            out_specs=pl.BlockSpec((MOE_BLOCK, d // 2), lambda b, be, nx, la, nu: (b, 0)),
            scratch_shapes=[pltpu.VMEM((d_ff, d), F32), pltpu.VMEM((d_ff, d), BF16),
                            pltpu.SemaphoreType.DMA(())]),
        compiler_params=_params("arbitrary"),
        name="moe_down",
    )(*runs, n_used, act, w_down, b_down3)


def _combine_kernel(pos_ref, pos_next_ref, y_hbm, gate_ref, x1_ref, mod_ref, gfin_ref, o_ref, buf, sem,
                    *, tb, n_tiles, last_layer):
    i = pl.program_id(0)
    slot = i % 2

    @pl.when(i == 0)
    def _():
        _gather_rows(y_hbm, pos_ref, TOP_K * tb, buf.at[0], sem.at[0])

    @pl.when(i + 1 < n_tiles)
    def _():
        _gather_rows(y_hbm, pos_next_ref, TOP_K * tb, buf.at[1 - slot], sem.at[1 - slot])

    _wait_rows(y_hbm, TOP_K * tb, buf.at[slot], sem.at[slot])
    gates = gate_ref[...]
    moe_lo = moe_hi = None
    for k in range(TOP_K):
        lo, hi = _unpack_halves(buf[slot, pl.ds(k * tb, tb), :])
        g = gates[:, k:k + 1]
        moe_lo = g * lo.astype(F32) if k == 0 else moe_lo + g * lo.astype(F32)
        moe_hi = g * hi.astype(F32) if k == 0 else moe_hi + g * hi.astype(F32)
    moe = jnp.concatenate([moe_lo, moe_hi], axis=1)
    x2 = x1_ref[...] + mod_ref[0, 5:6, :] * moe
    o_ref[...] = _rms(x2) * gfin_ref[...] if last_layer else x2


def _combine(pos3, y_buf, gates, x1, mod3, g_final, seq, last_layer):
    t, d = x1.shape
    n_tiles = pos3.shape[0]
    tb = pos3.shape[2] // TOP_K
    idx_spec = lambda step: pl.BlockSpec((1, 1, tb * TOP_K), lambda i: (jnp.minimum(i + step, n_tiles - 1), 0, 0),
                                         memory_space=pltpu.SMEM)
    return pl.pallas_call(
        functools.partial(_combine_kernel, tb=tb, n_tiles=n_tiles, last_layer=last_layer),
        out_shape=jax.ShapeDtypeStruct((t, d), F32),
        grid=(n_tiles,),
        in_specs=[idx_spec(0), idx_spec(1),
                  pl.BlockSpec(memory_space=pl.ANY),
                  pl.BlockSpec((tb, TOP_K), lambda i: (i, 0)),
                  pl.BlockSpec((tb, d), lambda i: (i, 0)),
                  pl.BlockSpec((1, mod3.shape[1], d), lambda i: (i // (seq // tb), 0, 0)),
                  pl.BlockSpec((1, d), lambda i: (0, 0))],
        out_specs=pl.BlockSpec((tb, d), lambda i: (i, 0)),
        scratch_shapes=[pltpu.VMEM((2, TOP_K * tb, y_buf.shape[1]), y_buf.dtype), pltpu.SemaphoreType.DMA((2,))],
        compiler_params=_params("arbitrary"),
        name="moe_combine",
    )(pos3, pos3, y_buf, gates, x1, mod3, g_final)


def kernel(x, c, w_ada, b_ada, g_norm1, g_norm2, w_in, b_fgate, g_fox_out, lb_logits, g_hg_out,
           w_out, w_router, b_router, w_gu, b_gu, w_down, b_down, g_final):
    n_batch, seq, d = x.shape
    t = n_batch * seq
    depth = w_ada.shape[0]
    fox_heads = b_fgate.shape[1]
    fox_w = g_fox_out.shape[1]
    hg_w = g_hg_out.shape[1]
    hg_heads = hg_w // HEAD_DIM
    n_exp = w_router.shape[2]
    assert fox_w == fox_heads * HEAD_DIM and fox_heads <= LANES
    n_blocks = -(-t * TOP_K // MOE_BLOCK) + n_exp
    n_mod = w_ada.shape[2] // d

    x2d = x.reshape(t, d)
    for l in range(depth):
        mod3 = _ada_mod(c, w_ada[l], b_ada[l]).reshape(n_batch, n_mod, d)

        w_l = w_in[l]
        w_main = jnp.concatenate([w_l[:, :3 * fox_w], w_l[:, 3 * fox_w + fox_heads:]], axis=1).astype(BF16)
        w_fg = jnp.pad(w_l[:, 3 * fox_w:3 * fox_w + fox_heads], ((0, 0), (0, LANES - fox_heads))).astype(BF16)
        b_fg = jnp.pad(b_fgate[l], (0, LANES - fox_heads)).reshape(1, LANES)
        proj, cum = _inproj(x2d, mod3, g_norm1[l].reshape(1, d), w_main, w_fg, b_fg, seq)
        proj3 = proj.reshape(n_batch, seq, -1)

        fox = _fox_attention(proj3, cum.reshape(n_batch, seq, LANES), fox_heads)

        lb3 = lb_logits.reshape(lb_logits.shape[0], hg_heads, HEAD_DIM).transpose(1, 0, 2)
        hg = _hgrn2(proj3, lb3, g_hg_out[l].reshape(hg_heads, 1, HEAD_DIM), hg_heads, 3 * fox_heads, l)

        x1, h2p, logits_t = _outproj(
            fox.reshape(t, fox_w), hg.reshape(t, hg_w), x2d, mod3, g_fox_out[l].reshape(1, fox_w),
            g_norm2[l].reshape(1, d), w_out[l].astype(BF16), w_router[l], b_router[l].reshape(n_exp, 1), seq)

        pos_t, gates_t, counts = _route(logits_t)

        cnt = counts[:, 0].astype(jnp.int32)
        padded = (cnt + MOE_BLOCK - 1) // MOE_BLOCK * MOE_BLOCK
        padded_end = jnp.cumsum(padded)
        block_row0 = jnp.arange(n_blocks, dtype=jnp.int32) * MOE_BLOCK
        block_e = jnp.sum((block_row0[:, None] >= padded_end[None, :]).astype(jnp.int32), axis=1)
        n_used = (padded_end[-1:] // MOE_BLOCK).astype(jnp.int32)
        block_id = jnp.arange(n_blocks, dtype=jnp.int32)
        block_e = jnp.minimum(block_e, jnp.sum(jnp.where(block_id == n_used - 1, block_e, 0)))
        change_at = jnp.where(block_e != jnp.roll(block_e, 1), block_id, n_blocks).at[0].set(n_blocks)
        next_change = jnp.flip(lax.cummin(jnp.flip(jnp.roll(change_at, -1).at[-1].set(n_blocks))))
        last_run = (next_change >= n_blocks).astype(jnp.int32)
        next_e = block_e[jnp.minimum(next_change, n_blocks - 1)]
        runs = (block_e, next_e, last_run)

        def tile_major(tb):
            return pos_t.reshape(TOP_K, t // tb, tb).transpose(1, 0, 2).reshape(t // tb, 1, TOP_K * tb)

        pad_count = padded - cnt
        xs = _dispatch(h2p, tile_major(min(256, seq)), padded_end - pad_count, pad_count, n_used,
                       n_blocks * MOE_BLOCK)
        act = _gateup(runs, n_used, xs, w_gu[l], b_gu[l].reshape(n_exp, 1, -1))
        y_buf = _down(runs, n_used, act, w_down[l], b_down[l].reshape(n_exp, 1, d))
        x2d = _combine(tile_major(min(128, seq)), y_buf, gates_t.T, x1, mod3, g_final.reshape(1, d), seq,
                       l == depth - 1)
    return x2d.reshape(n_batch, seq, d)
```

```python
import functools

import jax
import jax.numpy as jnp
from jax import lax
from jax.experimental import pallas as pl
from jax.experimental.pallas import tpu as pltpu

HEAD_DIM = 128
TOP_K = 4
MOE_BLOCK = 256
HG_CHUNK = 64
NORM_EPS = 1e-6
SWIGLU_ALPHA = 1.702
SWIGLU_LIMIT = 7.0
LANES = 128
VMEM_LIMIT = 56 * 1024 * 1024

F32 = jnp.float32
BF16 = jnp.bfloat16
HIGHEST = lax.Precision.HIGHEST
NT_DIMS = (((1,), (1,)), ((), ()))
TN_DIMS = (((0,), (0,)), ((), ()))


def _params(*semantics):
    return pltpu.CompilerParams(dimension_semantics=semantics, vmem_limit_bytes=VMEM_LIMIT)


def _sigmoid(z):
    return 1.0 / (1.0 + jnp.exp(-z))


def _rms(v):
    return v * lax.rsqrt(jnp.mean(v * v, axis=-1, keepdims=True) + NORM_EPS)


def _ada_kernel(ct_ref, w_ref, b_ref, o_ref, *, n_batch, d_model):
    ct = ct_ref[...]
    cond = ct * _sigmoid(ct)
    rows = 256
    for b in range(n_batch):
        acc = jnp.zeros((8, w_ref.shape[1]), F32)
        for i in range(d_model // rows):
            w3 = w_ref[pl.ds(i * rows, rows), :].reshape(rows // 8, 8, -1)
            c3 = cond[i * rows:(i + 1) * rows, b:b + 1].reshape(rows // 8, 8, 1)
            acc = acc + jnp.sum(w3 * c3, axis=0)
        o_ref[pl.ds(b, 1), :] = jnp.sum(acc, axis=0, keepdims=True) + b_ref[...]


def _ada_mod(c, w_ada, b_ada):
    n_batch, d = c.shape
    n = w_ada.shape[1]
    tn = 1024
    return pl.pallas_call(
        functools.partial(_ada_kernel, n_batch=n_batch, d_model=d),
        out_shape=jax.ShapeDtypeStruct((n_batch, n), F32),
        grid=(n // tn,),
        in_specs=[pl.BlockSpec((d, n_batch), lambda j: (0, 0)),
                  pl.BlockSpec((d, tn), lambda j: (0, j)),
                  pl.BlockSpec((1, tn), lambda j: (0, j))],
        out_specs=pl.BlockSpec((n_batch, tn), lambda j: (0, j)),
        compiler_params=_params("arbitrary"),
        name="ada_mod",
    )(c.T, w_ada, b_ada.reshape(1, n))


def _inproj_kernel(x_ref, mod_ref, g_ref, w_ref, wf_ref, bf_ref, proj_ref, cum_ref, h_sc, carry_sc,
                   *, tiles_per_batch, sub):
    i = pl.program_id(0)
    j = pl.program_id(1)

    @pl.when(j == 0)
    def _():
        x = x_ref[...]
        h = _rms(x) * (g_ref[...] * (1.0 + mod_ref[0, 1:2, :])) + mod_ref[0, 0:1, :]
        hb = h.astype(BF16)
        h_sc[...] = hb
        z = jnp.dot(hb, wf_ref[...], preferred_element_type=F32) + bf_ref[...]
        logf = jnp.minimum(z, 0.0) - jnp.log(1.0 + jnp.exp(-jnp.abs(z)))

        @pl.when(i % tiles_per_batch == 0)
        def _():
            carry_sc[...] = jnp.zeros_like(carry_sc)

        r = lax.broadcasted_iota(jnp.int32, (sub, sub), 0)
        c = lax.broadcasted_iota(jnp.int32, (sub, sub), 1)
        tri = (r >= c).astype(BF16)
        carry = carry_sc[...]
        for s in range(x.shape[0] // sub):
            blk = logf[s * sub:(s + 1) * sub, :]
            hi = blk.astype(BF16)
            rest = blk - hi.astype(F32)
            mid = rest.astype(BF16)
            low = (rest - mid.astype(F32)).astype(BF16)
            two = jnp.dot(tri, jnp.concatenate([hi, mid], axis=1), preferred_element_type=F32)
            cs = two[:, :LANES] + two[:, LANES:] + jnp.dot(tri, low, preferred_element_type=F32) + carry
            cum_ref[pl.ds(s * sub, sub), :] = cs
            carry = cs[sub - 1:sub, :]
        carry_sc[...] = carry

    proj_ref[...] = jnp.dot(h_sc[...], w_ref[...], preferred_element_type=F32).astype(BF16)


def _inproj(x2d, mod3, g1, w_main, w_fg, b_fg, seq):
    t, d = x2d.shape
    n = w_main.shape[1]
    tm = min(1024, seq)
    tn = next(c for c in (1024, 512, 256, 128) if n % c == 0)
    return pl.pallas_call(
        functools.partial(_inproj_kernel, tiles_per_batch=seq // tm, sub=256),
        out_shape=(jax.ShapeDtypeStruct((t, n), BF16), jax.ShapeDtypeStruct((t, LANES), F32)),
        grid=(t // tm, n // tn),
        in_specs=[pl.BlockSpec((tm, d), lambda i, j: (i, 0)),
                  pl.BlockSpec((1, mod3.shape[1], d), lambda i, j: (i // (seq // tm), 0, 0)),
                  pl.BlockSpec((1, d), lambda i, j: (0, 0)),
                  pl.BlockSpec((d, tn), lambda i, j: (0, j)),
                  pl.BlockSpec((d, LANES), lambda i, j: (0, 0)),
                  pl.BlockSpec((1, LANES), lambda i, j: (0, 0))],
        out_specs=(pl.BlockSpec((tm, tn), lambda i, j: (i, j)),
                   pl.BlockSpec((tm, LANES), lambda i, j: (i, 0))),
        scratch_shapes=[pltpu.VMEM((tm, d), BF16), pltpu.VMEM((1, LANES), F32)],
        compiler_params=_params("arbitrary", "arbitrary"),
        name="inproj",
    )(x2d, mod3, g1, w_main, w_fg, b_fg)


N_BIAS = 3
LOG2E = 1.4426950408889634


def _attn_kernel(q_ref, k_ref, v_ref, cum_ref, o_ref, kaug_sc, vt_sc, qt_sc, sa_sc, sb_sc, m_sc, l_sc, acc_sc,
                 *, blk, n_kv):
    h = pl.program_id(1)
    i = pl.program_id(2)

    @pl.when(i == 0)
    def _():
        lane = lax.broadcasted_iota(jnp.int32, (blk, LANES), 1)

        def prep(j, carry):
            start = pl.multiple_of(j * blk, blk)
            cum = cum_ref[0, pl.ds(start, blk), :]
            rest = jnp.sum(jnp.where(lane == h, cum, 0.0), axis=1, keepdims=True) * LOG2E
            bias = jnp.zeros((blk, LANES), F32)
            for piece in range(N_BIAS):
                part = rest.astype(BF16).astype(F32)
                bias = jnp.where(lane == piece, part, bias)
                rest = rest - part
            kaug_sc[j, :, 0:HEAD_DIM] = k_ref[0, pl.ds(start, blk), :]
            kaug_sc[j, :, HEAD_DIM:2 * HEAD_DIM] = bias.astype(BF16)
            vt_sc[j] = v_ref[0, pl.ds(start, blk), :].astype(F32).T.astype(BF16)
            return carry

        lax.fori_loop(0, n_kv, prep, 0)

    qs = q_ref[0].astype(F32) * (HEAD_DIM ** -0.5 * LOG2E)
    qt_sc[0:HEAD_DIM, :] = qs.T.astype(BF16)
    row = lax.broadcasted_iota(jnp.int32, (HEAD_DIM, blk), 0)
    qt_sc[HEAD_DIM:2 * HEAD_DIM, :] = jnp.where(row < N_BIAS, -1.0, 0.0).astype(BF16)
    m_sc[...] = jnp.full_like(m_sc, -jnp.inf)
    l_sc[...] = jnp.zeros_like(l_sc)
    acc_sc[...] = jnp.zeros_like(acc_sc)

    def scores(j):
        return jnp.dot(kaug_sc[j], qt_sc[...], preferred_element_type=F32)

    def update(j, s):
        m_prev = m_sc[...]
        m_new = jnp.maximum(m_prev, jnp.max(s, axis=0, keepdims=True))
        alpha = jnp.exp2(m_prev - m_new)
        p = jnp.exp2(s - m_new)
        l_sc[...] = alpha * l_sc[...] + jnp.sum(p, axis=0, keepdims=True)
        acc_sc[...] = alpha * acc_sc[...] + jnp.dot(vt_sc[j], p.astype(BF16), preferred_element_type=F32)
        m_sc[...] = m_new

    def causal(s):
        key = lax.broadcasted_iota(jnp.int32, (blk, blk), 0)
        qry = lax.broadcasted_iota(jnp.int32, (blk, blk), 1)
        return jnp.where(key <= qry, s, -jnp.inf)

    sa_sc[...] = scores(0)

    def pair_at(j):
        sb_sc[...] = scores(j + 1)
        update(j, sa_sc[...])
        sa_sc[...] = scores(j + 2)
        update(j + 1, sb_sc[...])

    def quad(jj, carry):
        pair_at(4 * jj)
        pair_at(4 * jj + 2)
        return carry

    def pair(jj, carry):
        pair_at(4 * (i // 4) + 2 * jj)
        return carry

    lax.fori_loop(0, i // 4, quad, 0)
    lax.fori_loop(0, (i % 4) // 2, pair, 0)

    @pl.when(i % 2 == 0)
    def _():
        update(i, causal(sa_sc[...]))

    @pl.when(i % 2 == 1)
    def _():
        sb_sc[...] = scores(i)
        update(i - 1, sa_sc[...])
        update(i, causal(sb_sc[...]))

    o_ref[0] = (acc_sc[...] / l_sc[...]).T.astype(BF16)


def _fox_attention(proj3, cum3, n_heads):
    n_batch, seq, _ = proj3.shape
    blk = min(512, seq)
    n_kv = seq // blk
    return pl.pallas_call(
        functools.partial(_attn_kernel, blk=blk, n_kv=n_kv),
        out_shape=jax.ShapeDtypeStruct((n_batch, seq, n_heads * HEAD_DIM), BF16),
        grid=(n_batch, n_heads, n_kv),
        in_specs=[pl.BlockSpec((1, blk, HEAD_DIM), lambda b, h, i: (b, i, h)),
                  pl.BlockSpec((1, seq, HEAD_DIM), lambda b, h, i: (b, 0, n_heads + h)),
                  pl.BlockSpec((1, seq, HEAD_DIM), lambda b, h, i: (b, 0, 2 * n_heads + h)),
                  pl.BlockSpec((1, seq, LANES), lambda b, h, i: (b, 0, 0))],
        out_specs=pl.BlockSpec((1, blk, HEAD_DIM), lambda b, h, i: (b, i, h)),
        scratch_shapes=[pltpu.VMEM((n_kv, blk, 2 * HEAD_DIM), BF16),
                        pltpu.VMEM((n_kv, HEAD_DIM, blk), BF16),
                        pltpu.VMEM((2 * HEAD_DIM, blk), BF16),
                        pltpu.VMEM((blk, blk), F32), pltpu.VMEM((blk, blk), F32),
                        pltpu.VMEM((1, blk), F32), pltpu.VMEM((1, blk), F32),
                        pltpu.VMEM((HEAD_DIM, blk), F32)],
        compiler_params=_params("arbitrary", "arbitrary", "arbitrary"),
        name="fox_attention",
    )(proj3, proj3, proj3, cum3)


def _hgrn_kernel(q_ref, f_ref, i_ref, g_ref, lbl_ref, gn_ref, o_ref, st_sc, *, rows, sub, layer):
    @pl.when(pl.program_id(2) == 0)
    def _():
        st_sc[...] = jnp.zeros_like(st_sc)

    lbl = lbl_ref[0]
    e = jnp.exp(lbl - jnp.max(lbl, axis=0, keepdims=True))
    lb = jnp.sum(e[0:layer + 1, :], axis=0, keepdims=True) / jnp.sum(e, axis=0, keepdims=True)
    f = lb + (1.0 - lb) * _sigmoid(f_ref[0].astype(F32))
    logf = jnp.log(f)
    kk = 1.0 - f
    qf = q_ref[0].astype(F32)
    qq = qf * _sigmoid(qf)

    n_ch = sub // HG_CHUNK
    r = lax.broadcasted_iota(jnp.int32, (sub, sub), 0)
    c = lax.broadcasted_iota(jnp.int32, (sub, sub), 1)
    within = (r >= c) & (r // HG_CHUNK == c // HG_CHUNK)
    tri = within.astype(BF16)
    rw = lax.broadcasted_iota(jnp.int32, (sub, n_ch * HEAD_DIM), 0)
    cw = lax.broadcasted_iota(jnp.int32, (sub, n_ch * HEAD_DIM), 1)
    own_block = rw // HG_CHUNK == cw // HEAD_DIM

    st = st_sc[...]
    for s in range(rows // sub):
        rs = slice(s * sub, (s + 1) * sub)
        lf = logf[rs, :]
        hi = lf.astype(BF16)
        rest = lf - hi.astype(F32)
        mid = rest.astype(BF16)
        low = (rest - mid.astype(F32)).astype(BF16)
        two = jnp.dot(tri, jnp.concatenate([hi, mid], axis=1), preferred_element_type=F32)
        b = two[:, :HEAD_DIM] + two[:, HEAD_DIM:] + jnp.dot(tri, low, preferred_element_type=F32)
        b_last = [b[(n + 1) * HG_CHUNK - 1:(n + 1) * HG_CHUNK, :] for n in range(n_ch)]
        b_last_rows = jnp.concatenate([jnp.broadcast_to(bl, (HG_CHUNK, HEAD_DIM)) for bl in b_last], axis=0)
        q_dec = (qq[rs, :] * jnp.exp(b)).astype(BF16)
        k_inv = (kk[rs, :] * jnp.exp(-b)).astype(BF16)
        k_tail = (kk[rs, :] * jnp.exp(b_last_rows - b)).astype(BF16)
        vv = i_ref[0, pl.ds(s * sub, sub), :]
        attn = lax.dot_general(q_dec, k_inv, NT_DIMS, preferred_element_type=F32)
        o = jnp.dot(jnp.where(within, attn, 0.0).astype(BF16), vv, preferred_element_type=F32)
        zero = jnp.zeros((), BF16)
        k_blocks = jnp.where(own_block, jnp.concatenate([k_tail] * n_ch, axis=1), zero)
        upd = lax.dot_general(vv, k_blocks, TN_DIMS, preferred_element_type=F32)
        states = []
        for n in range(n_ch):
            states.append(st.astype(BF16))
            st = jnp.exp(b_last[n]) * st + upd[:, n * HEAD_DIM:(n + 1) * HEAD_DIM]
        q_blocks = jnp.where(own_block, jnp.concatenate([q_dec] * n_ch, axis=1), zero)
        o = o + lax.dot_general(q_blocks, jnp.concatenate(states, axis=1), NT_DIMS, preferred_element_type=F32)
        y = _rms(o) * gn_ref[0]
        gf = g_ref[0, pl.ds(s * sub, sub), :].astype(F32)
        o_ref[0, pl.ds(s * sub, sub), :] = (y * (gf * _sigmoid(gf))).astype(BF16)
    st_sc[...] = st


def _hgrn2(proj3, lb_logits3, g_hg3, n_heads, col0, layer):
    n_batch, seq, _ = proj3.shape
    rows = min(512, seq)
    spec = lambda off: pl.BlockSpec((1, rows, HEAD_DIM), lambda b, h, r: (b, r, col0 + off * n_heads + h))
    return pl.pallas_call(
        functools.partial(_hgrn_kernel, rows=rows, sub=min(256, rows), layer=layer),
        out_shape=jax.ShapeDtypeStruct((n_batch, seq, n_heads * HEAD_DIM), BF16),
        grid=(n_batch, n_heads, seq // rows),
        in_specs=[spec(0), spec(1), spec(2), spec(3),
                  pl.BlockSpec((1, lb_logits3.shape[1], HEAD_DIM), lambda b, h, r: (h, 0, 0)),
                  pl.BlockSpec((1, 1, HEAD_DIM), lambda b, h, r: (h, 0, 0))],
        out_specs=pl.BlockSpec((1, rows, HEAD_DIM), lambda b, h, r: (b, r, h)),
        scratch_shapes=[pltpu.VMEM((HEAD_DIM, HEAD_DIM), F32)],
        compiler_params=_params("arbitrary", "arbitrary", "arbitrary"),
        name="hgrn2",
    )(proj3, proj3, proj3, proj3, lb_logits3, g_hg3)


def _pack_halves(v):
    n = v.shape[1] // 2
    lo = lax.bitcast_convert_type(v[:, :n].astype(BF16).astype(F32), jnp.uint32) >> 16
    hi = lax.bitcast_convert_type(v[:, n:].astype(BF16).astype(F32), jnp.uint32) & jnp.uint32(0xFFFF0000)
    return lo | hi


def _unpack_halves(p):
    lo = lax.bitcast_convert_type(p << 16, F32).astype(BF16)
    hi = lax.bitcast_convert_type(p & jnp.uint32(0xFFFF0000), F32).astype(BF16)
    return lo, hi


def _outproj_kernel(fox_ref, hg_ref, x_ref, mod_ref, gfox_ref, g2_ref, w_ref, wr_ref, br_ref,
                    x1_ref, h2_ref, lg_ref, *, n_exp):
    fox = _rms(fox_ref[...].astype(F32)) * gfox_ref[...]
    mixed = jnp.concatenate([fox.astype(BF16), hg_ref[...]], axis=1)
    mix = jnp.dot(mixed, w_ref[...], preferred_element_type=F32)
    x1 = x_ref[...] + mod_ref[0, 2:3, :] * mix
    x1_ref[...] = x1
    h2 = _rms(x1) * g2_ref[...]
    h2 = h2 * (1.0 + mod_ref[0, 4:5, :]) + mod_ref[0, 3:4, :]
    h2_ref[...] = _pack_halves(h2)
    h_hi = h2.astype(BF16)
    h_lo = (h2 - h_hi.astype(F32)).astype(BF16)
    part = jnp.dot(h_hi, wr_ref[...], preferred_element_type=F32)
    part = part + jnp.dot(h_lo, wr_ref[...], preferred_element_type=F32)
    logits = part + pltpu.roll(part, LANES - n_exp, axis=1)
    lg_ref[...] = logits.T[0:n_exp, :] + br_ref[...]


def _outproj(fox2d, hg2d, x2d, mod3, g_fox, g2, w_out, w_router, b_router, seq):
    t, d = x2d.shape
    fox_w = fox2d.shape[1]
    n_exp = w_router.shape[1]
    assert 2 * n_exp <= LANES
    wr_hi = w_router.astype(BF16)
    wr_lo = (w_router - wr_hi.astype(F32)).astype(BF16)
    wr_cat = jnp.pad(jnp.concatenate([wr_hi, wr_lo], axis=1), ((0, 0), (0, LANES - 2 * n_exp)))
    tm = 256
    row = lambda i: (i, 0)
    const = lambda i: (0, 0)
    return pl.pallas_call(
        functools.partial(_outproj_kernel, n_exp=n_exp),
        out_shape=(jax.ShapeDtypeStruct((t, d), F32), jax.ShapeDtypeStruct((t, d // 2), jnp.uint32),
                   jax.ShapeDtypeStruct((n_exp, t), F32)),
        grid=(t // tm,),
        in_specs=[pl.BlockSpec((tm, fox_w), row),
                  pl.BlockSpec((tm, hg2d.shape[1]), row),
                  pl.BlockSpec((tm, d), row),
                  pl.BlockSpec((1, mod3.shape[1], d), lambda i: (i // (seq // tm), 0, 0)),
                  pl.BlockSpec((1, fox_w), const),
                  pl.BlockSpec((1, d), const),
                  pl.BlockSpec(w_out.shape, const),
                  pl.BlockSpec((d, LANES), const),
                  pl.BlockSpec((n_exp, 1), const)],
        out_specs=(pl.BlockSpec((tm, d), row), pl.BlockSpec((tm, d // 2), row),
                   pl.BlockSpec((n_exp, tm), lambda i: (0, i))),
        compiler_params=_params("arbitrary"),
        name="outproj_router",
    )(fox2d, hg2d, x2d, mod3, g_fox, g2, w_out, wr_cat, b_router)


def _route_kernel(lg_ref, pos_ref, gate_ref, cnt_ref, cnt_sc, run_sc, *, n_exp, tb):
    phase = pl.program_id(0)
    i = pl.program_id(1)

    @pl.when((phase == 0) & (i == 0))
    def _():
        cnt_sc[...] = jnp.zeros_like(cnt_sc)
        run_sc[...] = jnp.zeros_like(run_sc)

    logits = lg_ref[...]
    eidx = lax.broadcasted_iota(jnp.int32, (n_exp, tb), 0).astype(F32)
    work = logits
    vals, hots = [], []
    for _ in range(TOP_K):
        m = jnp.max(work, axis=0, keepdims=True)
        first = jnp.min(jnp.where(work == m, eidx, float(n_exp)), axis=0, keepdims=True)
        hot = eidx == first
        vals.append(m)
        hots.append(hot)
        work = jnp.where(hot, -jnp.inf, work)
    sel = hots[0] | hots[1] | hots[2] | hots[3]
    self32 = sel.astype(F32)

    @pl.when(phase == 0)
    def _():
        cnt_sc[...] += jnp.sum(self32, axis=1, keepdims=True)

    @pl.when(phase == 1)
    def _():
        cnt = cnt_sc[...]
        padded = jnp.ceil(cnt / MOE_BLOCK) * MOE_BLOCK
        r = lax.broadcasted_iota(jnp.int32, (n_exp, n_exp), 0)
        c = lax.broadcasted_iota(jnp.int32, (n_exp, n_exp), 1)
        strict = (c < r).astype(F32)
        pstart = jnp.dot(strict, jnp.broadcast_to(padded, (n_exp, LANES)), preferred_element_type=F32,
                         precision=HIGHEST)[:, 0:1]
        tr = lax.broadcasted_iota(jnp.int32, (tb, tb), 0)
        tc = lax.broadcasted_iota(jnp.int32, (tb, tb), 1)
        upper = (tr < tc).astype(BF16)
        rank = jnp.dot(sel.astype(BF16), upper, preferred_element_type=F32)
        base = pstart + run_sc[...] + rank
        exps = [jnp.exp(v - vals[0]) for v in vals]
        denom = exps[0] + exps[1] + exps[2] + exps[3]
        for k in range(TOP_K):
            pos_k = jnp.sum(jnp.where(hots[k], base, 0.0), axis=0, keepdims=True)
            pos_ref[pl.ds(k, 1), :] = pos_k.astype(jnp.int32)
            gate_ref[pl.ds(k, 1), :] = exps[k] / denom
        run_sc[...] += jnp.sum(self32, axis=1, keepdims=True)
        cnt_ref[...] = cnt


def _route(logits_t):
    n_exp, t = logits_t.shape
    tb = min(512, t)
    return pl.pallas_call(
        functools.partial(_route_kernel, n_exp=n_exp, tb=tb),
        out_shape=(jax.ShapeDtypeStruct((TOP_K, t), jnp.int32), jax.ShapeDtypeStruct((TOP_K, t), F32),
                   jax.ShapeDtypeStruct((n_exp, 1), F32)),
        grid=(2, t // tb),
        in_specs=[pl.BlockSpec((n_exp, tb), lambda p, i: (0, i))],
        out_specs=(pl.BlockSpec((TOP_K, tb), lambda p, i: (0, i * p)),
                   pl.BlockSpec((TOP_K, tb), lambda p, i: (0, i * p)),
                   pl.BlockSpec((n_exp, 1), lambda p, i: (0, 0))),
        scratch_shapes=[pltpu.VMEM((n_exp, 1), F32), pltpu.VMEM((n_exp, 1), F32)],
        compiler_params=_params("arbitrary", "arbitrary"),
        name="route_topk",
    )(logits_t)


def _row_copy(src_hbm, row, dst_vmem, slot, sem):
    return pltpu.make_async_copy(src_hbm.at[pl.ds(row, 1), :], dst_vmem.at[pl.ds(slot, 1), :], sem)


def _gather_rows(src_hbm, idx_ref, n_rows, dst_vmem, sem):
    for r in range(n_rows):
        _row_copy(src_hbm, idx_ref[0, 0, r], dst_vmem, r, sem).start()


def _wait_rows(src_hbm, n_rows, dst_vmem, sem):
    pltpu.make_async_copy(src_hbm.at[pl.ds(0, n_rows), :], dst_vmem, sem).wait()


def _dispatch_kernel(pad0_ref, padn_ref, nused_ref, pos_ref, h_ref, xs_hbm, zeros, sem, zsem,
                     *, tb, n_exp, n_blocks):
    @pl.when(pl.program_id(0) == 0)
    def _():
        zeros[...] = jnp.zeros_like(zeros)

        def each_fill(fn):
            def per_expert(e, carry):
                def per_row(r, inner):
                    fn(pltpu.make_async_copy(zeros.at[pl.ds(0, 1), :], xs_hbm.at[pl.ds(pad0_ref[e] + r, 1), :], zsem))
                    return inner

                lax.fori_loop(0, padn_ref[e], per_row, 0)
                return carry

            lax.fori_loop(0, n_exp, per_expert, 0)

            def per_block(b, carry):
                row0 = pl.multiple_of(b * MOE_BLOCK, MOE_BLOCK)
                fn(pltpu.make_async_copy(zeros, xs_hbm.at[pl.ds(row0, MOE_BLOCK), :], zsem))
                return carry

            lax.fori_loop(nused_ref[0], n_blocks, per_block, 0)

        each_fill(lambda cp: cp.start())
        each_fill(lambda cp: cp.wait())

    for r in range(tb):
        for k in range(TOP_K):
            pltpu.make_async_copy(h_ref.at[pl.ds(r, 1), :], xs_hbm.at[pl.ds(pos_ref[0, 0, k * tb + r], 1), :],
                                  sem).start()
    for k in range(TOP_K):
        pltpu.make_async_copy(h_ref, xs_hbm.at[pl.ds(0, tb), :], sem).wait()


def _dispatch(h2p, pos3, pad_start, pad_count, n_used, n_rows):
    t, dp = h2p.shape
    n_tiles = pos3.shape[0]
    tb = pos3.shape[2] // TOP_K
    return pl.pallas_call(
        functools.partial(_dispatch_kernel, tb=tb, n_exp=pad_start.shape[0], n_blocks=n_rows // MOE_BLOCK),
        out_shape=jax.ShapeDtypeStruct((n_rows, dp), h2p.dtype),
        grid_spec=pltpu.PrefetchScalarGridSpec(
            num_scalar_prefetch=3,
            grid=(n_tiles,),
            in_specs=[pl.BlockSpec((1, 1, TOP_K * tb), lambda i, p0, pn, nu: (i, 0, 0), memory_space=pltpu.SMEM),
                      pl.BlockSpec((tb, dp), lambda i, p0, pn, nu: (i, 0))],
            out_specs=pl.BlockSpec(memory_space=pl.ANY),
            scratch_shapes=[pltpu.VMEM((MOE_BLOCK, dp), h2p.dtype), pltpu.SemaphoreType.DMA(()),
                            pltpu.SemaphoreType.DMA(())]),
        compiler_params=_params("arbitrary"),
        name="moe_dispatch",
    )(pad_start, pad_count, n_used, pos3, h2p)


def _new_expert(be_ref, b):
    return (b == 0) | (be_ref[b] != be_ref[jnp.maximum(b - 1, 0)])


W_PIECES = 4


def _weight_copies(w_hbm, e, col0, stage, sem):
    rows = stage.shape[0] // W_PIECES
    return [pltpu.make_async_copy(w_hbm.at[e, pl.ds(i * rows, rows), pl.ds(col0, stage.shape[1])],
                                  stage.at[pl.ds(i * rows, rows), :], sem) for i in range(W_PIECES)]


def _convert_rows(src_f32, dst_bf16):
    rows = 16

    def body(i, carry):
        r0 = pl.multiple_of(i * rows, rows)
        dst_bf16[pl.ds(r0, rows), :] = src_f32[pl.ds(r0, rows), :].astype(BF16)
        return carry

    lax.fori_loop(0, src_f32.shape[0] // rows, body, 0, unroll=4)


def _gateup_kernel(be_ref, nxt_ref, last_ref, nused_ref, x_ref, w_hbm, bg_ref, bl_ref, o_ref,
                   stg_g, stg_l, wg_sc, wl_sc, sem, *, nf, tf):
    c = pl.program_id(0)
    b = pl.program_id(1)

    def slab(e, chunk):
        col = pl.multiple_of(chunk * tf, tf)
        return _weight_copies(w_hbm, e, col, stg_g, sem) + _weight_copies(w_hbm, e, col + nf * tf, stg_l, sem)

    @pl.when((c == 0) & (b == 0))
    def _():
        for cp in slab(be_ref[0], 0):
            cp.start()

    @pl.when(_new_expert(be_ref, b))
    def _():
        for cp in slab(0, 0):
            cp.wait()
        _convert_rows(stg_g, wg_sc)
        _convert_rows(stg_l, wl_sc)
        last_run = last_ref[b] == 1

        @pl.when(jnp.logical_not(last_run))
        def _():
            for cp in slab(nxt_ref[b], c):
                cp.start()

        @pl.when(last_run & (c + 1 < nf))
        def _():
            for cp in slab(be_ref[0], c + 1):
                cp.start()

    @pl.when(b < nused_ref[0])
    def _():
        x = jnp.concatenate(_unpack_halves(x_ref[...]), axis=1)
        glu = jnp.dot(x, wg_sc[...], preferred_element_type=F32) + bg_ref[0]
        lin = jnp.dot(x, wl_sc[...], preferred_element_type=F32) + bl_ref[0]
        glu = jnp.minimum(glu, SWIGLU_LIMIT)
        lin = jnp.clip(lin, -SWIGLU_LIMIT, SWIGLU_LIMIT)
        o_ref[...] = (glu * _sigmoid(SWIGLU_ALPHA * glu) * (lin + 1.0)).astype(BF16)

    @pl.when(b >= nused_ref[0])
    def _():
        o_ref[...] = jnp.zeros_like(o_ref)


def _gateup(runs, n_used, xs, w_gu, b_gu3):
    n_rows, dp = xs.shape
    d = 2 * dp
    n_blocks = n_rows // MOE_BLOCK
    d_ff = w_gu.shape[2] // 2
    tf = min(1024, d_ff)
    nf = d_ff // tf
    return pl.pallas_call(
        functools.partial(_gateup_kernel, nf=nf, tf=tf),
        out_shape=jax.ShapeDtypeStruct((n_rows, d_ff), BF16),
        grid_spec=pltpu.PrefetchScalarGridSpec(
            num_scalar_prefetch=4,
            grid=(nf, n_blocks),
            in_specs=[pl.BlockSpec((MOE_BLOCK, dp), lambda c, b, be, nx, la, nu: (jnp.minimum(b, nu[0] - 1), 0)),
                      pl.BlockSpec(memory_space=pl.ANY),
                      pl.BlockSpec((1, 1, tf), lambda c, b, be, nx, la, nu: (be[b], 0, c)),
                      pl.BlockSpec((1, 1, tf), lambda c, b, be, nx, la, nu: (be[b], 0, nf + c))],
            out_specs=pl.BlockSpec((MOE_BLOCK, tf), lambda c, b, be, nx, la, nu: (b, c)),
            scratch_shapes=[pltpu.VMEM((d, tf), F32), pltpu.VMEM((d, tf), F32),
                            pltpu.VMEM((d, tf), BF16), pltpu.VMEM((d, tf), BF16),
                            pltpu.SemaphoreType.DMA(())]),
        compiler_params=_params("arbitrary", "arbitrary"),
        name="moe_gateup",
    )(*runs, n_used, xs, w_gu, b_gu3, b_gu3)


def _down_kernel(be_ref, nxt_ref, last_ref, nused_ref, a_ref, w_hbm, b_ref, o_ref, stg, w_sc, sem):
    b = pl.program_id(0)

    @pl.when(b == 0)
    def _():
        for cp in _weight_copies(w_hbm, be_ref[0], 0, stg, sem):
            cp.start()

    @pl.when(_new_expert(be_ref, b))
    def _():
        for cp in _weight_copies(w_hbm, 0, 0, stg, sem):
            cp.wait()
        _convert_rows(stg, w_sc)

        @pl.when(last_ref[b] == 0)
        def _():
            for cp in _weight_copies(w_hbm, nxt_ref[b], 0, stg, sem):
                cp.start()

    @pl.when(b < nused_ref[0])
    def _():
        y = jnp.dot(a_ref[...], w_sc[...], preferred_element_type=F32) + b_ref[0]
        o_ref[...] = _pack_halves(y)

    @pl.when(b >= nused_ref[0])
    def _():
        o_ref[...] = jnp.zeros_like(o_ref)


def _down(runs, n_used, act, w_down, b_down3):
    n_rows, d_ff = act.shape
    n_blocks = n_rows // MOE_BLOCK
    d = w_down.shape[2]
    return pl.pallas_call(
        _down_kernel,
        out_shape=jax.ShapeDtypeStruct((n_rows, d // 2), jnp.uint32),
        grid_spec=pltpu.PrefetchScalarGridSpec(
            num_scalar_prefetch=4,
            grid=(n_blocks,),
            in_specs=[pl.BlockSpec((MOE_BLOCK, d_ff), lambda b, be, nx, la, nu: (b, 0)),
                      pl.BlockSpec(memory_space=pl.ANY),
                      pl.BlockSpec((1, 1, d), lambda b, be, nx, la, nu: (be[b], 0, 0))],
            out_specs=pl.BlockSpec((MOE_BLOCK, d // 2), lambda b, be, nx, la, nu: (b, 0)),
            scratch_shapes=[pltpu.VMEM((d_ff, d), F32), pltpu.VMEM((d_ff, d), BF16),
                            pltpu.SemaphoreType.DMA(())]),
        compiler_params=_params("arbitrary"),
        name="moe_down",
    )(*runs, n_used, act, w_down, b_down3)


def _combine_kernel(pos_ref, pos_next_ref, y_hbm, gate_ref, x1_ref, mod_ref, gfin_ref, o_ref, buf, sem,
                    *, tb, n_tiles, last_layer):
    i = pl.program_id(0)
    slot = i % 2

    @pl.when(i == 0)
    def _():
        _gather_rows(y_hbm, pos_ref, TOP_K * tb, buf.at[0], sem.at[0])

    @pl.when(i + 1 < n_tiles)
    def _():
        _gather_rows(y_hbm, pos_next_ref, TOP_K * tb, buf.at[1 - slot], sem.at[1 - slot])

    _wait_rows(y_hbm, TOP_K * tb, buf.at[slot], sem.at[slot])
    gates = gate_ref[...]
    moe_lo = moe_hi = None
    for k in range(TOP_K):
        lo, hi = _unpack_halves(buf[slot, pl.ds(k * tb, tb), :])
        g = gates[:, k:k + 1]
        moe_lo = g * lo.astype(F32) if k == 0 else moe_lo + g * lo.astype(F32)
        moe_hi = g * hi.astype(F32) if k == 0 else moe_hi + g * hi.astype(F32)
    moe = jnp.concatenate([moe_lo, moe_hi], axis=1)
    x2 = x1_ref[...] + mod_ref[0, 5:6, :] * moe
    o_ref[...] = _rms(x2) * gfin_ref[...] if last_layer else x2


def _combine(pos3, y_buf, gates, x1, mod3, g_final, seq, last_layer):
    t, d = x1.shape
    n_tiles = pos3.shape[0]
    tb = pos3.shape[2] // TOP_K
    idx_spec = lambda step: pl.BlockSpec((1, 1, tb * TOP_K), lambda i: (jnp.minimum(i + step, n_tiles - 1), 0, 0),
                                         memory_space=pltpu.SMEM)
    return pl.pallas_call(
        functools.partial(_combine_kernel, tb=tb, n_tiles=n_tiles, last_layer=last_layer),
        out_shape=jax.ShapeDtypeStruct((t, d), F32),
        grid=(n_tiles,),
        in_specs=[idx_spec(0), idx_spec(1),
                  pl.BlockSpec(memory_space=pl.ANY),
                  pl.BlockSpec((tb, TOP_K), lambda i: (i, 0)),
                  pl.BlockSpec((tb, d), lambda i: (i, 0)),
                  pl.BlockSpec((1, mod3.shape[1], d), lambda i: (i // (seq // tb), 0, 0)),
                  pl.BlockSpec((1, d), lambda i: (0, 0))],
        out_specs=pl.BlockSpec((tb, d), lambda i: (i, 0)),
        scratch_shapes=[pltpu.VMEM((2, TOP_K * tb, y_buf.shape[1]), y_buf.dtype), pltpu.SemaphoreType.DMA((2,))],
        compiler_params=_params("arbitrary"),
        name="moe_combine",
    )(pos3, pos3, y_buf, gates, x1, mod3, g_final)


def kernel(x, c, w_ada, b_ada, g_norm1, g_norm2, w_in, b_fgate, g_fox_out, lb_logits, g_hg_out,
           w_out, w_router, b_router, w_gu, b_gu, w_down, b_down, g_final):
    n_batch, seq, d = x.shape
    t = n_batch * seq
    depth = w_ada.shape[0]
    fox_heads = b_fgate.shape[1]
    fox_w = g_fox_out.shape[1]
    hg_w = g_hg_out.shape[1]
    hg_heads = hg_w // HEAD_DIM
    n_exp = w_router.shape[2]
    assert fox_w == fox_heads * HEAD_DIM and fox_heads <= LANES
    n_blocks = -(-t * TOP_K // MOE_BLOCK) + n_exp
    n_mod = w_ada.shape[2] // d

    x2d = x.reshape(t, d)
    for l in range(depth):
        mod3 = _ada_mod(c, w_ada[l], b_ada[l]).reshape(n_batch, n_mod, d)

        w_l = w_in[l].astype(BF16)
        w_main = jnp.concatenate([w_l[:, :3 * fox_w], w_l[:, 3 * fox_w + fox_heads:]], axis=1)
        w_fg = jnp.pad(w_l[:, 3 * fox_w:3 * fox_w + fox_heads], ((0, 0), (0, LANES - fox_heads)))
        b_fg = jnp.pad(b_fgate[l], (0, LANES - fox_heads)).reshape(1, LANES)
        proj, cum = _inproj(x2d, mod3, g_norm1[l].reshape(1, d), w_main, w_fg, b_fg, seq)
        proj3 = proj.reshape(n_batch, seq, -1)

        fox = _fox_attention(proj3, cum.reshape(n_batch, seq, LANES), fox_heads)

        lb3 = lb_logits.reshape(lb_logits.shape[0], hg_heads, HEAD_DIM).transpose(1, 0, 2)
        hg = _hgrn2(proj3, lb3, g_hg_out[l].reshape(hg_heads, 1, HEAD_DIM), hg_heads, 3 * fox_heads, l)

        x1, h2p, logits_t = _outproj(
            fox.reshape(t, fox_w), hg.reshape(t, hg_w), x2d, mod3, g_fox_out[l].reshape(1, fox_w),
            g_norm2[l].reshape(1, d), w_out[l].astype(BF16), w_router[l], b_router[l].reshape(n_exp, 1), seq)

        pos_t, gates_t, counts = _route(logits_t)

        cnt = counts[:, 0].astype(jnp.int32)
        padded = (cnt + MOE_BLOCK - 1) // MOE_BLOCK * MOE_BLOCK
        padded_end = jnp.cumsum(padded)
        block_row0 = jnp.arange(n_blocks, dtype=jnp.int32) * MOE_BLOCK
        block_e = jnp.sum((block_row0[:, None] >= padded_end[None, :]).astype(jnp.int32), axis=1)
        n_used = (padded_end[-1:] // MOE_BLOCK).astype(jnp.int32)
        block_id = jnp.arange(n_blocks, dtype=jnp.int32)
        block_e = jnp.minimum(block_e, jnp.sum(jnp.where(block_id == n_used - 1, block_e, 0)))
        change_at = jnp.where(block_e != jnp.roll(block_e, 1), block_id, n_blocks).at[0].set(n_blocks)
        next_change = jnp.flip(lax.cummin(jnp.flip(jnp.roll(change_at, -1).at[-1].set(n_blocks))))
        last_run = (next_change >= n_blocks).astype(jnp.int32)
        next_e = block_e[jnp.minimum(next_change, n_blocks - 1)]
        runs = (block_e, next_e, last_run)

        def tile_major(tb):
            return pos_t.reshape(TOP_K, t // tb, tb).transpose(1, 0, 2).reshape(t // tb, 1, TOP_K * tb)

        pad_count = padded - cnt
        xs = _dispatch(h2p, tile_major(min(256, seq)), padded_end - pad_count, pad_count, n_used,
                       n_blocks * MOE_BLOCK)
        act = _gateup(runs, n_used, xs, w_gu[l], b_gu[l].reshape(n_exp, 1, -1))
        y_buf = _down(runs, n_used, act, w_down[l], b_down[l].reshape(n_exp, 1, d))
        x2d = _combine(tile_major(min(128, seq)), y_buf, gates_t.T, x1, mod3, g_final.reshape(1, d), seq,
                       l == depth - 1)
    return x2d.reshape(n_batch, seq, d)
```

```python
import functools

import jax
import jax.numpy as jnp
from jax import lax
from jax.experimental import pallas as pl
from jax.experimental.pallas import tpu as pltpu

HEAD_DIM = 128
TOP_K = 4
MOE_BLOCK = 256
HG_CHUNK = 64
NORM_EPS = 1e-6
SWIGLU_ALPHA = 1.702
SWIGLU_LIMIT = 7.0
LANES = 128
VMEM_LIMIT = 56 * 1024 * 1024

F32 = jnp.float32
BF16 = jnp.bfloat16
HIGHEST = lax.Precision.HIGHEST
NT_DIMS = (((1,), (1,)), ((), ()))
TN_DIMS = (((0,), (0,)), ((), ()))


def _params(*semantics):
    return pltpu.CompilerParams(dimension_semantics=semantics, vmem_limit_bytes=VMEM_LIMIT)


def _sigmoid(z):
    return 1.0 / (1.0 + jnp.exp(-z))


def _rms(v):
    return v * lax.rsqrt(jnp.mean(v * v, axis=-1, keepdims=True) + NORM_EPS)


def _ada_kernel(ct_ref, w_ref, b_ref, o_ref, *, n_batch, d_model):
    ct = ct_ref[...]
    cond = ct * _sigmoid(ct)
    rows = 256
    for b in range(n_batch):
        acc = jnp.zeros((8, w_ref.shape[1]), F32)
        for i in range(d_model // rows):
            w3 = w_ref[pl.ds(i * rows, rows), :].reshape(rows // 8, 8, -1)
            c3 = cond[i * rows:(i + 1) * rows, b:b + 1].reshape(rows // 8, 8, 1)
            acc = acc + jnp.sum(w3 * c3, axis=0)
        o_ref[pl.ds(b, 1), :] = jnp.sum(acc, axis=0, keepdims=True) + b_ref[...]


def _ada_mod(c, w_ada, b_ada):
    n_batch, d = c.shape
    n = w_ada.shape[1]
    tn = 1024
    return pl.pallas_call(
        functools.partial(_ada_kernel, n_batch=n_batch, d_model=d),
        out_shape=jax.ShapeDtypeStruct((n_batch, n), F32),
        grid=(n // tn,),
        in_specs=[pl.BlockSpec((d, n_batch), lambda j: (0, 0)),
                  pl.BlockSpec((d, tn), lambda j: (0, j)),
                  pl.BlockSpec((1, tn), lambda j: (0, j))],
        out_specs=pl.BlockSpec((n_batch, tn), lambda j: (0, j)),
        compiler_params=_params("arbitrary"),
        name="ada_mod",
    )(c.T, w_ada, b_ada.reshape(1, n))


def _inproj_kernel(x_ref, mod_ref, g_ref, w_ref, wf_ref, bf_ref, proj_ref, cum_ref, h_sc, carry_sc,
                   *, tiles_per_batch, sub):
    i = pl.program_id(0)
    j = pl.program_id(1)

    @pl.when(j == 0)
    def _():
        x = x_ref[...]
        h = _rms(x) * (g_ref[...] * (1.0 + mod_ref[0, 1:2, :])) + mod_ref[0, 0:1, :]
        hb = h.astype(BF16)
        h_sc[...] = hb
        z = jnp.dot(hb, wf_ref[...], preferred_element_type=F32) + bf_ref[...]
        logf = jnp.minimum(z, 0.0) - jnp.log(1.0 + jnp.exp(-jnp.abs(z)))

        @pl.when(i % tiles_per_batch == 0)
        def _():
            carry_sc[...] = jnp.zeros_like(carry_sc)

        r = lax.broadcasted_iota(jnp.int32, (sub, sub), 0)
        c = lax.broadcasted_iota(jnp.int32, (sub, sub), 1)
        tri = (r >= c).astype(BF16)
        carry = carry_sc[...]
        for s in range(x.shape[0] // sub):
            blk = logf[s * sub:(s + 1) * sub, :]
            hi = blk.astype(BF16)
            rest = blk - hi.astype(F32)
            mid = rest.astype(BF16)
            low = (rest - mid.astype(F32)).astype(BF16)
            two = jnp.dot(tri, jnp.concatenate([hi, mid], axis=1), preferred_element_type=F32)
            cs = two[:, :LANES] + two[:, LANES:] + jnp.dot(tri, low, preferred_element_type=F32) + carry
            cum_ref[pl.ds(s * sub, sub), :] = cs
            carry = cs[sub - 1:sub, :]
        carry_sc[...] = carry

    proj_ref[...] = jnp.dot(h_sc[...], w_ref[...], preferred_element_type=F32).astype(BF16)


def _inproj(x2d, mod3, g1, w_main, w_fg, b_fg, seq):
    t, d = x2d.shape
    n = w_main.shape[1]
    tm = min(1024, seq)
    tn = next(c for c in (1024, 512, 256, 128) if n % c == 0)
    return pl.pallas_call(
        functools.partial(_inproj_kernel, tiles_per_batch=seq // tm, sub=256),
        out_shape=(jax.ShapeDtypeStruct((t, n), BF16), jax.ShapeDtypeStruct((t, LANES), F32)),
        grid=(t // tm, n // tn),
        in_specs=[pl.BlockSpec((tm, d), lambda i, j: (i, 0)),
                  pl.BlockSpec((1, mod3.shape[1], d), lambda i, j: (i // (seq // tm), 0, 0)),
                  pl.BlockSpec((1, d), lambda i, j: (0, 0)),
                  pl.BlockSpec((d, tn), lambda i, j: (0, j)),
                  pl.BlockSpec((d, LANES), lambda i, j: (0, 0)),
                  pl.BlockSpec((1, LANES), lambda i, j: (0, 0))],
        out_specs=(pl.BlockSpec((tm, tn), lambda i, j: (i, j)),
                   pl.BlockSpec((tm, LANES), lambda i, j: (i, 0))),
        scratch_shapes=[pltpu.VMEM((tm, d), BF16), pltpu.VMEM((1, LANES), F32)],
        compiler_params=_params("arbitrary", "arbitrary"),
        name="inproj",
    )(x2d, mod3, g1, w_main, w_fg, b_fg)


N_BIAS = 3
LOG2E = 1.4426950408889634


def _attn_kernel(q_ref, k_ref, v_ref, cum_ref, o_ref, kaug_sc, vt_sc, qt_sc, sa_sc, sb_sc, m_sc, l_sc, acc_sc,
                 *, blk, n_kv):
    h = pl.program_id(1)
    i = pl.program_id(2)

    @pl.when(i == 0)
    def _():
        lane = lax.broadcasted_iota(jnp.int32, (blk, LANES), 1)

        def prep(j, carry):
            start = pl.multiple_of(j * blk, blk)
            cum = cum_ref[0, pl.ds(start, blk), :]
            rest = jnp.sum(jnp.where(lane == h, cum, 0.0), axis=1, keepdims=True) * LOG2E
            bias = jnp.zeros((blk, LANES), F32)
            for piece in range(N_BIAS):
                part = rest.astype(BF16).astype(F32)
                bias = jnp.where(lane == piece, part, bias)
                rest = rest - part
            kaug_sc[j, :, 0:HEAD_DIM] = k_ref[0, pl.ds(start, blk), :]
            kaug_sc[j, :, HEAD_DIM:2 * HEAD_DIM] = bias.astype(BF16)
            vt_sc[j] = v_ref[0, pl.ds(start, blk), :].astype(F32).T.astype(BF16)
            return carry

        lax.fori_loop(0, n_kv, prep, 0)

    qs = q_ref[0].astype(F32) * (HEAD_DIM ** -0.5 * LOG2E)
    qt_sc[0:HEAD_DIM, :] = qs.T.astype(BF16)
    row = lax.broadcasted_iota(jnp.int32, (HEAD_DIM, blk), 0)
    qt_sc[HEAD_DIM:2 * HEAD_DIM, :] = jnp.where(row < N_BIAS, -1.0, 0.0).astype(BF16)
    m_sc[...] = jnp.full_like(m_sc, -jnp.inf)
    l_sc[...] = jnp.zeros_like(l_sc)
    acc_sc[...] = jnp.zeros_like(acc_sc)

    def scores(j):
        return jnp.dot(kaug_sc[j], qt_sc[...], preferred_element_type=F32)

    def update(j, s):
        m_prev = m_sc[...]
        m_new = jnp.maximum(m_prev, jnp.max(s, axis=0, keepdims=True))
        alpha = jnp.exp2(m_prev - m_new)
        p = jnp.exp2(s - m_new)
        l_sc[...] = alpha * l_sc[...] + jnp.sum(p, axis=0, keepdims=True)
        acc_sc[...] = alpha * acc_sc[...] + jnp.dot(vt_sc[j], p.astype(BF16), preferred_element_type=F32)
        m_sc[...] = m_new

    def causal(s):
        key = lax.broadcasted_iota(jnp.int32, (blk, blk), 0)
        qry = lax.broadcasted_iota(jnp.int32, (blk, blk), 1)
        return jnp.where(key <= qry, s, -jnp.inf)

    sa_sc[...] = scores(0)

    def pair_at(j):
        sb_sc[...] = scores(j + 1)
        update(j, sa_sc[...])
        sa_sc[...] = scores(j + 2)
        update(j + 1, sb_sc[...])

    def quad(jj, carry):
        pair_at(4 * jj)
        pair_at(4 * jj + 2)
        return carry

    def pair(jj, carry):
        pair_at(4 * (i // 4) + 2 * jj)
        return carry

    lax.fori_loop(0, i // 4, quad, 0)
    lax.fori_loop(0, (i % 4) // 2, pair, 0)

    @pl.when(i % 2 == 0)
    def _():
        update(i, causal(sa_sc[...]))

    @pl.when(i % 2 == 1)
    def _():
        sb_sc[...] = scores(i)
        update(i - 1, sa_sc[...])
        update(i, causal(sb_sc[...]))

    o_ref[0] = (acc_sc[...] / l_sc[...]).T.astype(BF16)


def _fox_attention(proj3, cum3, n_heads):
    n_batch, seq, _ = proj3.shape
    blk = min(512, seq)
    n_kv = seq // blk
    return pl.pallas_call(
        functools.partial(_attn_kernel, blk=blk, n_kv=n_kv),
        out_shape=jax.ShapeDtypeStruct((n_batch, seq, n_heads * HEAD_DIM), BF16),
        grid=(n_batch, n_heads, n_kv),
        in_specs=[pl.BlockSpec((1, blk, HEAD_DIM), lambda b, h, i: (b, i, h)),
                  pl.BlockSpec((1, seq, HEAD_DIM), lambda b, h, i: (b, 0, n_heads + h)),
                  pl.BlockSpec((1, seq, HEAD_DIM), lambda b, h, i: (b, 0, 2 * n_heads + h)),
                  pl.BlockSpec((1, seq, LANES), lambda b, h, i: (b, 0, 0))],
        out_specs=pl.BlockSpec((1, blk, HEAD_DIM), lambda b, h, i: (b, i, h)),
        scratch_shapes=[pltpu.VMEM((n_kv, blk, 2 * HEAD_DIM), BF16),
                        pltpu.VMEM((n_kv, HEAD_DIM, blk), BF16),
                        pltpu.VMEM((2 * HEAD_DIM, blk), BF16),
                        pltpu.VMEM((blk, blk), F32), pltpu.VMEM((blk, blk), F32),
                        pltpu.VMEM((1, blk), F32), pltpu.VMEM((1, blk), F32),
                        pltpu.VMEM((HEAD_DIM, blk), F32)],
        compiler_params=_params("arbitrary", "arbitrary", "arbitrary"),
        name="fox_attention",
    )(proj3, proj3, proj3, cum3)


def _hgrn_kernel(q_ref, f_ref, i_ref, g_ref, lbl_ref, gn_ref, o_ref, st_sc, *, rows, sub, layer):
    @pl.when(pl.program_id(2) == 0)
    def _():
        st_sc[...] = jnp.zeros_like(st_sc)

    lbl = lbl_ref[0]
    e = jnp.exp(lbl - jnp.max(lbl, axis=0, keepdims=True))
    lb = jnp.sum(e[0:layer + 1, :], axis=0, keepdims=True) / jnp.sum(e, axis=0, keepdims=True)
    f = lb + (1.0 - lb) * _sigmoid(f_ref[0].astype(F32))
    logf = jnp.log(f)
    kk = 1.0 - f
    qf = q_ref[0].astype(F32)
    qq = qf * _sigmoid(qf)

    n_ch = sub // HG_CHUNK
    r = lax.broadcasted_iota(jnp.int32, (sub, sub), 0)
    c = lax.broadcasted_iota(jnp.int32, (sub, sub), 1)
    within = (r >= c) & (r // HG_CHUNK == c // HG_CHUNK)
    tri = within.astype(BF16)
    rw = lax.broadcasted_iota(jnp.int32, (sub, n_ch * HEAD_DIM), 0)
    cw = lax.broadcasted_iota(jnp.int32, (sub, n_ch * HEAD_DIM), 1)
    own_block = rw // HG_CHUNK == cw // HEAD_DIM

    zero = jnp.zeros((), BF16)
    n_sub = rows // sub
    b_all, b_last_all = [], []
    for s in range(n_sub):
        lf = logf[s * sub:(s + 1) * sub, :]
        hi = lf.astype(BF16)
        rest = lf - hi.astype(F32)
        mid = rest.astype(BF16)
        low = (rest - mid.astype(F32)).astype(BF16)
        two = jnp.dot(tri, jnp.concatenate([hi, mid], axis=1), preferred_element_type=F32)
        b = two[:, :HEAD_DIM] + two[:, HEAD_DIM:] + jnp.dot(tri, low, preferred_element_type=F32)
        b_all.append(b)
        b_last_all.append([b[(n + 1) * HG_CHUNK - 1:(n + 1) * HG_CHUNK, :] for n in range(n_ch)])

    q_dec_all, o_all, upd_all = [], [], []
    for s in range(n_sub):
        rs = slice(s * sub, (s + 1) * sub)
        b = b_all[s]
        b_last_rows = jnp.concatenate([jnp.broadcast_to(bl, (HG_CHUNK, HEAD_DIM)) for bl in b_last_all[s]], axis=0)
        q_dec = (qq[rs, :] * jnp.exp(b)).astype(BF16)
        k_inv = (kk[rs, :] * jnp.exp(-b)).astype(BF16)
        k_tail = (kk[rs, :] * jnp.exp(b_last_rows - b)).astype(BF16)
        vv = i_ref[0, pl.ds(s * sub, sub), :]
        attn = lax.dot_general(q_dec, k_inv, NT_DIMS, preferred_element_type=F32)
        o_all.append(jnp.dot(jnp.where(within, attn, 0.0).astype(BF16), vv, preferred_element_type=F32))
        k_blocks = jnp.where(own_block, jnp.concatenate([k_tail] * n_ch, axis=1), zero)
        upd_all.append(lax.dot_general(vv, k_blocks, TN_DIMS, preferred_element_type=F32))
        q_dec_all.append(q_dec)

    st = st_sc[...]
    for s in range(n_sub):
        states = []
        for n in range(n_ch):
            states.append(st.astype(BF16))
            st = jnp.exp(b_last_all[s][n]) * st + upd_all[s][:, n * HEAD_DIM:(n + 1) * HEAD_DIM]
        q_blocks = jnp.where(own_block, jnp.concatenate([q_dec_all[s]] * n_ch, axis=1), zero)
        o = o_all[s] + lax.dot_general(q_blocks, jnp.concatenate(states, axis=1), NT_DIMS,
                                       preferred_element_type=F32)
        y = _rms(o) * gn_ref[0]
        gf = g_ref[0, pl.ds(s * sub, sub), :].astype(F32)
        o_ref[0, pl.ds(s * sub, sub), :] = (y * (gf * _sigmoid(gf))).astype(BF16)
    st_sc[...] = st


def _hgrn2(proj3, lb_logits3, g_hg3, n_heads, col0, layer):
    n_batch, seq, _ = proj3.shape
    rows = min(1024, seq)
    sub = min(256, rows)
    spec = lambda off: pl.BlockSpec((1, rows, HEAD_DIM), lambda b, h, r: (b, r, col0 + off * n_heads + h))
    return pl.pallas_call(
        functools.partial(_hgrn_kernel, rows=rows, sub=sub, layer=layer),
        out_shape=jax.ShapeDtypeStruct((n_batch, seq, n_heads * HEAD_DIM), BF16),
        grid=(n_batch, n_heads, seq // rows),
        in_specs=[spec(0), spec(1), spec(2), spec(3),
                  pl.BlockSpec((1, lb_logits3.shape[1], HEAD_DIM), lambda b, h, r: (h, 0, 0)),
                  pl.BlockSpec((1, 1, HEAD_DIM), lambda b, h, r: (h, 0, 0))],
        out_specs=pl.BlockSpec((1, rows, HEAD_DIM), lambda b, h, r: (b, r, h)),
        scratch_shapes=[pltpu.VMEM((HEAD_DIM, HEAD_DIM), F32)],
        compiler_params=_params("arbitrary", "arbitrary", "arbitrary"),
        name="hgrn2",
    )(proj3, proj3, proj3, proj3, lb_logits3, g_hg3)


def _pack_halves(v):
    n = v.shape[1] // 2
    lo = lax.bitcast_convert_type(v[:, :n].astype(BF16).astype(F32), jnp.uint32) >> 16
    hi = lax.bitcast_convert_type(v[:, n:].astype(BF16).astype(F32), jnp.uint32) & jnp.uint32(0xFFFF0000)
    return lo | hi


def _unpack_halves(p):
    lo = lax.bitcast_convert_type(p << 16, F32).astype(BF16)
    hi = lax.bitcast_convert_type(p & jnp.uint32(0xFFFF0000), F32).astype(BF16)
    return lo, hi


def _outproj_kernel(fox_ref, hg_ref, x_ref, mod_ref, gfox_ref, g2_ref, w_ref, wr_ref, br_ref,
                    x1_ref, h2_ref, lg_ref, *, n_exp, group):
    groups = [pl.ds(k * group, group) for k in range(x_ref.shape[0] // group)]
    mixed = []
    for rows in groups:
        fox = _rms(fox_ref[rows, :].astype(F32)) * gfox_ref[...]
        mixed.append(jnp.concatenate([fox.astype(BF16), hg_ref[rows, :]], axis=1))
    mix = [jnp.dot(m, w_ref[...], preferred_element_type=F32) for m in mixed]
    h2s = []
    for rows, m in zip(groups, mix):
        x1 = x_ref[rows, :] + mod_ref[0, 2:3, :] * m
        x1_ref[rows, :] = x1
        h2 = _rms(x1) * (g2_ref[...] * (1.0 + mod_ref[0, 4:5, :])) + mod_ref[0, 3:4, :]
        h2_ref[rows, :] = _pack_halves(h2)
        h2s.append(h2)
    for rows, h2 in zip(groups, h2s):
        h_hi = h2.astype(BF16)
        h_lo = (h2 - h_hi.astype(F32)).astype(BF16)
        part = jnp.dot(h_hi, wr_ref[...], preferred_element_type=F32)
        part = part + jnp.dot(h_lo, wr_ref[...], preferred_element_type=F32)
        logits = part + pltpu.roll(part, LANES - n_exp, axis=1)
        lg_ref[:, rows] = logits.T[0:n_exp, :] + br_ref[...]


def _outproj(fox2d, hg2d, x2d, mod3, g_fox, g2, w_out, w_router, b_router, seq):
    t, d = x2d.shape
    fox_w = fox2d.shape[1]
    n_exp = w_router.shape[1]
    assert 2 * n_exp <= LANES
    wr_hi = w_router.astype(BF16)
    wr_lo = (w_router - wr_hi.astype(F32)).astype(BF16)
    wr_cat = jnp.pad(jnp.concatenate([wr_hi, wr_lo], axis=1), ((0, 0), (0, LANES - 2 * n_exp)))
    tm = 512
    row = lambda i: (i, 0)
    const = lambda i: (0, 0)
    return pl.pallas_call(
        functools.partial(_outproj_kernel, n_exp=n_exp, group=256),
        out_shape=(jax.ShapeDtypeStruct((t, d), F32), jax.ShapeDtypeStruct((t, d // 2), jnp.uint32),
                   jax.ShapeDtypeStruct((n_exp, t), F32)),
        grid=(t // tm,),
        in_specs=[pl.BlockSpec((tm, fox_w), row),
                  pl.BlockSpec((tm, hg2d.shape[1]), row),
                  pl.BlockSpec((tm, d), row),
                  pl.BlockSpec((1, mod3.shape[1], d), lambda i: (i // (seq // tm), 0, 0)),
                  pl.BlockSpec((1, fox_w), const),
                  pl.BlockSpec((1, d), const),
                  pl.BlockSpec(w_out.shape, const),
                  pl.BlockSpec((d, LANES), const),
                  pl.BlockSpec((n_exp, 1), const)],
        out_specs=(pl.BlockSpec((tm, d), row), pl.BlockSpec((tm, d // 2), row),
                   pl.BlockSpec((n_exp, tm), lambda i: (0, i))),
        compiler_params=_params("arbitrary"),
        name="outproj_router",
    )(fox2d, hg2d, x2d, mod3, g_fox, g2, w_out, wr_cat, b_router)


def _route_kernel(lg_ref, pos_ref, gate_ref, cnt_ref, cnt_sc, run_sc, *, n_exp, tb):
    phase = pl.program_id(0)
    i = pl.program_id(1)

    @pl.when((phase == 0) & (i == 0))
    def _():
        cnt_sc[...] = jnp.zeros_like(cnt_sc)
        run_sc[...] = jnp.zeros_like(run_sc)

    logits = lg_ref[...]
    eidx = lax.broadcasted_iota(jnp.int32, (n_exp, tb), 0).astype(F32)
    work = logits
    vals, hots = [], []
    for _ in range(TOP_K):
        m = jnp.max(work, axis=0, keepdims=True)
        first = jnp.min(jnp.where(work == m, eidx, float(n_exp)), axis=0, keepdims=True)
        hot = eidx == first
        vals.append(m)
        hots.append(hot)
        work = jnp.where(hot, -jnp.inf, work)
    sel = hots[0] | hots[1] | hots[2] | hots[3]
    self32 = sel.astype(F32)

    @pl.when(phase == 0)
    def _():
        cnt_sc[...] += jnp.sum(self32, axis=1, keepdims=True)

    @pl.when(phase == 1)
    def _():
        cnt = cnt_sc[...]
        padded = jnp.ceil(cnt / MOE_BLOCK) * MOE_BLOCK
        r = lax.broadcasted_iota(jnp.int32, (n_exp, n_exp), 0)
        c = lax.broadcasted_iota(jnp.int32, (n_exp, n_exp), 1)
        strict = (c < r).astype(F32)
        pstart = jnp.dot(strict, jnp.broadcast_to(padded, (n_exp, LANES)), preferred_element_type=F32,
                         precision=HIGHEST)[:, 0:1]
        tr = lax.broadcasted_iota(jnp.int32, (tb, tb), 0)
        tc = lax.broadcasted_iota(jnp.int32, (tb, tb), 1)
        upper = (tr < tc).astype(BF16)
        rank = jnp.dot(sel.astype(BF16), upper, preferred_element_type=F32)
        base = pstart + run_sc[...] + rank
        exps = [jnp.exp(v - vals[0]) for v in vals]
        denom = exps[0] + exps[1] + exps[2] + exps[3]
        for k in range(TOP_K):
            pos_k = jnp.sum(jnp.where(hots[k], base, 0.0), axis=0, keepdims=True)
            pos_ref[pl.ds(k, 1), :] = pos_k.astype(jnp.int32)
            gate_ref[pl.ds(k, 1), :] = exps[k] / denom
        run_sc[...] += jnp.sum(self32, axis=1, keepdims=True)
        cnt_ref[...] = cnt


def _route(logits_t):
    n_exp, t = logits_t.shape
    tb = min(512, t)
    return pl.pallas_call(
        functools.partial(_route_kernel, n_exp=n_exp, tb=tb),
        out_shape=(jax.ShapeDtypeStruct((TOP_K, t), jnp.int32), jax.ShapeDtypeStruct((TOP_K, t), F32),
                   jax.ShapeDtypeStruct((n_exp, 1), F32)),
        grid=(2, t // tb),
        in_specs=[pl.BlockSpec((n_exp, tb), lambda p, i: (0, i))],
        out_specs=(pl.BlockSpec((TOP_K, tb), lambda p, i: (0, i * p)),
                   pl.BlockSpec((TOP_K, tb), lambda p, i: (0, i * p)),
                   pl.BlockSpec((n_exp, 1), lambda p, i: (0, 0))),
        scratch_shapes=[pltpu.VMEM((n_exp, 1), F32), pltpu.VMEM((n_exp, 1), F32)],
        compiler_params=_params("arbitrary", "arbitrary"),
        name="route_topk",
    )(logits_t)


def _row_copy(src_hbm, row, dst_vmem, slot, sem):
    return pltpu.make_async_copy(src_hbm.at[pl.ds(row, 1), :], dst_vmem.at[pl.ds(slot, 1), :], sem)


def _gather_rows(src_hbm, idx_ref, n_rows, dst_vmem, sem):
    for r in range(n_rows):
        _row_copy(src_hbm, idx_ref[0, 0, r], dst_vmem, r, sem).start()


def _wait_rows(src_hbm, n_rows, dst_vmem, sem):
    pltpu.make_async_copy(src_hbm.at[pl.ds(0, n_rows), :], dst_vmem, sem).wait()


def _dispatch_kernel(pad0_ref, padn_ref, nused_ref, pos_ref, h_ref, xs_hbm, zeros, sem, zsem,
                     *, tb, n_exp, n_blocks):
    @pl.when(pl.program_id(0) == 0)
    def _():
        zeros[...] = jnp.zeros_like(zeros)

        def each_fill(fn):
            def per_expert(e, carry):
                def per_row(r, inner):
                    fn(pltpu.make_async_copy(zeros.at[pl.ds(0, 1), :], xs_hbm.at[pl.ds(pad0_ref[e] + r, 1), :], zsem))
                    return inner

                lax.fori_loop(0, padn_ref[e], per_row, 0)
                return carry

            lax.fori_loop(0, n_exp, per_expert, 0)

            def per_block(b, carry):
                row0 = pl.multiple_of(b * MOE_BLOCK, MOE_BLOCK)
                fn(pltpu.make_async_copy(zeros, xs_hbm.at[pl.ds(row0, MOE_BLOCK), :], zsem))
                return carry

            lax.fori_loop(nused_ref[0], n_blocks, per_block, 0)

        each_fill(lambda cp: cp.start())
        each_fill(lambda cp: cp.wait())

    for r in range(tb):
        for k in range(TOP_K):
            pltpu.make_async_copy(h_ref.at[pl.ds(r, 1), :], xs_hbm.at[pl.ds(pos_ref[0, 0, k * tb + r], 1), :],
                                  sem).start()
    for k in range(TOP_K):
        pltpu.make_async_copy(h_ref, xs_hbm.at[pl.ds(0, tb), :], sem).wait()


def _dispatch(h2p, pos3, pad_start, pad_count, n_used, n_rows):
    t, dp = h2p.shape
    n_tiles = pos3.shape[0]
    tb = pos3.shape[2] // TOP_K
    return pl.pallas_call(
        functools.partial(_dispatch_kernel, tb=tb, n_exp=pad_start.shape[0], n_blocks=n_rows // MOE_BLOCK),
        out_shape=jax.ShapeDtypeStruct((n_rows, dp), h2p.dtype),
        grid_spec=pltpu.PrefetchScalarGridSpec(
            num_scalar_prefetch=3,
            grid=(n_tiles,),
            in_specs=[pl.BlockSpec((1, 1, TOP_K * tb), lambda i, p0, pn, nu: (i, 0, 0), memory_space=pltpu.SMEM),
                      pl.BlockSpec((tb, dp), lambda i, p0, pn, nu: (i, 0))],
            out_specs=pl.BlockSpec(memory_space=pl.ANY),
            scratch_shapes=[pltpu.VMEM((MOE_BLOCK, dp), h2p.dtype), pltpu.SemaphoreType.DMA(()),
                            pltpu.SemaphoreType.DMA(())]),
        compiler_params=_params("arbitrary"),
        name="moe_dispatch",
    )(pad_start, pad_count, n_used, pos3, h2p)


def _new_expert(be_ref, b):
    return (b == 0) | (be_ref[b] != be_ref[jnp.maximum(b - 1, 0)])


W_PIECES = 4


def _weight_copies(w_hbm, e, col0, stage, sem):
    rows = stage.shape[0] // W_PIECES
    return [pltpu.make_async_copy(w_hbm.at[e, pl.ds(i * rows, rows), pl.ds(col0, stage.shape[1])],
                                  stage.at[pl.ds(i * rows, rows), :], sem) for i in range(W_PIECES)]


def _convert_rows(src_f32, dst_bf16):
    rows = 16

    def body(i, carry):
        r0 = pl.multiple_of(i * rows, rows)
        dst_bf16[pl.ds(r0, rows), :] = src_f32[pl.ds(r0, rows), :].astype(BF16)
        return carry

    lax.fori_loop(0, src_f32.shape[0] // rows, body, 0, unroll=4)


def _gateup_kernel(be_ref, nxt_ref, last_ref, nused_ref, x_ref, w_hbm, bg_ref, bl_ref, o_ref,
                   stg_g, stg_l, wg_sc, wl_sc, sem, *, nf, tf):
    c = pl.program_id(0)
    b = pl.program_id(1)

    def slab(e, chunk):
        col = pl.multiple_of(chunk * tf, tf)
        return _weight_copies(w_hbm, e, col, stg_g, sem) + _weight_copies(w_hbm, e, col + nf * tf, stg_l, sem)

    @pl.when((c == 0) & (b == 0))
    def _():
        for cp in slab(be_ref[0], 0):
            cp.start()

    @pl.when(_new_expert(be_ref, b))
    def _():
        for cp in slab(0, 0):
            cp.wait()
        _convert_rows(stg_g, wg_sc)
        _convert_rows(stg_l, wl_sc)
        last_run = last_ref[b] == 1

        @pl.when(jnp.logical_not(last_run))
        def _():
            for cp in slab(nxt_ref[b], c):
                cp.start()

        @pl.when(last_run & (c + 1 < nf))
        def _():
            for cp in slab(be_ref[0], c + 1):
                cp.start()

    @pl.when(b < nused_ref[0])
    def _():
        x = jnp.concatenate(_unpack_halves(x_ref[...]), axis=1)
        glu = jnp.dot(x, wg_sc[...], preferred_element_type=F32) + bg_ref[0]
        lin = jnp.dot(x, wl_sc[...], preferred_element_type=F32) + bl_ref[0]
        glu = jnp.minimum(glu, SWIGLU_LIMIT)
        lin = jnp.clip(lin, -SWIGLU_LIMIT, SWIGLU_LIMIT)
        o_ref[...] = (glu * _sigmoid(SWIGLU_ALPHA * glu) * (lin + 1.0)).astype(BF16)

    @pl.when(b >= nused_ref[0])
    def _():
        o_ref[...] = jnp.zeros_like(o_ref)


def _gateup(runs, n_used, xs, w_gu, b_gu3):
    n_rows, dp = xs.shape
    d = 2 * dp
    n_blocks = n_rows // MOE_BLOCK
    d_ff = w_gu.shape[2] // 2
    tf = min(1024, d_ff)
    nf = d_ff // tf
    return pl.pallas_call(
        functools.partial(_gateup_kernel, nf=nf, tf=tf),
        out_shape=jax.ShapeDtypeStruct((n_rows, d_ff), BF16),
        grid_spec=pltpu.PrefetchScalarGridSpec(
            num_scalar_prefetch=4,
            grid=(nf, n_blocks),
            in_specs=[pl.BlockSpec((MOE_BLOCK, dp), lambda c, b, be, nx, la, nu: (jnp.minimum(b, nu[0] - 1), 0)),
                      pl.BlockSpec(memory_space=pl.ANY),
                      pl.BlockSpec((1, 1, tf), lambda c, b, be, nx, la, nu: (be[b], 0, c)),
                      pl.BlockSpec((1, 1, tf), lambda c, b, be, nx, la, nu: (be[b], 0, nf + c))],
            out_specs=pl.BlockSpec((MOE_BLOCK, tf), lambda c, b, be, nx, la, nu: (b, c)),
            scratch_shapes=[pltpu.VMEM((d, tf), F32), pltpu.VMEM((d, tf), F32),
                            pltpu.VMEM((d, tf), BF16), pltpu.VMEM((d, tf), BF16),
                            pltpu.SemaphoreType.DMA(())]),
        compiler_params=_params("arbitrary", "arbitrary"),
        name="moe_gateup",
    )(*runs, n_used, xs, w_gu, b_gu3, b_gu3)


def _down_kernel(be_ref, nxt_ref, last_ref, nused_ref, a_ref, w_hbm, b_ref, o_ref, stg, w_sc, sem):
    b = pl.program_id(0)

    @pl.when(b == 0)
    def _():
        for cp in _weight_copies(w_hbm, be_ref[0], 0, stg, sem):
            cp.start()

    @pl.when(_new_expert(be_ref, b))
    def _():
        for cp in _weight_copies(w_hbm, 0, 0, stg, sem):
            cp.wait()
        _convert_rows(stg, w_sc)

        @pl.when(last_ref[b] == 0)
        def _():
            for cp in _weight_copies(w_hbm, nxt_ref[b], 0, stg, sem):
                cp.start()

    @pl.when(b < nused_ref[0])
    def _():
        y = jnp.dot(a_ref[...], w_sc[...], preferred_element_type=F32) + b_ref[0]
        o_ref[...] = _pack_halves(y)

    @pl.when(b >= nused_ref[0])
    def _():
        o_ref[...] = jnp.zeros_like(o_ref)


def _down(runs, n_used, act, w_down, b_down3):
    n_rows, d_ff = act.shape
    n_blocks = n_rows // MOE_BLOCK
    d = w_down.shape[2]
    return pl.pallas_call(
        _down_kernel,
        out_shape=jax.ShapeDtypeStruct((n_rows, d // 2), jnp.uint32),
        grid_spec=pltpu.PrefetchScalarGridSpec(
            num_scalar_prefetch=4,
            grid=(n_blocks,),
            in_specs=[pl.BlockSpec((MOE_BLOCK, d_ff), lambda b, be, nx, la, nu: (b, 0)),
                      pl.BlockSpec(memory_space=pl.ANY),
                      pl.BlockSpec((1, 1, d), lambda b, be, nx, la, nu: (be[b], 0, 0))],
            out_specs=pl.BlockSpec((MOE_BLOCK, d // 2), lambda b, be, nx, la, nu: (b, 0)),
            scratch_shapes=[pltpu.VMEM((d_ff, d), F32), pltpu.VMEM((d_ff, d), BF16),
                            pltpu.SemaphoreType.DMA(())]),
        compiler_params=_params("arbitrary"),
        name="moe_down",
    )(*runs, n_used, act, w_down, b_down3)


def _combine_kernel(pos_ref, pos_next_ref, y_hbm, gate_ref, x1_ref, mod_ref, gfin_ref, o_ref, buf, sem,
                    *, tb, n_tiles, last_layer):
    i = pl.program_id(0)
    slot = i % 2

    @pl.when(i == 0)
    def _():
        _gather_rows(y_hbm, pos_ref, TOP_K * tb, buf.at[0], sem.at[0])

    @pl.when(i + 1 < n_tiles)
    def _():
        _gather_rows(y_hbm, pos_next_ref, TOP_K * tb, buf.at[1 - slot], sem.at[1 - slot])

    _wait_rows(y_hbm, TOP_K * tb, buf.at[slot], sem.at[slot])
    gates = gate_ref[...]
    moe_lo = moe_hi = None
    for k in range(TOP_K):
        lo, hi = _unpack_halves(buf[slot, pl.ds(k * tb, tb), :])
        g = gates[:, k:k + 1]
        moe_lo = g * lo.astype(F32) if k == 0 else moe_lo + g * lo.astype(F32)
        moe_hi = g * hi.astype(F32) if k == 0 else moe_hi + g * hi.astype(F32)
    moe = jnp.concatenate([moe_lo, moe_hi], axis=1)
    x2 = x1_ref[...] + mod_ref[0, 5:6, :] * moe
    o_ref[...] = _rms(x2) * gfin_ref[...] if last_layer else x2


def _combine(pos3, y_buf, gates, x1, mod3, g_final, seq, last_layer):
    t, d = x1.shape
    n_tiles = pos3.shape[0]
    tb = pos3.shape[2] // TOP_K
    idx_spec = lambda step: pl.BlockSpec((1, 1, tb * TOP_K), lambda i: (jnp.minimum(i + step, n_tiles - 1), 0, 0),
                                         memory_space=pltpu.SMEM)
    return pl.pallas_call(
        functools.partial(_combine_kernel, tb=tb, n_tiles=n_tiles, last_layer=last_layer),
        out_shape=jax.ShapeDtypeStruct((t, d), F32),
        grid=(n_tiles,),
        in_specs=[idx_spec(0), idx_spec(1),
                  pl.BlockSpec(memory_space=pl.ANY),
                  pl.BlockSpec((tb, TOP_K), lambda i: (i, 0)),
                  pl.BlockSpec((tb, d), lambda i: (i, 0)),
                  pl.BlockSpec((1, mod3.shape[1], d), lambda i: (i // (seq // tb), 0, 0)),
                  pl.BlockSpec((1, d), lambda i: (0, 0))],
        out_specs=pl.BlockSpec((tb, d), lambda i: (i, 0)),
        scratch_shapes=[pltpu.VMEM((2, TOP_K * tb, y_buf.shape[1]), y_buf.dtype), pltpu.SemaphoreType.DMA((2,))],
        compiler_params=_params("arbitrary"),
        name="moe_combine",
    )(pos3, pos3, y_buf, gates, x1, mod3, g_final)


def kernel(x, c, w_ada, b_ada, g_norm1, g_norm2, w_in, b_fgate, g_fox_out, lb_logits, g_hg_out,
           w_out, w_router, b_router, w_gu, b_gu, w_down, b_down, g_final):
    n_batch, seq, d = x.shape
    t = n_batch * seq
    depth = w_ada.shape[0]
    fox_heads = b_fgate.shape[1]
    fox_w = g_fox_out.shape[1]
    hg_w = g_hg_out.shape[1]
    hg_heads = hg_w // HEAD_DIM
    n_exp = w_router.shape[2]
    assert fox_w == fox_heads * HEAD_DIM and fox_heads <= LANES
    n_blocks = -(-t * TOP_K // MOE_BLOCK) + n_exp
    n_mod = w_ada.shape[2] // d

    x2d = x.reshape(t, d)
    for l in range(depth):
        mod3 = _ada_mod(c, w_ada[l], b_ada[l]).reshape(n_batch, n_mod, d)

        w_l = w_in[l].astype(BF16)
        w_main = jnp.concatenate([w_l[:, :3 * fox_w], w_l[:, 3 * fox_w + fox_heads:]], axis=1)
        w_fg = jnp.pad(w_l[:, 3 * fox_w:3 * fox_w + fox_heads], ((0, 0), (0, LANES - fox_heads)))
        b_fg = jnp.pad(b_fgate[l], (0, LANES - fox_heads)).reshape(1, LANES)
        proj, cum = _inproj(x2d, mod3, g_norm1[l].reshape(1, d), w_main, w_fg, b_fg, seq)
        proj3 = proj.reshape(n_batch, seq, -1)

        fox = _fox_attention(proj3, cum.reshape(n_batch, seq, LANES), fox_heads)

        lb3 = lb_logits.reshape(lb_logits.shape[0], hg_heads, HEAD_DIM).transpose(1, 0, 2)
        hg = _hgrn2(proj3, lb3, g_hg_out[l].reshape(hg_heads, 1, HEAD_DIM), hg_heads, 3 * fox_heads, l)

        x1, h2p, logits_t = _outproj(
            fox.reshape(t, fox_w), hg.reshape(t, hg_w), x2d, mod3, g_fox_out[l].reshape(1, fox_w),
            g_norm2[l].reshape(1, d), w_out[l].astype(BF16), w_router[l], b_router[l].reshape(n_exp, 1), seq)

        pos_t, gates_t, counts = _route(logits_t)

        cnt = counts[:, 0].astype(jnp.int32)
        padded = (cnt + MOE_BLOCK - 1) // MOE_BLOCK * MOE_BLOCK
        padded_end = jnp.cumsum(padded)
        block_row0 = jnp.arange(n_blocks, dtype=jnp.int32) * MOE_BLOCK
        block_e = jnp.sum((block_row0[:, None] >= padded_end[None, :]).astype(jnp.int32), axis=1)
        n_used = (padded_end[-1:] // MOE_BLOCK).astype(jnp.int32)
        block_id = jnp.arange(n_blocks, dtype=jnp.int32)
        block_e = jnp.minimum(block_e, jnp.sum(jnp.where(block_id == n_used - 1, block_e, 0)))
        change_at = jnp.where(block_e != jnp.roll(block_e, 1), block_id, n_blocks).at[0].set(n_blocks)
        next_change = jnp.flip(lax.cummin(jnp.flip(jnp.roll(change_at, -1).at[-1].set(n_blocks))))
        last_run = (next_change >= n_blocks).astype(jnp.int32)
        next_e = block_e[jnp.minimum(next_change, n_blocks - 1)]
        runs = (block_e, next_e, last_run)

        def tile_major(tb):
            return pos_t.reshape(TOP_K, t // tb, tb).transpose(1, 0, 2).reshape(t // tb, 1, TOP_K * tb)

        pad_count = padded - cnt
        xs = _dispatch(h2p, tile_major(min(256, seq)), padded_end - pad_count, pad_count, n_used,
                       n_blocks * MOE_BLOCK)
        act = _gateup(runs, n_used, xs, w_gu[l], b_gu[l].reshape(n_exp, 1, -1))
        y_buf = _down(runs, n_used, act, w_down[l], b_down[l].reshape(n_exp, 1, d))
        x2d = _combine(tile_major(min(128, seq)), y_buf, gates_t.T, x1, mod3, g_final.reshape(1, d), seq,
                       l == depth - 1)
    return x2d.reshape(n_batch, seq, d)
```

```python
import functools

import jax
import jax.numpy as jnp
from jax import lax
from jax.experimental import pallas as pl
from jax.experimental.pallas import tpu as pltpu

HEAD_DIM = 128
TOP_K = 4
MOE_BLOCK = 256
HG_CHUNK = 64
NORM_EPS = 1e-6
SWIGLU_ALPHA = 1.702
SWIGLU_LIMIT = 7.0
LANES = 128
VMEM_LIMIT = 56 * 1024 * 1024

F32 = jnp.float32
BF16 = jnp.bfloat16
HIGHEST = lax.Precision.HIGHEST
NT_DIMS = (((1,), (1,)), ((), ()))
TN_DIMS = (((0,), (0,)), ((), ()))


def _params(*semantics):
    return pltpu.CompilerParams(dimension_semantics=semantics, vmem_limit_bytes=VMEM_LIMIT)


def _sigmoid(z):
    return 1.0 / (1.0 + jnp.exp(-z))


def _rms(v):
    return v * lax.rsqrt(jnp.mean(v * v, axis=-1, keepdims=True) + NORM_EPS)


def _ada_kernel(ct_ref, w_ref, b_ref, o_ref, *, n_batch, d_model):
    ct = ct_ref[...]
    cond = ct * _sigmoid(ct)
    rows = 256
    for b in range(n_batch):
        acc = jnp.zeros((8, w_ref.shape[1]), F32)
        for i in range(d_model // rows):
            w3 = w_ref[pl.ds(i * rows, rows), :].reshape(rows // 8, 8, -1)
            c3 = cond[i * rows:(i + 1) * rows, b:b + 1].reshape(rows // 8, 8, 1)
            acc = acc + jnp.sum(w3 * c3, axis=0)
        o_ref[pl.ds(b, 1), :] = jnp.sum(acc, axis=0, keepdims=True) + b_ref[...]


def _ada_mod(c, w_ada, b_ada):
    n_batch, d = c.shape
    n = w_ada.shape[1]
    tn = 1024
    return pl.pallas_call(
        functools.partial(_ada_kernel, n_batch=n_batch, d_model=d),
        out_shape=jax.ShapeDtypeStruct((n_batch, n), F32),
        grid=(n // tn,),
        in_specs=[pl.BlockSpec((d, n_batch), lambda j: (0, 0)),
                  pl.BlockSpec((d, tn), lambda j: (0, j)),
                  pl.BlockSpec((1, tn), lambda j: (0, j))],
        out_specs=pl.BlockSpec((n_batch, tn), lambda j: (0, j)),
        compiler_params=_params("arbitrary"),
        name="ada_mod",
    )(c.T, w_ada, b_ada.reshape(1, n))


def _inproj_kernel(x_ref, mod_ref, g_ref, w_ref, wf_ref, bf_ref, proj_ref, cum_ref, h_sc, carry_sc,
                   *, tiles_per_batch, sub):
    i = pl.program_id(0)
    j = pl.program_id(1)

    @pl.when(j == 0)
    def _():
        x = x_ref[...]
        h = _rms(x) * (g_ref[...] * (1.0 + mod_ref[0, 1:2, :])) + mod_ref[0, 0:1, :]
        hb = h.astype(BF16)
        h_sc[...] = hb
        z = jnp.dot(hb, wf_ref[...], preferred_element_type=F32) + bf_ref[...]
        logf = jnp.minimum(z, 0.0) - jnp.log(1.0 + jnp.exp(-jnp.abs(z)))

        @pl.when(i % tiles_per_batch == 0)
        def _():
            carry_sc[...] = jnp.zeros_like(carry_sc)

        r = lax.broadcasted_iota(jnp.int32, (sub, sub), 0)
        c = lax.broadcasted_iota(jnp.int32, (sub, sub), 1)
        tri = (r >= c).astype(BF16)
        carry = carry_sc[...]
        for s in range(x.shape[0] // sub):
            blk = logf[s * sub:(s + 1) * sub, :]
            hi = blk.astype(BF16)
            rest = blk - hi.astype(F32)
            mid = rest.astype(BF16)
            low = (rest - mid.astype(F32)).astype(BF16)
            two = jnp.dot(tri, jnp.concatenate([hi, mid], axis=1), preferred_element_type=F32)
            cs = two[:, :LANES] + two[:, LANES:] + jnp.dot(tri, low, preferred_element_type=F32) + carry
            cum_ref[pl.ds(s * sub, sub), :] = cs
            carry = cs[sub - 1:sub, :]
        carry_sc[...] = carry

    proj_ref[...] = jnp.dot(h_sc[...], w_ref[...], preferred_element_type=F32).astype(BF16)


def _inproj(x2d, mod3, g1, w_main, w_fg, b_fg, seq):
    t, d = x2d.shape
    n = w_main.shape[1]
    tm = min(1024, seq)
    tn = next(c for c in (1024, 512, 256, 128) if n % c == 0)
    return pl.pallas_call(
        functools.partial(_inproj_kernel, tiles_per_batch=seq // tm, sub=256),
        out_shape=(jax.ShapeDtypeStruct((t, n), BF16), jax.ShapeDtypeStruct((t, LANES), F32)),
        grid=(t // tm, n // tn),
        in_specs=[pl.BlockSpec((tm, d), lambda i, j: (i, 0)),
                  pl.BlockSpec((1, mod3.shape[1], d), lambda i, j: (i // (seq // tm), 0, 0)),
                  pl.BlockSpec((1, d), lambda i, j: (0, 0)),
                  pl.BlockSpec((d, tn), lambda i, j: (0, j)),
                  pl.BlockSpec((d, LANES), lambda i, j: (0, 0)),
                  pl.BlockSpec((1, LANES), lambda i, j: (0, 0))],
        out_specs=(pl.BlockSpec((tm, tn), lambda i, j: (i, j)),
                   pl.BlockSpec((tm, LANES), lambda i, j: (i, 0))),
        scratch_shapes=[pltpu.VMEM((tm, d), BF16), pltpu.VMEM((1, LANES), F32)],
        compiler_params=_params("arbitrary", "arbitrary"),
        name="inproj",
    )(x2d, mod3, g1, w_main, w_fg, b_fg)


N_BIAS = 3
LOG2E = 1.4426950408889634


def _attn_kernel(q_ref, k_ref, v_ref, cum_ref, o_ref, kaug_sc, vt_sc, qt_sc, sa_sc, sb_sc, m_sc, l_sc, acc_sc,
                 *, blk, n_kv):
    h = pl.program_id(1)
    i = pl.program_id(2)

    @pl.when(i == 0)
    def _():
        lane = lax.broadcasted_iota(jnp.int32, (blk, LANES), 1)

        def prep(j, carry):
            start = pl.multiple_of(j * blk, blk)
            cum = cum_ref[0, pl.ds(start, blk), :]
            rest = jnp.sum(jnp.where(lane == h, cum, 0.0), axis=1, keepdims=True) * LOG2E
            bias = jnp.zeros((blk, LANES), F32)
            for piece in range(N_BIAS):
                part = rest.astype(BF16).astype(F32)
                bias = jnp.where(lane == piece, part, bias)
                rest = rest - part
            kaug_sc[j, :, 0:HEAD_DIM] = k_ref[0, pl.ds(start, blk), :]
            kaug_sc[j, :, HEAD_DIM:2 * HEAD_DIM] = bias.astype(BF16)
            vt_sc[j] = v_ref[0, pl.ds(start, blk), :].astype(F32).T.astype(BF16)
            return carry

        lax.fori_loop(0, n_kv, prep, 0)

    qs = q_ref[0].astype(F32) * (HEAD_DIM ** -0.5 * LOG2E)
    qt_sc[0:HEAD_DIM, :] = qs.T.astype(BF16)
    row = lax.broadcasted_iota(jnp.int32, (HEAD_DIM, blk), 0)
    qt_sc[HEAD_DIM:2 * HEAD_DIM, :] = jnp.where(row < N_BIAS, -1.0, 0.0).astype(BF16)
    m_sc[...] = jnp.full_like(m_sc, -jnp.inf)
    l_sc[...] = jnp.zeros_like(l_sc)
    acc_sc[...] = jnp.zeros_like(acc_sc)

    def scores(j):
        return jnp.dot(kaug_sc[j], qt_sc[...], preferred_element_type=F32)

    def update(j, s):
        m_prev = m_sc[...]
        m_new = jnp.maximum(m_prev, jnp.max(s, axis=0, keepdims=True))
        alpha = jnp.exp2(m_prev - m_new)
        p = jnp.exp2(s - m_new)
        l_sc[...] = alpha * l_sc[...] + jnp.sum(p, axis=0, keepdims=True)
        acc_sc[...] = alpha * acc_sc[...] + jnp.dot(vt_sc[j], p.astype(BF16), preferred_element_type=F32)
        m_sc[...] = m_new

    def causal(s):
        key = lax.broadcasted_iota(jnp.int32, (blk, blk), 0)
        qry = lax.broadcasted_iota(jnp.int32, (blk, blk), 1)
        return jnp.where(key <= qry, s, -jnp.inf)

    sa_sc[...] = scores(0)

    def pair_at(j):
        sb_sc[...] = scores(j + 1)
        update(j, sa_sc[...])
        sa_sc[...] = scores(j + 2)
        update(j + 1, sb_sc[...])

    def quad(jj, carry):
        pair_at(4 * jj)
        pair_at(4 * jj + 2)
        return carry

    def pair(jj, carry):
        pair_at(4 * (i // 4) + 2 * jj)
        return carry

    lax.fori_loop(0, i // 4, quad, 0)
    lax.fori_loop(0, (i % 4) // 2, pair, 0)

    @pl.when(i % 2 == 0)
    def _():
        update(i, causal(sa_sc[...]))

    @pl.when(i % 2 == 1)
    def _():
        sb_sc[...] = scores(i)
        update(i - 1, sa_sc[...])
        update(i, causal(sb_sc[...]))

    o_ref[0] = (acc_sc[...] / l_sc[...]).T.astype(BF16)


def _fox_attention(proj3, cum3, n_heads):
    n_batch, seq, _ = proj3.shape
    blk = min(512, seq)
    n_kv = seq // blk
    return pl.pallas_call(
        functools.partial(_attn_kernel, blk=blk, n_kv=n_kv),
        out_shape=jax.ShapeDtypeStruct((n_batch, seq, n_heads * HEAD_DIM), BF16),
        grid=(n_batch, n_heads, n_kv),
        in_specs=[pl.BlockSpec((1, blk, HEAD_DIM), lambda b, h, i: (b, i, h)),
                  pl.BlockSpec((1, seq, HEAD_DIM), lambda b, h, i: (b, 0, n_heads + h)),
                  pl.BlockSpec((1, seq, HEAD_DIM), lambda b, h, i: (b, 0, 2 * n_heads + h)),
                  pl.BlockSpec((1, seq, LANES), lambda b, h, i: (b, 0, 0))],
        out_specs=pl.BlockSpec((1, blk, HEAD_DIM), lambda b, h, i: (b, i, h)),
        scratch_shapes=[pltpu.VMEM((n_kv, blk, 2 * HEAD_DIM), BF16),
                        pltpu.VMEM((n_kv, HEAD_DIM, blk), BF16),
                        pltpu.VMEM((2 * HEAD_DIM, blk), BF16),
                        pltpu.VMEM((blk, blk), F32), pltpu.VMEM((blk, blk), F32),
                        pltpu.VMEM((1, blk), F32), pltpu.VMEM((1, blk), F32),
                        pltpu.VMEM((HEAD_DIM, blk), F32)],
        compiler_params=_params("arbitrary", "arbitrary", "arbitrary"),
        name="fox_attention",
    )(proj3, proj3, proj3, cum3)


def _hgrn_kernel(q_ref, f_ref, i_ref, g_ref, lbl_ref, gn_ref, o_ref, st_sc, *, rows, sub, layer):
    @pl.when(pl.program_id(2) == 0)
    def _():
        st_sc[...] = jnp.zeros_like(st_sc)

    lbl = lbl_ref[0]
    e = jnp.exp(lbl - jnp.max(lbl, axis=0, keepdims=True))
    lb = jnp.sum(e[0:layer + 1, :], axis=0, keepdims=True) / jnp.sum(e, axis=0, keepdims=True)
    f = lb + (1.0 - lb) * _sigmoid(f_ref[0].astype(F32))
    logf = jnp.log(f)
    kk = 1.0 - f
    qf = q_ref[0].astype(F32)
    qq = qf * _sigmoid(qf)

    n_ch = sub // HG_CHUNK
    r = lax.broadcasted_iota(jnp.int32, (sub, sub), 0)
    c = lax.broadcasted_iota(jnp.int32, (sub, sub), 1)
    within = (r >= c) & (r // HG_CHUNK == c // HG_CHUNK)
    tri = within.astype(BF16)
    rw = lax.broadcasted_iota(jnp.int32, (sub, n_ch * HEAD_DIM), 0)
    cw = lax.broadcasted_iota(jnp.int32, (sub, n_ch * HEAD_DIM), 1)
    own_block = rw // HG_CHUNK == cw // HEAD_DIM

    zero = jnp.zeros((), BF16)
    n_sub = rows // sub
    b_all, b_last_all = [], []
    for s in range(n_sub):
        lf = logf[s * sub:(s + 1) * sub, :]
        hi = lf.astype(BF16)
        rest = lf - hi.astype(F32)
        mid = rest.astype(BF16)
        low = (rest - mid.astype(F32)).astype(BF16)
        two = jnp.dot(tri, jnp.concatenate([hi, mid], axis=1), preferred_element_type=F32)
        b = two[:, :HEAD_DIM] + two[:, HEAD_DIM:] + jnp.dot(tri, low, preferred_element_type=F32)
        b_all.append(b)
        b_last_all.append([b[(n + 1) * HG_CHUNK - 1:(n + 1) * HG_CHUNK, :] for n in range(n_ch)])

    q_dec_all, o_all, upd_all = [], [], []
    for s in range(n_sub):
        rs = slice(s * sub, (s + 1) * sub)
        b = b_all[s]
        b_last_rows = jnp.concatenate([jnp.broadcast_to(bl, (HG_CHUNK, HEAD_DIM)) for bl in b_last_all[s]], axis=0)
        q_dec = (qq[rs, :] * jnp.exp(b)).astype(BF16)
        k_inv = (kk[rs, :] * jnp.exp(-b)).astype(BF16)
        k_tail = (kk[rs, :] * jnp.exp(b_last_rows - b)).astype(BF16)
        vv = i_ref[0, pl.ds(s * sub, sub), :]
        attn = lax.dot_general(q_dec, k_inv, NT_DIMS, preferred_element_type=F32)
        o_all.append(jnp.dot(jnp.where(within, attn, 0.0).astype(BF16), vv, preferred_element_type=F32))
        k_blocks = jnp.where(own_block, jnp.concatenate([k_tail] * n_ch, axis=1), zero)
        upd_all.append(lax.dot_general(vv, k_blocks, TN_DIMS, preferred_element_type=F32))
        q_dec_all.append(q_dec)

    st = st_sc[...]
    for s in range(n_sub):
        states = []
        for n in range(n_ch):
            states.append(st.astype(BF16))
            st = jnp.exp(b_last_all[s][n]) * st + upd_all[s][:, n * HEAD_DIM:(n + 1) * HEAD_DIM]
        q_blocks = jnp.where(own_block, jnp.concatenate([q_dec_all[s]] * n_ch, axis=1), zero)
        o = o_all[s] + lax.dot_general(q_blocks, jnp.concatenate(states, axis=1), NT_DIMS,
                                       preferred_element_type=F32)
        y = _rms(o) * gn_ref[0]
        gf = g_ref[0, pl.ds(s * sub, sub), :].astype(F32)
        o_ref[0, pl.ds(s * sub, sub), :] = (y * (gf * _sigmoid(gf))).astype(BF16)
    st_sc[...] = st


def _hgrn2(proj3, lb_logits3, g_hg3, n_heads, col0, layer):
    n_batch, seq, _ = proj3.shape
    rows = min(1024, seq)
    sub = min(256, rows)
    spec = lambda off: pl.BlockSpec((1, rows, HEAD_DIM), lambda b, h, r: (b, r, col0 + off * n_heads + h))
    return pl.pallas_call(
        functools.partial(_hgrn_kernel, rows=rows, sub=sub, layer=layer),
        out_shape=jax.ShapeDtypeStruct((n_batch, seq, n_heads * HEAD_DIM), BF16),
        grid=(n_batch, n_heads, seq // rows),
        in_specs=[spec(0), spec(1), spec(2), spec(3),
                  pl.BlockSpec((1, lb_logits3.shape[1], HEAD_DIM), lambda b, h, r: (h, 0, 0)),
                  pl.BlockSpec((1, 1, HEAD_DIM), lambda b, h, r: (h, 0, 0))],
        out_specs=pl.BlockSpec((1, rows, HEAD_DIM), lambda b, h, r: (b, r, h)),
        scratch_shapes=[pltpu.VMEM((HEAD_DIM, HEAD_DIM), F32)],
        compiler_params=_params("arbitrary", "arbitrary", "arbitrary"),
        name="hgrn2",
    )(proj3, proj3, proj3, proj3, lb_logits3, g_hg3)


def _pack_halves(v):
    n = v.shape[1] // 2
    lo = lax.bitcast_convert_type(v[:, :n].astype(BF16).astype(F32), jnp.uint32) >> 16
    hi = lax.bitcast_convert_type(v[:, n:].astype(BF16).astype(F32), jnp.uint32) & jnp.uint32(0xFFFF0000)
    return lo | hi


def _unpack_halves(p):
    lo = lax.bitcast_convert_type(p << 16, F32).astype(BF16)
    hi = lax.bitcast_convert_type(p & jnp.uint32(0xFFFF0000), F32).astype(BF16)
    return lo, hi


def _outproj_kernel(fox_ref, hg_ref, x_ref, mod_ref, gfox_ref, g2_ref, w_ref, wr_ref, br_ref,
                    x1_ref, h2_ref, lg_ref, *, n_exp, group):
    groups = [pl.ds(k * group, group) for k in range(x_ref.shape[0] // group)]
    mixed = []
    for rows in groups:
        fox = _rms(fox_ref[rows, :].astype(F32)) * gfox_ref[...]
        mixed.append(jnp.concatenate([fox.astype(BF16), hg_ref[rows, :]], axis=1))
    mix = [jnp.dot(m, w_ref[...], preferred_element_type=F32) for m in mixed]
    h2s = []
    for rows, m in zip(groups, mix):
        x1 = x_ref[rows, :] + mod_ref[0, 2:3, :] * m
        x1_ref[rows, :] = x1
        h2 = _rms(x1) * (g2_ref[...] * (1.0 + mod_ref[0, 4:5, :])) + mod_ref[0, 3:4, :]
        h2_ref[rows, :] = _pack_halves(h2)
        h2s.append(h2)
    for rows, h2 in zip(groups, h2s):
        h_hi = h2.astype(BF16)
        h_lo = (h2 - h_hi.astype(F32)).astype(BF16)
        part = jnp.dot(h_hi, wr_ref[...], preferred_element_type=F32)
        part = part + jnp.dot(h_lo, wr_ref[...], preferred_element_type=F32)
        logits = part + pltpu.roll(part, LANES - n_exp, axis=1)
        lg_ref[:, rows] = logits.T[0:n_exp, :] + br_ref[...]


def _outproj(fox2d, hg2d, x2d, mod3, g_fox, g2, w_out, w_router, b_router, seq):
    t, d = x2d.shape
    fox_w = fox2d.shape[1]
    n_exp = w_router.shape[1]
    assert 2 * n_exp <= LANES
    wr_hi = w_router.astype(BF16)
    wr_lo = (w_router - wr_hi.astype(F32)).astype(BF16)
    wr_cat = jnp.pad(jnp.concatenate([wr_hi, wr_lo], axis=1), ((0, 0), (0, LANES - 2 * n_exp)))
    tm = 512
    row = lambda i: (i, 0)
    const = lambda i: (0, 0)
    return pl.pallas_call(
        functools.partial(_outproj_kernel, n_exp=n_exp, group=256),
        out_shape=(jax.ShapeDtypeStruct((t, d), F32), jax.ShapeDtypeStruct((t, d // 2), jnp.uint32),
                   jax.ShapeDtypeStruct((n_exp, t), F32)),
        grid=(t // tm,),
        in_specs=[pl.BlockSpec((tm, fox_w), row),
                  pl.BlockSpec((tm, hg2d.shape[1]), row),
                  pl.BlockSpec((tm, d), row),
                  pl.BlockSpec((1, mod3.shape[1], d), lambda i: (i // (seq // tm), 0, 0)),
                  pl.BlockSpec((1, fox_w), const),
                  pl.BlockSpec((1, d), const),
                  pl.BlockSpec(w_out.shape, const),
                  pl.BlockSpec((d, LANES), const),
                  pl.BlockSpec((n_exp, 1), const)],
        out_specs=(pl.BlockSpec((tm, d), row), pl.BlockSpec((tm, d // 2), row),
                   pl.BlockSpec((n_exp, tm), lambda i: (0, i))),
        compiler_params=_params("arbitrary"),
        name="outproj_router",
    )(fox2d, hg2d, x2d, mod3, g_fox, g2, w_out, wr_cat, b_router)


def _route_kernel(lg_ref, pos_ref, gate_ref, cnt_ref, cnt_sc, run_sc, *, n_exp, tb):
    phase = pl.program_id(0)
    i = pl.program_id(1)

    @pl.when((phase == 0) & (i == 0))
    def _():
        cnt_sc[...] = jnp.zeros_like(cnt_sc)
        run_sc[...] = jnp.zeros_like(run_sc)

    logits = lg_ref[...]
    eidx = lax.broadcasted_iota(jnp.int32, (n_exp, tb), 0).astype(F32)
    work = logits
    vals, hots = [], []
    for _ in range(TOP_K):
        m = jnp.max(work, axis=0, keepdims=True)
        first = jnp.min(jnp.where(work == m, eidx, float(n_exp)), axis=0, keepdims=True)
        hot = eidx == first
        vals.append(m)
        hots.append(hot)
        work = jnp.where(hot, -jnp.inf, work)
    sel = hots[0] | hots[1] | hots[2] | hots[3]
    self32 = sel.astype(F32)

    @pl.when(phase == 0)
    def _():
        cnt_sc[...] += jnp.sum(self32, axis=1, keepdims=True)

    @pl.when(phase == 1)
    def _():
        cnt = cnt_sc[...]
        padded = jnp.ceil(cnt / MOE_BLOCK) * MOE_BLOCK
        r = lax.broadcasted_iota(jnp.int32, (n_exp, n_exp), 0)
        c = lax.broadcasted_iota(jnp.int32, (n_exp, n_exp), 1)
        strict = (c < r).astype(F32)
        pstart = jnp.dot(strict, jnp.broadcast_to(padded, (n_exp, LANES)), preferred_element_type=F32,
                         precision=HIGHEST)[:, 0:1]
        tr = lax.broadcasted_iota(jnp.int32, (tb, tb), 0)
        tc = lax.broadcasted_iota(jnp.int32, (tb, tb), 1)
        upper = (tr < tc).astype(BF16)
        rank = jnp.dot(sel.astype(BF16), upper, preferred_element_type=F32)
        base = pstart + run_sc[...] + rank
        exps = [jnp.exp(v - vals[0]) for v in vals]
        denom = exps[0] + exps[1] + exps[2] + exps[3]
        for k in range(TOP_K):
            pos_k = jnp.sum(jnp.where(hots[k], base, 0.0), axis=0, keepdims=True)
            pos_ref[pl.ds(k, 1), :] = pos_k.astype(jnp.int32)
            gate_ref[pl.ds(k, 1), :] = exps[k] / denom
        run_sc[...] += jnp.sum(self32, axis=1, keepdims=True)
        cnt_ref[...] = cnt


def _route(logits_t):
    n_exp, t = logits_t.shape
    tb = min(512, t)
    return pl.pallas_call(
        functools.partial(_route_kernel, n_exp=n_exp, tb=tb),
        out_shape=(jax.ShapeDtypeStruct((TOP_K, t), jnp.int32), jax.ShapeDtypeStruct((TOP_K, t), F32),
                   jax.ShapeDtypeStruct((n_exp, 1), F32)),
        grid=(2, t // tb),
        in_specs=[pl.BlockSpec((n_exp, tb), lambda p, i: (0, i))],
        out_specs=(pl.BlockSpec((TOP_K, tb), lambda p, i: (0, i * p)),
                   pl.BlockSpec((TOP_K, tb), lambda p, i: (0, i * p)),
                   pl.BlockSpec((n_exp, 1), lambda p, i: (0, 0))),
        scratch_shapes=[pltpu.VMEM((n_exp, 1), F32), pltpu.VMEM((n_exp, 1), F32)],
        compiler_params=_params("arbitrary", "arbitrary"),
        name="route_topk",
    )(logits_t)


def _row_copy(src_hbm, row, dst_vmem, slot, sem):
    return pltpu.make_async_copy(src_hbm.at[pl.ds(row, 1), :], dst_vmem.at[pl.ds(slot, 1), :], sem)


def _gather_rows(src_hbm, idx_ref, idx0, n_rows, dst_vmem, sem):
    for r in range(n_rows):
        _row_copy(src_hbm, idx_ref[0, 0, idx0 + r], dst_vmem, r, sem).start()


def _wait_rows(src_hbm, n_rows, dst_vmem, sem):
    pltpu.make_async_copy(src_hbm.at[pl.ds(0, n_rows), :], dst_vmem, sem).wait()


def _dispatch_kernel(pad0_ref, padn_ref, nused_ref, pos_ref, h_ref, xs_hbm, zeros, sem, zsem,
                     *, tb, n_exp, n_blocks, n_tiles):
    def each_fill(fn):
        def per_expert(e, carry):
            def per_row(r, inner):
                fn(pltpu.make_async_copy(zeros.at[pl.ds(0, 1), :], xs_hbm.at[pl.ds(pad0_ref[e] + r, 1), :], zsem))
                return inner

            lax.fori_loop(0, padn_ref[e], per_row, 0)
            return carry

        lax.fori_loop(0, n_exp, per_expert, 0)

        def per_block(b, carry):
            row0 = pl.multiple_of(b * MOE_BLOCK, MOE_BLOCK)
            fn(pltpu.make_async_copy(zeros, xs_hbm.at[pl.ds(row0, MOE_BLOCK), :], zsem))
            return carry

        lax.fori_loop(nused_ref[0], n_blocks, per_block, 0)

    @pl.when(pl.program_id(0) == 0)
    def _():
        zeros[...] = jnp.zeros_like(zeros)
        each_fill(lambda cp: cp.start())

    @pl.when(pl.program_id(0) == n_tiles - 1)
    def _():
        each_fill(lambda cp: cp.wait())

    for r in range(tb):
        for k in range(TOP_K):
            pltpu.make_async_copy(h_ref.at[pl.ds(r, 1), :], xs_hbm.at[pl.ds(pos_ref[0, 0, k * tb + r], 1), :],
                                  sem).start()
    for k in range(TOP_K):
        pltpu.make_async_copy(h_ref, xs_hbm.at[pl.ds(0, tb), :], sem).wait()


def _dispatch(h2p, pos3, pad_start, pad_count, n_used, n_rows):
    t, dp = h2p.shape
    n_tiles = pos3.shape[0]
    tb = pos3.shape[2] // TOP_K
    return pl.pallas_call(
        functools.partial(_dispatch_kernel, tb=tb, n_exp=pad_start.shape[0], n_blocks=n_rows // MOE_BLOCK,
                          n_tiles=n_tiles),
        out_shape=jax.ShapeDtypeStruct((n_rows, dp), h2p.dtype),
        grid_spec=pltpu.PrefetchScalarGridSpec(
            num_scalar_prefetch=3,
            grid=(n_tiles,),
            in_specs=[pl.BlockSpec((1, 1, TOP_K * tb), lambda i, p0, pn, nu: (i, 0, 0), memory_space=pltpu.SMEM),
                      pl.BlockSpec((tb, dp), lambda i, p0, pn, nu: (i, 0))],
            out_specs=pl.BlockSpec(memory_space=pl.ANY),
            scratch_shapes=[pltpu.VMEM((MOE_BLOCK, dp), h2p.dtype), pltpu.SemaphoreType.DMA(()),
                            pltpu.SemaphoreType.DMA(())]),
        compiler_params=_params("arbitrary"),
        name="moe_dispatch",
    )(pad_start, pad_count, n_used, pos3, h2p)


def _new_expert(be_ref, b):
    return (b == 0) | (be_ref[b] != be_ref[jnp.maximum(b - 1, 0)])


W_PIECES = 4


def _weight_copies(w_hbm, e, col0, stage, sem):
    rows = stage.shape[0] // W_PIECES
    return [pltpu.make_async_copy(w_hbm.at[e, pl.ds(i * rows, rows), pl.ds(col0, stage.shape[1])],
                                  stage.at[pl.ds(i * rows, rows), :], sem) for i in range(W_PIECES)]


def _convert_rows(src_f32, dst_bf16):
    rows = 16

    def body(i, carry):
        r0 = pl.multiple_of(i * rows, rows)
        dst_bf16[pl.ds(r0, rows), :] = src_f32[pl.ds(r0, rows), :].astype(BF16)
        return carry

    lax.fori_loop(0, src_f32.shape[0] // rows, body, 0, unroll=4)


def _gateup_kernel(be_ref, nxt_ref, last_ref, nused_ref, x_ref, w_hbm, bg_ref, bl_ref, o_ref,
                   stg_g, stg_l, wg_sc, wl_sc, sem, *, nf, tf):
    c = pl.program_id(0)
    b = pl.program_id(1)

    def slab(e, chunk):
        col = pl.multiple_of(chunk * tf, tf)
        return _weight_copies(w_hbm, e, col, stg_g, sem) + _weight_copies(w_hbm, e, col + nf * tf, stg_l, sem)

    @pl.when((c == 0) & (b == 0))
    def _():
        for cp in slab(be_ref[0], 0):
            cp.start()

    @pl.when(_new_expert(be_ref, b))
    def _():
        for cp in slab(0, 0):
            cp.wait()
        _convert_rows(stg_g, wg_sc)
        _convert_rows(stg_l, wl_sc)
        last_run = last_ref[b] == 1

        @pl.when(jnp.logical_not(last_run))
        def _():
            for cp in slab(nxt_ref[b], c):
                cp.start()

        @pl.when(last_run & (c + 1 < nf))
        def _():
            for cp in slab(be_ref[0], c + 1):
                cp.start()

    @pl.when(b < nused_ref[0])
    def _():
        x = jnp.concatenate(_unpack_halves(x_ref[...]), axis=1)
        glu = jnp.dot(x, wg_sc[...], preferred_element_type=F32) + bg_ref[0]
        lin = jnp.dot(x, wl_sc[...], preferred_element_type=F32) + bl_ref[0]
        glu = jnp.minimum(glu, SWIGLU_LIMIT)
        lin = jnp.clip(lin, -SWIGLU_LIMIT, SWIGLU_LIMIT)
        o_ref[...] = (glu * _sigmoid(SWIGLU_ALPHA * glu) * (lin + 1.0)).astype(BF16)

    @pl.when(b >= nused_ref[0])
    def _():
        o_ref[...] = jnp.zeros_like(o_ref)


def _gateup(runs, n_used, xs, w_gu, b_gu3):
    n_rows, dp = xs.shape
    d = 2 * dp
    n_blocks = n_rows // MOE_BLOCK
    d_ff = w_gu.shape[2] // 2
    tf = min(1024, d_ff)
    nf = d_ff // tf
    return pl.pallas_call(
        functools.partial(_gateup_kernel, nf=nf, tf=tf),
        out_shape=jax.ShapeDtypeStruct((n_rows, d_ff), BF16),
        grid_spec=pltpu.PrefetchScalarGridSpec(
            num_scalar_prefetch=4,
            grid=(nf, n_blocks),
            in_specs=[pl.BlockSpec((MOE_BLOCK, dp), lambda c, b, be, nx, la, nu: (jnp.minimum(b, nu[0] - 1), 0)),
                      pl.BlockSpec(memory_space=pl.ANY),
                      pl.BlockSpec((1, 1, tf), lambda c, b, be, nx, la, nu: (be[b], 0, c)),
                      pl.BlockSpec((1, 1, tf), lambda c, b, be, nx, la, nu: (be[b], 0, nf + c))],
            out_specs=pl.BlockSpec((MOE_BLOCK, tf), lambda c, b, be, nx, la, nu: (b, c)),
            scratch_shapes=[pltpu.VMEM((d, tf), F32), pltpu.VMEM((d, tf), F32),
                            pltpu.VMEM((d, tf), BF16), pltpu.VMEM((d, tf), BF16),
                            pltpu.SemaphoreType.DMA(())]),
        compiler_params=_params("arbitrary", "arbitrary"),
        name="moe_gateup",
    )(*runs, n_used, xs, w_gu, b_gu3, b_gu3)


def _down_kernel(be_ref, nxt_ref, last_ref, nused_ref, a_ref, w_hbm, b_ref, o_ref, stg, w_sc, sem):
    b = pl.program_id(0)

    @pl.when(b == 0)
    def _():
        for cp in _weight_copies(w_hbm, be_ref[0], 0, stg, sem):
            cp.start()

    @pl.when(_new_expert(be_ref, b))
    def _():
        for cp in _weight_copies(w_hbm, 0, 0, stg, sem):
            cp.wait()
        _convert_rows(stg, w_sc)

        @pl.when(last_ref[b] == 0)
        def _():
            for cp in _weight_copies(w_hbm, nxt_ref[b], 0, stg, sem):
                cp.start()

    @pl.when(b < nused_ref[0])
    def _():
        y = jnp.dot(a_ref[...], w_sc[...], preferred_element_type=F32) + b_ref[0]
        o_ref[...] = _pack_halves(y)

    @pl.when(b >= nused_ref[0])
    def _():
        o_ref[...] = jnp.zeros_like(o_ref)


def _down(runs, n_used, act, w_down, b_down3):
    n_rows, d_ff = act.shape
    n_blocks = n_rows // MOE_BLOCK
    d = w_down.shape[2]
    return pl.pallas_call(
        _down_kernel,
        out_shape=jax.ShapeDtypeStruct((n_rows, d // 2), jnp.uint32),
        grid_spec=pltpu.PrefetchScalarGridSpec(
            num_scalar_prefetch=4,
            grid=(n_blocks,),
            in_specs=[pl.BlockSpec((MOE_BLOCK, d_ff), lambda b, be, nx, la, nu: (b, 0)),
                      pl.BlockSpec(memory_space=pl.ANY),
                      pl.BlockSpec((1, 1, d), lambda b, be, nx, la, nu: (be[b], 0, 0))],
            out_specs=pl.BlockSpec((MOE_BLOCK, d // 2), lambda b, be, nx, la, nu: (b, 0)),
            scratch_shapes=[pltpu.VMEM((d_ff, d), F32), pltpu.VMEM((d_ff, d), BF16),
                            pltpu.SemaphoreType.DMA(())]),
        compiler_params=_params("arbitrary"),
        name="moe_down",
    )(*runs, n_used, act, w_down, b_down3)


def _combine_kernel(pos_ref, pos_next_ref, y_hbm, gate_ref, x1_ref, mod_ref, gfin_ref, o_ref, buf_a, buf_b, sem,
                    *, tb, n_steps, last_layer):
    i = pl.program_id(0)
    n_idx = TOP_K * tb

    def tile(buf, half):
        rows = pl.ds(half * tb, tb)
        gates = gate_ref[rows, :]
        moe_lo = moe_hi = None
        for k in range(TOP_K):
            lo, hi = _unpack_halves(buf[pl.ds(k * tb, tb), :])
            g = gates[:, k:k + 1]
            moe_lo = g * lo.astype(F32) if k == 0 else moe_lo + g * lo.astype(F32)
            moe_hi = g * hi.astype(F32) if k == 0 else moe_hi + g * hi.astype(F32)
        moe = jnp.concatenate([moe_lo, moe_hi], axis=1)
        x2 = x1_ref[rows, :] + mod_ref[0, 5:6, :] * moe
        o_ref[rows, :] = _rms(x2) * gfin_ref[...] if last_layer else x2

    @pl.when(i == 0)
    def _():
        _gather_rows(y_hbm, pos_ref, 0, n_idx, buf_a, sem.at[0])

    _wait_rows(y_hbm, n_idx, buf_a, sem.at[0])
    _gather_rows(y_hbm, pos_ref, n_idx, n_idx, buf_b, sem.at[1])
    tile(buf_a, 0)
    _wait_rows(y_hbm, n_idx, buf_b, sem.at[1])
    _gather_rows(y_hbm, pos_next_ref, 0, n_idx, buf_a, sem.at[0])
    tile(buf_b, 1)

    @pl.when(i == n_steps - 1)
    def _():
        _wait_rows(y_hbm, n_idx, buf_a, sem.at[0])


def _combine(pos3, y_buf, gates, x1, mod3, g_final, seq, last_layer):
    t, d = x1.shape
    n_steps = pos3.shape[0]
    tb = pos3.shape[2] // (2 * TOP_K)
    idx_spec = lambda step: pl.BlockSpec((1, 1, 2 * TOP_K * tb), lambda i: (jnp.minimum(i + step, n_steps - 1), 0, 0),
                                         memory_space=pltpu.SMEM)
    buf = pltpu.VMEM((TOP_K * tb, y_buf.shape[1]), y_buf.dtype)
    return pl.pallas_call(
        functools.partial(_combine_kernel, tb=tb, n_steps=n_steps, last_layer=last_layer),
        out_shape=jax.ShapeDtypeStruct((t, d), F32),
        grid=(n_steps,),
        in_specs=[idx_spec(0), idx_spec(1),
                  pl.BlockSpec(memory_space=pl.ANY),
                  pl.BlockSpec((2 * tb, TOP_K), lambda i: (i, 0)),
                  pl.BlockSpec((2 * tb, d), lambda i: (i, 0)),
                  pl.BlockSpec((1, mod3.shape[1], d), lambda i: (i // (seq // (2 * tb)), 0, 0)),
                  pl.BlockSpec((1, d), lambda i: (0, 0))],
        out_specs=pl.BlockSpec((2 * tb, d), lambda i: (i, 0)),
        scratch_shapes=[buf, buf, pltpu.SemaphoreType.DMA((2,))],
        compiler_params=_params("arbitrary"),
        name="moe_combine",
    )(pos3, pos3, y_buf, gates, x1, mod3, g_final)


def kernel(x, c, w_ada, b_ada, g_norm1, g_norm2, w_in, b_fgate, g_fox_out, lb_logits, g_hg_out,
           w_out, w_router, b_router, w_gu, b_gu, w_down, b_down, g_final):
    n_batch, seq, d = x.shape
    t = n_batch * seq
    depth = w_ada.shape[0]
    fox_heads = b_fgate.shape[1]
    fox_w = g_fox_out.shape[1]
    hg_w = g_hg_out.shape[1]
    hg_heads = hg_w // HEAD_DIM
    n_exp = w_router.shape[2]
    assert fox_w == fox_heads * HEAD_DIM and fox_heads <= LANES
    n_blocks = -(-t * TOP_K // MOE_BLOCK) + n_exp
    n_mod = w_ada.shape[2] // d

    x2d = x.reshape(t, d)
    for l in range(depth):
        mod3 = _ada_mod(c, w_ada[l], b_ada[l]).reshape(n_batch, n_mod, d)

        w_l = w_in[l].astype(BF16)
        w_main = jnp.concatenate([w_l[:, :3 * fox_w], w_l[:, 3 * fox_w + fox_heads:]], axis=1)
        w_fg = jnp.pad(w_l[:, 3 * fox_w:3 * fox_w + fox_heads], ((0, 0), (0, LANES - fox_heads)))
        b_fg = jnp.pad(b_fgate[l], (0, LANES - fox_heads)).reshape(1, LANES)
        proj, cum = _inproj(x2d, mod3, g_norm1[l].reshape(1, d), w_main, w_fg, b_fg, seq)
        proj3 = proj.reshape(n_batch, seq, -1)

        fox = _fox_attention(proj3, cum.reshape(n_batch, seq, LANES), fox_heads)

        lb3 = lb_logits.reshape(lb_logits.shape[0], hg_heads, HEAD_DIM).transpose(1, 0, 2)
        hg = _hgrn2(proj3, lb3, g_hg_out[l].reshape(hg_heads, 1, HEAD_DIM), hg_heads, 3 * fox_heads, l)

        x1, h2p, logits_t = _outproj(
            fox.reshape(t, fox_w), hg.reshape(t, hg_w), x2d, mod3, g_fox_out[l].reshape(1, fox_w),
            g_norm2[l].reshape(1, d), w_out[l].astype(BF16), w_router[l], b_router[l].reshape(n_exp, 1), seq)

        pos_t, gates_t, counts = _route(logits_t)

        cnt = counts[:, 0].astype(jnp.int32)
        padded = (cnt + MOE_BLOCK - 1) // MOE_BLOCK * MOE_BLOCK
        padded_end = jnp.cumsum(padded)
        block_row0 = jnp.arange(n_blocks, dtype=jnp.int32) * MOE_BLOCK
        block_e = jnp.sum((block_row0[:, None] >= padded_end[None, :]).astype(jnp.int32), axis=1)
        n_used = (padded_end[-1:] // MOE_BLOCK).astype(jnp.int32)
        block_id = jnp.arange(n_blocks, dtype=jnp.int32)
        block_e = jnp.minimum(block_e, jnp.sum(jnp.where(block_id == n_used - 1, block_e, 0)))
        change_at = jnp.where(block_e != jnp.roll(block_e, 1), block_id, n_blocks).at[0].set(n_blocks)
        next_change = jnp.flip(lax.cummin(jnp.flip(jnp.roll(change_at, -1).at[-1].set(n_blocks))))
        last_run = (next_change >= n_blocks).astype(jnp.int32)
        next_e = block_e[jnp.minimum(next_change, n_blocks - 1)]
        runs = (block_e, next_e, last_run)

        def tile_major(tb):
            return pos_t.reshape(TOP_K, t // tb, tb).transpose(1, 0, 2).reshape(t // tb, 1, TOP_K * tb)

        pad_count = padded - cnt
        xs = _dispatch(h2p, tile_major(min(256, seq)), padded_end - pad_count, pad_count, n_used,
                       n_blocks * MOE_BLOCK)
        act = _gateup(runs, n_used, xs, w_gu[l], b_gu[l].reshape(n_exp, 1, -1))
        y_buf = _down(runs, n_used, act, w_down[l], b_down[l].reshape(n_exp, 1, d))
        tb = min(128, seq // 2)
        pos_pairs = tile_major(tb).reshape(t // (2 * tb), 1, 2 * TOP_K * tb)
        x2d = _combine(pos_pairs, y_buf, gates_t.T, x1, mod3, g_final.reshape(1, d), seq, l == depth - 1)
    return x2d.reshape(n_batch, seq, d)
```

```python
import functools

import jax
import jax.numpy as jnp
from jax import lax
from jax.experimental import pallas as pl
from jax.experimental.pallas import tpu as pltpu

HEAD_DIM = 128
TOP_K = 4
MOE_BLOCK = 256
HG_CHUNK = 64
NORM_EPS = 1e-6
SWIGLU_ALPHA = 1.702
SWIGLU_LIMIT = 7.0
LANES = 128
VMEM_LIMIT = 56 * 1024 * 1024

F32 = jnp.float32
BF16 = jnp.bfloat16
HIGHEST = lax.Precision.HIGHEST
NT_DIMS = (((1,), (1,)), ((), ()))
TN_DIMS = (((0,), (0,)), ((), ()))


def _params(*semantics):
    return pltpu.CompilerParams(dimension_semantics=semantics, vmem_limit_bytes=VMEM_LIMIT)


def _sigmoid(z):
    return 1.0 / (1.0 + jnp.exp(-z))


def _rms(v):
    return v * lax.rsqrt(jnp.mean(v * v, axis=-1, keepdims=True) + NORM_EPS)


def _ada_kernel(ct_ref, w_ref, b_ref, o_ref, *, n_batch, d_model):
    ct = ct_ref[...]
    cond = ct * _sigmoid(ct)
    rows = 256
    for b in range(n_batch):
        acc = jnp.zeros((8, w_ref.shape[1]), F32)
        for i in range(d_model // rows):
            w3 = w_ref[pl.ds(i * rows, rows), :].reshape(rows // 8, 8, -1)
            c3 = cond[i * rows:(i + 1) * rows, b:b + 1].reshape(rows // 8, 8, 1)
            acc = acc + jnp.sum(w3 * c3, axis=0)
        o_ref[pl.ds(b, 1), :] = jnp.sum(acc, axis=0, keepdims=True) + b_ref[...]


def _ada_mod(c, w_ada, b_ada):
    n_batch, d = c.shape
    n = w_ada.shape[1]
    tn = 1024
    return pl.pallas_call(
        functools.partial(_ada_kernel, n_batch=n_batch, d_model=d),
        out_shape=jax.ShapeDtypeStruct((n_batch, n), F32),
        grid=(n // tn,),
        in_specs=[pl.BlockSpec((d, n_batch), lambda j: (0, 0)),
                  pl.BlockSpec((d, tn), lambda j: (0, j)),
                  pl.BlockSpec((1, tn), lambda j: (0, j))],
        out_specs=pl.BlockSpec((n_batch, tn), lambda j: (0, j)),
        compiler_params=_params("arbitrary"),
        name="ada_mod",
    )(c.T, w_ada, b_ada.reshape(1, n))


def _inproj_kernel(x_ref, mod_ref, g_ref, w_ref, wf_ref, bf_ref, proj_ref, cum_ref, h_sc, carry_sc,
                   *, tiles_per_batch, sub):
    i = pl.program_id(0)
    j = pl.program_id(1)

    @pl.when(j == 0)
    def _():
        x = x_ref[...]
        h = _rms(x) * (g_ref[...] * (1.0 + mod_ref[0, 1:2, :])) + mod_ref[0, 0:1, :]
        hb = h.astype(BF16)
        h_sc[...] = hb
        z = jnp.dot(hb, wf_ref[...], preferred_element_type=F32) + bf_ref[...]
        logf = jnp.minimum(z, 0.0) - jnp.log(1.0 + jnp.exp(-jnp.abs(z)))

        @pl.when(i % tiles_per_batch == 0)
        def _():
            carry_sc[...] = jnp.zeros_like(carry_sc)

        r = lax.broadcasted_iota(jnp.int32, (sub, sub), 0)
        c = lax.broadcasted_iota(jnp.int32, (sub, sub), 1)
        tri = (r >= c).astype(BF16)
        carry = carry_sc[...]
        for s in range(x.shape[0] // sub):
            blk = logf[s * sub:(s + 1) * sub, :]
            hi = blk.astype(BF16)
            rest = blk - hi.astype(F32)
            mid = rest.astype(BF16)
            low = (rest - mid.astype(F32)).astype(BF16)
            two = jnp.dot(tri, jnp.concatenate([hi, mid], axis=1), preferred_element_type=F32)
            cs = two[:, :LANES] + two[:, LANES:] + jnp.dot(tri, low, preferred_element_type=F32) + carry
            cum_ref[pl.ds(s * sub, sub), :] = cs
            carry = cs[sub - 1:sub, :]
        carry_sc[...] = carry

    proj_ref[...] = jnp.dot(h_sc[...], w_ref[...], preferred_element_type=F32).astype(BF16)


def _inproj(x2d, mod3, g1, w_main, w_fg, b_fg, seq):
    t, d = x2d.shape
    n = w_main.shape[1]
    tm = min(1024, seq)
    tn = next(c for c in (1024, 512, 256, 128) if n % c == 0)
    return pl.pallas_call(
        functools.partial(_inproj_kernel, tiles_per_batch=seq // tm, sub=256),
        out_shape=(jax.ShapeDtypeStruct((t, n), BF16), jax.ShapeDtypeStruct((t, LANES), F32)),
        grid=(t // tm, n // tn),
        in_specs=[pl.BlockSpec((tm, d), lambda i, j: (i, 0)),
                  pl.BlockSpec((1, mod3.shape[1], d), lambda i, j: (i // (seq // tm), 0, 0)),
                  pl.BlockSpec((1, d), lambda i, j: (0, 0)),
                  pl.BlockSpec((d, tn), lambda i, j: (0, j)),
                  pl.BlockSpec((d, LANES), lambda i, j: (0, 0)),
                  pl.BlockSpec((1, LANES), lambda i, j: (0, 0))],
        out_specs=(pl.BlockSpec((tm, tn), lambda i, j: (i, j)),
                   pl.BlockSpec((tm, LANES), lambda i, j: (i, 0))),
        scratch_shapes=[pltpu.VMEM((tm, d), BF16), pltpu.VMEM((1, LANES), F32)],
        compiler_params=_params("arbitrary", "arbitrary"),
        name="inproj",
    )(x2d, mod3, g1, w_main, w_fg, b_fg)


N_BIAS = 3
LOG2E = 1.4426950408889634


def _attn_kernel(q_ref, k_ref, v_ref, cum_ref, o_ref, kaug_sc, vt_sc, qt_sc, sa_sc, sb_sc, m_sc, l_sc, acc_sc,
                 *, blk, n_kv):
    h = pl.program_id(1)
    i = pl.program_id(2)

    @pl.when(i == 0)
    def _():
        lane = lax.broadcasted_iota(jnp.int32, (blk, LANES), 1)

        def prep(j, carry):
            start = pl.multiple_of(j * blk, blk)
            cum = cum_ref[0, pl.ds(start, blk), :]
            rest = jnp.sum(jnp.where(lane == h, cum, 0.0), axis=1, keepdims=True) * LOG2E
            bias = jnp.zeros((blk, LANES), F32)
            for piece in range(N_BIAS):
                part = rest.astype(BF16).astype(F32)
                bias = jnp.where(lane == piece, part, bias)
                rest = rest - part
            kaug_sc[j, :, 0:HEAD_DIM] = k_ref[0, pl.ds(start, blk), :]
            kaug_sc[j, :, HEAD_DIM:2 * HEAD_DIM] = bias.astype(BF16)
            vt_sc[j] = v_ref[0, pl.ds(start, blk), :].astype(F32).T.astype(BF16)
            return carry

        lax.fori_loop(0, n_kv, prep, 0)

    qs = q_ref[0].astype(F32) * (HEAD_DIM ** -0.5 * LOG2E)
    qt_sc[0:HEAD_DIM, :] = qs.T.astype(BF16)
    row = lax.broadcasted_iota(jnp.int32, (HEAD_DIM, blk), 0)
    qt_sc[HEAD_DIM:2 * HEAD_DIM, :] = jnp.where(row < N_BIAS, -1.0, 0.0).astype(BF16)
    m_sc[...] = jnp.full_like(m_sc, -jnp.inf)
    l_sc[...] = jnp.zeros_like(l_sc)
    acc_sc[...] = jnp.zeros_like(acc_sc)

    def scores(j):
        return jnp.dot(kaug_sc[j], qt_sc[...], preferred_element_type=F32)

    def update(j, s):
        m_prev = m_sc[...]
        m_new = jnp.maximum(m_prev, jnp.max(s, axis=0, keepdims=True))
        alpha = jnp.exp2(m_prev - m_new)
        p = jnp.exp2(s - m_new)
        l_sc[...] = alpha * l_sc[...] + jnp.sum(p, axis=0, keepdims=True)
        acc_sc[...] = alpha * acc_sc[...] + jnp.dot(vt_sc[j], p.astype(BF16), preferred_element_type=F32)
        m_sc[...] = m_new

    def causal(s):
        key = lax.broadcasted_iota(jnp.int32, (blk, blk), 0)
        qry = lax.broadcasted_iota(jnp.int32, (blk, blk), 1)
        return jnp.where(key <= qry, s, -jnp.inf)

    sa_sc[...] = scores(0)

    def pair_at(j):
        sb_sc[...] = scores(j + 1)
        update(j, sa_sc[...])
        sa_sc[...] = scores(j + 2)
        update(j + 1, sb_sc[...])

    def quad(jj, carry):
        pair_at(4 * jj)
        pair_at(4 * jj + 2)
        return carry

    def pair(jj, carry):
        pair_at(4 * (i // 4) + 2 * jj)
        return carry

    lax.fori_loop(0, i // 4, quad, 0)
    lax.fori_loop(0, (i % 4) // 2, pair, 0)

    @pl.when(i % 2 == 0)
    def _():
        update(i, causal(sa_sc[...]))

    @pl.when(i % 2 == 1)
    def _():
        sb_sc[...] = scores(i)
        update(i - 1, sa_sc[...])
        update(i, causal(sb_sc[...]))

    o_ref[0] = (acc_sc[...] / l_sc[...]).T.astype(BF16)


def _fox_attention(proj3, cum3, n_heads):
    n_batch, seq, _ = proj3.shape
    blk = min(512, seq)
    n_kv = seq // blk
    return pl.pallas_call(
        functools.partial(_attn_kernel, blk=blk, n_kv=n_kv),
        out_shape=jax.ShapeDtypeStruct((n_batch, seq, n_heads * HEAD_DIM), BF16),
        grid=(n_batch, n_heads, n_kv),
        in_specs=[pl.BlockSpec((1, blk, HEAD_DIM), lambda b, h, i: (b, i, h)),
                  pl.BlockSpec((1, seq, HEAD_DIM), lambda b, h, i: (b, 0, n_heads + h)),
                  pl.BlockSpec((1, seq, HEAD_DIM), lambda b, h, i: (b, 0, 2 * n_heads + h)),
                  pl.BlockSpec((1, seq, LANES), lambda b, h, i: (b, 0, 0))],
        out_specs=pl.BlockSpec((1, blk, HEAD_DIM), lambda b, h, i: (b, i, h)),
        scratch_shapes=[pltpu.VMEM((n_kv, blk, 2 * HEAD_DIM), BF16),
                        pltpu.VMEM((n_kv, HEAD_DIM, blk), BF16),
                        pltpu.VMEM((2 * HEAD_DIM, blk), BF16),
                        pltpu.VMEM((blk, blk), F32), pltpu.VMEM((blk, blk), F32),
                        pltpu.VMEM((1, blk), F32), pltpu.VMEM((1, blk), F32),
                        pltpu.VMEM((HEAD_DIM, blk), F32)],
        compiler_params=_params("arbitrary", "arbitrary", "arbitrary"),
        name="fox_attention",
    )(proj3, proj3, proj3, cum3)


def _hgrn_kernel(q_ref, f_ref, i_ref, g_ref, lbl_ref, gn_ref, o_ref, st_sc, *, rows, sub, layer):
    @pl.when(pl.program_id(2) == 0)
    def _():
        st_sc[...] = jnp.zeros_like(st_sc)

    lbl = lbl_ref[0]
    e = jnp.exp(lbl - jnp.max(lbl, axis=0, keepdims=True))
    lb = jnp.sum(e[0:layer + 1, :], axis=0, keepdims=True) / jnp.sum(e, axis=0, keepdims=True)
    f = lb + (1.0 - lb) * _sigmoid(f_ref[0].astype(F32))
    logf = jnp.log(f)
    kk = 1.0 - f
    qf = q_ref[0].astype(F32)
    qq = qf * _sigmoid(qf)

    n_ch = sub // HG_CHUNK
    r = lax.broadcasted_iota(jnp.int32, (sub, sub), 0)
    c = lax.broadcasted_iota(jnp.int32, (sub, sub), 1)
    within = (r >= c) & (r // HG_CHUNK == c // HG_CHUNK)
    tri = within.astype(BF16)
    rw = lax.broadcasted_iota(jnp.int32, (sub, n_ch * HEAD_DIM), 0)
    cw = lax.broadcasted_iota(jnp.int32, (sub, n_ch * HEAD_DIM), 1)
    own_block = rw // HG_CHUNK == cw // HEAD_DIM

    zero = jnp.zeros((), BF16)
    n_sub = rows // sub
    b_all, b_last_all = [], []
    for s in range(n_sub):
        lf = logf[s * sub:(s + 1) * sub, :]
        hi = lf.astype(BF16)
        rest = lf - hi.astype(F32)
        mid = rest.astype(BF16)
        low = (rest - mid.astype(F32)).astype(BF16)
        two = jnp.dot(tri, jnp.concatenate([hi, mid], axis=1), preferred_element_type=F32)
        b = two[:, :HEAD_DIM] + two[:, HEAD_DIM:] + jnp.dot(tri, low, preferred_element_type=F32)
        b_all.append(b)
        b_last_all.append([b[(n + 1) * HG_CHUNK - 1:(n + 1) * HG_CHUNK, :] for n in range(n_ch)])

    q_dec_all, o_all, upd_all = [], [], []
    for s in range(n_sub):
        rs = slice(s * sub, (s + 1) * sub)
        b = b_all[s]
        b_last_rows = jnp.concatenate([jnp.broadcast_to(bl, (HG_CHUNK, HEAD_DIM)) for bl in b_last_all[s]], axis=0)
        q_dec = (qq[rs, :] * jnp.exp(b)).astype(BF16)
        k_inv = (kk[rs, :] * jnp.exp(-b)).astype(BF16)
        k_tail = (kk[rs, :] * jnp.exp(b_last_rows - b)).astype(BF16)
        vv = i_ref[0, pl.ds(s * sub, sub), :]
        attn = lax.dot_general(q_dec, k_inv, NT_DIMS, preferred_element_type=F32)
        o_all.append(jnp.dot(jnp.where(within, attn, 0.0).astype(BF16), vv, preferred_element_type=F32))
        k_blocks = jnp.where(own_block, jnp.concatenate([k_tail] * n_ch, axis=1), zero)
        upd_all.append(lax.dot_general(vv, k_blocks, TN_DIMS, preferred_element_type=F32))
        q_dec_all.append(q_dec)

    st = st_sc[...]
    for s in range(n_sub):
        states = []
        for n in range(n_ch):
            states.append(st.astype(BF16))
            st = jnp.exp(b_last_all[s][n]) * st + upd_all[s][:, n * HEAD_DIM:(n + 1) * HEAD_DIM]
        q_blocks = jnp.where(own_block, jnp.concatenate([q_dec_all[s]] * n_ch, axis=1), zero)
        o = o_all[s] + lax.dot_general(q_blocks, jnp.concatenate(states, axis=1), NT_DIMS,
                                       preferred_element_type=F32)
        y = _rms(o) * gn_ref[0]
        gf = g_ref[0, pl.ds(s * sub, sub), :].astype(F32)
        o_ref[0, pl.ds(s * sub, sub), :] = (y * (gf * _sigmoid(gf))).astype(BF16)
    st_sc[...] = st


def _hgrn2(proj3, lb_logits3, g_hg3, n_heads, col0, layer):
    n_batch, seq, _ = proj3.shape
    rows = min(1024, seq)
    sub = min(256, rows)
    spec = lambda off: pl.BlockSpec((1, rows, HEAD_DIM), lambda b, h, r: (b, r, col0 + off * n_heads + h))
    return pl.pallas_call(
        functools.partial(_hgrn_kernel, rows=rows, sub=sub, layer=layer),
        out_shape=jax.ShapeDtypeStruct((n_batch, seq, n_heads * HEAD_DIM), BF16),
        grid=(n_batch, n_heads, seq // rows),
        in_specs=[spec(0), spec(1), spec(2), spec(3),
                  pl.BlockSpec((1, lb_logits3.shape[1], HEAD_DIM), lambda b, h, r: (h, 0, 0)),
                  pl.BlockSpec((1, 1, HEAD_DIM), lambda b, h, r: (h, 0, 0))],
        out_specs=pl.BlockSpec((1, rows, HEAD_DIM), lambda b, h, r: (b, r, h)),
        scratch_shapes=[pltpu.VMEM((HEAD_DIM, HEAD_DIM), F32)],
        compiler_params=_params("arbitrary", "arbitrary", "arbitrary"),
        name="hgrn2",
    )(proj3, proj3, proj3, proj3, lb_logits3, g_hg3)


def _pack_halves(v):
    n = v.shape[1] // 2
    lo = lax.bitcast_convert_type(v[:, :n].astype(BF16).astype(F32), jnp.uint32) >> 16
    hi = lax.bitcast_convert_type(v[:, n:].astype(BF16).astype(F32), jnp.uint32) & jnp.uint32(0xFFFF0000)
    return lo | hi


def _unpack_halves(p):
    lo = lax.bitcast_convert_type(p << 16, F32).astype(BF16)
    hi = lax.bitcast_convert_type(p & jnp.uint32(0xFFFF0000), F32).astype(BF16)
    return lo, hi


def _outproj_kernel(fox_ref, hg_ref, x_ref, mod_ref, gfox_ref, g2_ref, w_ref, wr_ref, br_ref,
                    x1_ref, h2_ref, lg_ref, *, n_exp, group):
    groups = [pl.ds(k * group, group) for k in range(x_ref.shape[0] // group)]
    mixed = []
    for rows in groups:
        fox = _rms(fox_ref[rows, :].astype(F32)) * gfox_ref[...]
        mixed.append(jnp.concatenate([fox.astype(BF16), hg_ref[rows, :]], axis=1))
    mix = [jnp.dot(m, w_ref[...], preferred_element_type=F32) for m in mixed]
    h2s = []
    for rows, m in zip(groups, mix):
        x1 = x_ref[rows, :] + mod_ref[0, 2:3, :] * m
        x1_ref[rows, :] = x1
        h2 = _rms(x1) * (g2_ref[...] * (1.0 + mod_ref[0, 4:5, :])) + mod_ref[0, 3:4, :]
        h2_ref[rows, :] = _pack_halves(h2)
        h2s.append(h2)
    for rows, h2 in zip(groups, h2s):
        h_hi = h2.astype(BF16)
        h_lo = (h2 - h_hi.astype(F32)).astype(BF16)
        part = jnp.dot(h_hi, wr_ref[...], preferred_element_type=F32)
        part = part + jnp.dot(h_lo, wr_ref[...], preferred_element_type=F32)
        logits = part + pltpu.roll(part, LANES - n_exp, axis=1)
        lg_ref[:, rows] = logits.T[0:n_exp, :] + br_ref[...]


def _outproj(fox2d, hg2d, x2d, mod3, g_fox, g2, w_out, w_router, b_router, seq):
    t, d = x2d.shape
    fox_w = fox2d.shape[1]
    n_exp = w_router.shape[1]
    assert 2 * n_exp <= LANES
    wr_hi = w_router.astype(BF16)
    wr_lo = (w_router - wr_hi.astype(F32)).astype(BF16)
    wr_cat = jnp.pad(jnp.concatenate([wr_hi, wr_lo], axis=1), ((0, 0), (0, LANES - 2 * n_exp)))
    tm = 512
    row = lambda i: (i, 0)
    const = lambda i: (0, 0)
    return pl.pallas_call(
        functools.partial(_outproj_kernel, n_exp=n_exp, group=256),
        out_shape=(jax.ShapeDtypeStruct((t, d), F32), jax.ShapeDtypeStruct((t, d // 2), jnp.uint32),
                   jax.ShapeDtypeStruct((n_exp, t), F32)),
        grid=(t // tm,),
        in_specs=[pl.BlockSpec((tm, fox_w), row),
                  pl.BlockSpec((tm, hg2d.shape[1]), row),
                  pl.BlockSpec((tm, d), row),
                  pl.BlockSpec((1, mod3.shape[1], d), lambda i: (i // (seq // tm), 0, 0)),
                  pl.BlockSpec((1, fox_w), const),
                  pl.BlockSpec((1, d), const),
                  pl.BlockSpec(w_out.shape, const),
                  pl.BlockSpec((d, LANES), const),
                  pl.BlockSpec((n_exp, 1), const)],
        out_specs=(pl.BlockSpec((tm, d), row), pl.BlockSpec((tm, d // 2), row),
                   pl.BlockSpec((n_exp, tm), lambda i: (0, i))),
        compiler_params=_params("arbitrary"),
        name="outproj_router",
    )(fox2d, hg2d, x2d, mod3, g_fox, g2, w_out, wr_cat, b_router)


def _route_kernel(lg_ref, pos_ref, gate_ref, cnt_ref, cnt_sc, run_sc, *, n_exp, tb):
    phase = pl.program_id(0)
    i = pl.program_id(1)

    @pl.when((phase == 0) & (i == 0))
    def _():
        cnt_sc[...] = jnp.zeros_like(cnt_sc)
        run_sc[...] = jnp.zeros_like(run_sc)

    logits = lg_ref[...]
    eidx = lax.broadcasted_iota(jnp.int32, (n_exp, tb), 0).astype(F32)
    work = logits
    vals, hots = [], []
    for _ in range(TOP_K):
        m = jnp.max(work, axis=0, keepdims=True)
        first = jnp.min(jnp.where(work == m, eidx, float(n_exp)), axis=0, keepdims=True)
        hot = eidx == first
        vals.append(m)
        hots.append(hot)
        work = jnp.where(hot, -jnp.inf, work)
    sel = hots[0] | hots[1] | hots[2] | hots[3]
    self32 = sel.astype(F32)

    @pl.when(phase == 0)
    def _():
        cnt_sc[...] += jnp.sum(self32, axis=1, keepdims=True)

    @pl.when(phase == 1)
    def _():
        cnt = cnt_sc[...]
        padded = jnp.ceil(cnt / MOE_BLOCK) * MOE_BLOCK
        r = lax.broadcasted_iota(jnp.int32, (n_exp, n_exp), 0)
        c = lax.broadcasted_iota(jnp.int32, (n_exp, n_exp), 1)
        strict = (c < r).astype(F32)
        pstart = jnp.dot(strict, jnp.broadcast_to(padded, (n_exp, LANES)), preferred_element_type=F32,
                         precision=HIGHEST)[:, 0:1]
        tr = lax.broadcasted_iota(jnp.int32, (tb, tb), 0)
        tc = lax.broadcasted_iota(jnp.int32, (tb, tb), 1)
        upper = (tr < tc).astype(BF16)
        rank = jnp.dot(sel.astype(BF16), upper, preferred_element_type=F32)
        base = pstart + run_sc[...] + rank
        exps = [jnp.exp(v - vals[0]) for v in vals]
        denom = exps[0] + exps[1] + exps[2] + exps[3]
        for k in range(TOP_K):
            pos_k = jnp.sum(jnp.where(hots[k], base, 0.0), axis=0, keepdims=True)
            pos_ref[pl.ds(k, 1), :] = pos_k.astype(jnp.int32)
            gate_ref[pl.ds(k, 1), :] = exps[k] / denom
        run_sc[...] += jnp.sum(self32, axis=1, keepdims=True)
        cnt_ref[...] = cnt


def _route(logits_t):
    n_exp, t = logits_t.shape
    tb = min(512, t)
    return pl.pallas_call(
        functools.partial(_route_kernel, n_exp=n_exp, tb=tb),
        out_shape=(jax.ShapeDtypeStruct((TOP_K, t), jnp.int32), jax.ShapeDtypeStruct((TOP_K, t), F32),
                   jax.ShapeDtypeStruct((n_exp, 1), F32)),
        grid=(2, t // tb),
        in_specs=[pl.BlockSpec((n_exp, tb), lambda p, i: (0, i))],
        out_specs=(pl.BlockSpec((TOP_K, tb), lambda p, i: (0, i * p)),
                   pl.BlockSpec((TOP_K, tb), lambda p, i: (0, i * p)),
                   pl.BlockSpec((n_exp, 1), lambda p, i: (0, 0))),
        scratch_shapes=[pltpu.VMEM((n_exp, 1), F32), pltpu.VMEM((n_exp, 1), F32)],
        compiler_params=_params("arbitrary", "arbitrary"),
        name="route_topk",
    )(logits_t)


N_DMA_QUEUES = 2


def _row_copy(src_hbm, row, dst_vmem, slot, sem):
    return pltpu.make_async_copy(src_hbm.at[pl.ds(row, 1), :], dst_vmem.at[pl.ds(slot, 1), :], sem)


def _gather_rows(src_hbm, idx_ref, n_rows, dst_vmem, sem):
    for r in range(n_rows):
        _row_copy(src_hbm, idx_ref[0, 0, r], dst_vmem, r, sem).start(priority=r % N_DMA_QUEUES)


def _wait_rows(src_hbm, n_rows, dst_vmem, sem):
    pltpu.make_async_copy(src_hbm.at[pl.ds(0, n_rows), :], dst_vmem, sem).wait()


def _dispatch_kernel(pad0_ref, padn_ref, nused_ref, pos_ref, h_ref, xs_hbm, zeros, sem, zsem,
                     *, tb, n_exp, n_blocks):
    @pl.when(pl.program_id(0) == 0)
    def _():
        zeros[...] = jnp.zeros_like(zeros)

        def each_fill(fn):
            def per_expert(e, carry):
                def per_row(r, inner):
                    fn(pltpu.make_async_copy(zeros.at[pl.ds(0, 1), :], xs_hbm.at[pl.ds(pad0_ref[e] + r, 1), :], zsem))
                    return inner

                lax.fori_loop(0, padn_ref[e], per_row, 0)
                return carry

            lax.fori_loop(0, n_exp, per_expert, 0)

            def per_block(b, carry):
                row0 = pl.multiple_of(b * MOE_BLOCK, MOE_BLOCK)
                fn(pltpu.make_async_copy(zeros, xs_hbm.at[pl.ds(row0, MOE_BLOCK), :], zsem))
                return carry

            lax.fori_loop(nused_ref[0], n_blocks, per_block, 0)

        each_fill(lambda cp: cp.start())
        each_fill(lambda cp: cp.wait())

    for r in range(tb):
        for k in range(TOP_K):
            pltpu.make_async_copy(h_ref.at[pl.ds(r, 1), :], xs_hbm.at[pl.ds(pos_ref[0, 0, k * tb + r], 1), :],
                                  sem).start(priority=k % N_DMA_QUEUES)
    for k in range(TOP_K):
        pltpu.make_async_copy(h_ref, xs_hbm.at[pl.ds(0, tb), :], sem).wait()


def _dispatch(h2p, pos3, pad_start, pad_count, n_used, n_rows):
    t, dp = h2p.shape
    n_tiles = pos3.shape[0]
    tb = pos3.shape[2] // TOP_K
    return pl.pallas_call(
        functools.partial(_dispatch_kernel, tb=tb, n_exp=pad_start.shape[0], n_blocks=n_rows // MOE_BLOCK),
        out_shape=jax.ShapeDtypeStruct((n_rows, dp), h2p.dtype),
        grid_spec=pltpu.PrefetchScalarGridSpec(
            num_scalar_prefetch=3,
            grid=(n_tiles,),
            in_specs=[pl.BlockSpec((1, 1, TOP_K * tb), lambda i, p0, pn, nu: (i, 0, 0), memory_space=pltpu.SMEM),
                      pl.BlockSpec((tb, dp), lambda i, p0, pn, nu: (i, 0))],
            out_specs=pl.BlockSpec(memory_space=pl.ANY),
            scratch_shapes=[pltpu.VMEM((MOE_BLOCK, dp), h2p.dtype), pltpu.SemaphoreType.DMA(()),
                            pltpu.SemaphoreType.DMA(())]),
        compiler_params=_params("arbitrary"),
        name="moe_dispatch",
    )(pad_start, pad_count, n_used, pos3, h2p)


def _new_expert(be_ref, b):
    return (b == 0) | (be_ref[b] != be_ref[jnp.maximum(b - 1, 0)])


W_PIECES = 4


def _weight_copies(w_hbm, e, col0, stage, sem):
    rows = stage.shape[0] // W_PIECES
    return [pltpu.make_async_copy(w_hbm.at[e, pl.ds(i * rows, rows), pl.ds(col0, stage.shape[1])],
                                  stage.at[pl.ds(i * rows, rows), :], sem) for i in range(W_PIECES)]


def _convert_rows(src_f32, dst_bf16):
    rows = 16

    def body(i, carry):
        r0 = pl.multiple_of(i * rows, rows)
        dst_bf16[pl.ds(r0, rows), :] = src_f32[pl.ds(r0, rows), :].astype(BF16)
        return carry

    lax.fori_loop(0, src_f32.shape[0] // rows, body, 0, unroll=4)


def _gateup_kernel(be_ref, nxt_ref, last_ref, nused_ref, x_ref, w_hbm, bg_ref, bl_ref, o_ref,
                   stg_g, stg_l, wg_sc, wl_sc, sem, *, nf, tf):
    c = pl.program_id(0)
    b = pl.program_id(1)

    def slab(e, chunk):
        col = pl.multiple_of(chunk * tf, tf)
        return _weight_copies(w_hbm, e, col, stg_g, sem) + _weight_copies(w_hbm, e, col + nf * tf, stg_l, sem)

    @pl.when((c == 0) & (b == 0))
    def _():
        for cp in slab(be_ref[0], 0):
            cp.start()

    @pl.when(_new_expert(be_ref, b))
    def _():
        for cp in slab(0, 0):
            cp.wait()
        _convert_rows(stg_g, wg_sc)
        _convert_rows(stg_l, wl_sc)
        last_run = last_ref[b] == 1

        @pl.when(jnp.logical_not(last_run))
        def _():
            for cp in slab(nxt_ref[b], c):
                cp.start()

        @pl.when(last_run & (c + 1 < nf))
        def _():
            for cp in slab(be_ref[0], c + 1):
                cp.start()

    @pl.when(b < nused_ref[0])
    def _():
        x = jnp.concatenate(_unpack_halves(x_ref[...]), axis=1)
        glu = jnp.dot(x, wg_sc[...], preferred_element_type=F32) + bg_ref[0]
        lin = jnp.dot(x, wl_sc[...], preferred_element_type=F32) + bl_ref[0]
        glu = jnp.minimum(glu, SWIGLU_LIMIT)
        lin = jnp.clip(lin, -SWIGLU_LIMIT, SWIGLU_LIMIT)
        o_ref[...] = (glu * _sigmoid(SWIGLU_ALPHA * glu) * (lin + 1.0)).astype(BF16)

    @pl.when(b >= nused_ref[0])
    def _():
        o_ref[...] = jnp.zeros_like(o_ref)


def _gateup(runs, n_used, xs, w_gu, b_gu3):
    n_rows, dp = xs.shape
    d = 2 * dp
    n_blocks = n_rows // MOE_BLOCK
    d_ff = w_gu.shape[2] // 2
    tf = min(1024, d_ff)
    nf = d_ff // tf
    return pl.pallas_call(
        functools.partial(_gateup_kernel, nf=nf, tf=tf),
        out_shape=jax.ShapeDtypeStruct((n_rows, d_ff), BF16),
        grid_spec=pltpu.PrefetchScalarGridSpec(
            num_scalar_prefetch=4,
            grid=(nf, n_blocks),
            in_specs=[pl.BlockSpec((MOE_BLOCK, dp), lambda c, b, be, nx, la, nu: (jnp.minimum(b, nu[0] - 1), 0)),
                      pl.BlockSpec(memory_space=pl.ANY),
                      pl.BlockSpec((1, 1, tf), lambda c, b, be, nx, la, nu: (be[b], 0, c)),
                      pl.BlockSpec((1, 1, tf), lambda c, b, be, nx, la, nu: (be[b], 0, nf + c))],
            out_specs=pl.BlockSpec((MOE_BLOCK, tf), lambda c, b, be, nx, la, nu: (b, c)),
            scratch_shapes=[pltpu.VMEM((d, tf), F32), pltpu.VMEM((d, tf), F32),
                            pltpu.VMEM((d, tf), BF16), pltpu.VMEM((d, tf), BF16),
                            pltpu.SemaphoreType.DMA(())]),
        compiler_params=_params("arbitrary", "arbitrary"),
        name="moe_gateup",
    )(*runs, n_used, xs, w_gu, b_gu3, b_gu3)


def _down_kernel(be_ref, nxt_ref, last_ref, nused_ref, a_ref, w_hbm, b_ref, o_ref, stg, w_sc, sem):
    b = pl.program_id(0)

    @pl.when(b == 0)
    def _():
        for cp in _weight_copies(w_hbm, be_ref[0], 0, stg, sem):
            cp.start()

    @pl.when(_new_expert(be_ref, b))
    def _():
        for cp in _weight_copies(w_hbm, 0, 0, stg, sem):
            cp.wait()
        _convert_rows(stg, w_sc)

        @pl.when(last_ref[b] == 0)
        def _():
            for cp in _weight_copies(w_hbm, nxt_ref[b], 0, stg, sem):
                cp.start()

    @pl.when(b < nused_ref[0])
    def _():
        y = jnp.dot(a_ref[...], w_sc[...], preferred_element_type=F32) + b_ref[0]
        o_ref[...] = _pack_halves(y)

    @pl.when(b >= nused_ref[0])
    def _():
        o_ref[...] = jnp.zeros_like(o_ref)


def _down(runs, n_used, act, w_down, b_down3):
    n_rows, d_ff = act.shape
    n_blocks = n_rows // MOE_BLOCK
    d = w_down.shape[2]
    return pl.pallas_call(
        _down_kernel,
        out_shape=jax.ShapeDtypeStruct((n_rows, d // 2), jnp.uint32),
        grid_spec=pltpu.PrefetchScalarGridSpec(
            num_scalar_prefetch=4,
            grid=(n_blocks,),
            in_specs=[pl.BlockSpec((MOE_BLOCK, d_ff), lambda b, be, nx, la, nu: (b, 0)),
                      pl.BlockSpec(memory_space=pl.ANY),
                      pl.BlockSpec((1, 1, d), lambda b, be, nx, la, nu: (be[b], 0, 0))],
            out_specs=pl.BlockSpec((MOE_BLOCK, d // 2), lambda b, be, nx, la, nu: (b, 0)),
            scratch_shapes=[pltpu.VMEM((d_ff, d), F32), pltpu.VMEM((d_ff, d), BF16),
                            pltpu.SemaphoreType.DMA(())]),
        compiler_params=_params("arbitrary"),
        name="moe_down",
    )(*runs, n_used, act, w_down, b_down3)


def _combine_kernel(pos_ref, pos_next_ref, y_hbm, gate_ref, x1_ref, mod_ref, gfin_ref, o_ref, buf, sem,
                    *, tb, n_tiles, last_layer):
    i = pl.program_id(0)
    slot = i % 2

    @pl.when(i == 0)
    def _():
        _gather_rows(y_hbm, pos_ref, TOP_K * tb, buf.at[0], sem.at[0])

    @pl.when(i + 1 < n_tiles)
    def _():
        _gather_rows(y_hbm, pos_next_ref, TOP_K * tb, buf.at[1 - slot], sem.at[1 - slot])

    _wait_rows(y_hbm, TOP_K * tb, buf.at[slot], sem.at[slot])
    gates = gate_ref[...]
    moe_lo = moe_hi = None
    for k in range(TOP_K):
        lo, hi = _unpack_halves(buf[slot, pl.ds(k * tb, tb), :])
        g = gates[:, k:k + 1]
        moe_lo = g * lo.astype(F32) if k == 0 else moe_lo + g * lo.astype(F32)
        moe_hi = g * hi.astype(F32) if k == 0 else moe_hi + g * hi.astype(F32)
    moe = jnp.concatenate([moe_lo, moe_hi], axis=1)
    x2 = x1_ref[...] + mod_ref[0, 5:6, :] * moe
    o_ref[...] = _rms(x2) * gfin_ref[...] if last_layer else x2


def _combine(pos3, y_buf, gates, x1, mod3, g_final, seq, last_layer):
    t, d = x1.shape
    n_tiles = pos3.shape[0]
    tb = pos3.shape[2] // TOP_K
    idx_spec = lambda step: pl.BlockSpec((1, 1, tb * TOP_K), lambda i: (jnp.minimum(i + step, n_tiles - 1), 0, 0),
                                         memory_space=pltpu.SMEM)
    return pl.pallas_call(
        functools.partial(_combine_kernel, tb=tb, n_tiles=n_tiles, last_layer=last_layer),
        out_shape=jax.ShapeDtypeStruct((t, d), F32),
        grid=(n_tiles,),
        in_specs=[idx_spec(0), idx_spec(1),
                  pl.BlockSpec(memory_space=pl.ANY),
                  pl.BlockSpec((tb, TOP_K), lambda i: (i, 0)),
                  pl.BlockSpec((tb, d), lambda i: (i, 0)),
                  pl.BlockSpec((1, mod3.shape[1], d), lambda i: (i // (seq // tb), 0, 0)),
                  pl.BlockSpec((1, d), lambda i: (0, 0))],
        out_specs=pl.BlockSpec((tb, d), lambda i: (i, 0)),
        scratch_shapes=[pltpu.VMEM((2, TOP_K * tb, y_buf.shape[1]), y_buf.dtype), pltpu.SemaphoreType.DMA((2,))],
        compiler_params=_params("arbitrary"),
        name="moe_combine",
    )(pos3, pos3, y_buf, gates, x1, mod3, g_final)


def kernel(x, c, w_ada, b_ada, g_norm1, g_norm2, w_in, b_fgate, g_fox_out, lb_logits, g_hg_out,
           w_out, w_router, b_router, w_gu, b_gu, w_down, b_down, g_final):
    n_batch, seq, d = x.shape
    t = n_batch * seq
    depth = w_ada.shape[0]
    fox_heads = b_fgate.shape[1]
    fox_w = g_fox_out.shape[1]
    hg_w = g_hg_out.shape[1]
    hg_heads = hg_w // HEAD_DIM
    n_exp = w_router.shape[2]
    assert fox_w == fox_heads * HEAD_DIM and fox_heads <= LANES
    n_blocks = -(-t * TOP_K // MOE_BLOCK) + n_exp
    n_mod = w_ada.shape[2] // d

    x2d = x.reshape(t, d)
    for l in range(depth):
        mod3 = _ada_mod(c, w_ada[l], b_ada[l]).reshape(n_batch, n_mod, d)

        w_l = w_in[l].astype(BF16)
        w_main = jnp.concatenate([w_l[:, :3 * fox_w], w_l[:, 3 * fox_w + fox_heads:]], axis=1)
        w_fg = jnp.pad(w_l[:, 3 * fox_w:3 * fox_w + fox_heads], ((0, 0), (0, LANES - fox_heads)))
        b_fg = jnp.pad(b_fgate[l], (0, LANES - fox_heads)).reshape(1, LANES)
        proj, cum = _inproj(x2d, mod3, g_norm1[l].reshape(1, d), w_main, w_fg, b_fg, seq)
        proj3 = proj.reshape(n_batch, seq, -1)

        fox = _fox_attention(proj3, cum.reshape(n_batch, seq, LANES), fox_heads)

        lb3 = lb_logits.reshape(lb_logits.shape[0], hg_heads, HEAD_DIM).transpose(1, 0, 2)
        hg = _hgrn2(proj3, lb3, g_hg_out[l].reshape(hg_heads, 1, HEAD_DIM), hg_heads, 3 * fox_heads, l)

        x1, h2p, logits_t = _outproj(
            fox.reshape(t, fox_w), hg.reshape(t, hg_w), x2d, mod3, g_fox_out[l].reshape(1, fox_w),
            g_norm2[l].reshape(1, d), w_out[l].astype(BF16), w_router[l], b_router[l].reshape(n_exp, 1), seq)

        pos_t, gates_t, counts = _route(logits_t)

        cnt = counts[:, 0].astype(jnp.int32)
        padded = (cnt + MOE_BLOCK - 1) // MOE_BLOCK * MOE_BLOCK
        padded_end = jnp.cumsum(padded)
        block_row0 = jnp.arange(n_blocks, dtype=jnp.int32) * MOE_BLOCK
        block_e = jnp.sum((block_row0[:, None] >= padded_end[None, :]).astype(jnp.int32), axis=1)
        n_used = (padded_end[-1:] // MOE_BLOCK).astype(jnp.int32)
        block_id = jnp.arange(n_blocks, dtype=jnp.int32)
        block_e = jnp.minimum(block_e, jnp.sum(jnp.where(block_id == n_used - 1, block_e, 0)))
        change_at = jnp.where(block_e != jnp.roll(block_e, 1), block_id, n_blocks).at[0].set(n_blocks)
        next_change = jnp.flip(lax.cummin(jnp.flip(jnp.roll(change_at, -1).at[-1].set(n_blocks))))
        last_run = (next_change >= n_blocks).astype(jnp.int32)
        next_e = block_e[jnp.minimum(next_change, n_blocks - 1)]
        runs = (block_e, next_e, last_run)

        def tile_major(tb):
            return pos_t.reshape(TOP_K, t // tb, tb).transpose(1, 0, 2).reshape(t // tb, 1, TOP_K * tb)

        pad_count = padded - cnt
        xs = _dispatch(h2p, tile_major(min(256, seq)), padded_end - pad_count, pad_count, n_used,
                       n_blocks * MOE_BLOCK)
        act = _gateup(runs, n_used, xs, w_gu[l], b_gu[l].reshape(n_exp, 1, -1))
        y_buf = _down(runs, n_used, act, w_down[l], b_down[l].reshape(n_exp, 1, d))
        x2d = _combine(tile_major(min(128, seq)), y_buf, gates_t.T, x1, mod3, g_final.reshape(1, d), seq,
                       l == depth - 1)
    return x2d.reshape(n_batch, seq, d)
```

```python
import functools

import jax
import jax.numpy as jnp
from jax import lax
from jax.experimental import pallas as pl
from jax.experimental.pallas import tpu as pltpu

HEAD_DIM = 128
TOP_K = 4
MOE_BLOCK = 256
HG_CHUNK = 64
NORM_EPS = 1e-6
SWIGLU_ALPHA = 1.702
SWIGLU_LIMIT = 7.0
LANES = 128
VMEM_LIMIT = 56 * 1024 * 1024

F32 = jnp.float32
BF16 = jnp.bfloat16
HIGHEST = lax.Precision.HIGHEST
NT_DIMS = (((1,), (1,)), ((), ()))
TN_DIMS = (((0,), (0,)), ((), ()))


def _params(*semantics):
    return pltpu.CompilerParams(dimension_semantics=semantics, vmem_limit_bytes=VMEM_LIMIT)


def _sigmoid(z):
    return 1.0 / (1.0 + jnp.exp(-z))


def _rms(v):
    return v * lax.rsqrt(jnp.mean(v * v, axis=-1, keepdims=True) + NORM_EPS)


def _ada_kernel(ct_ref, w_ref, b_ref, o_ref, *, n_batch, d_model):
    ct = ct_ref[...]
    cond = ct * _sigmoid(ct)
    rows = 256
    for b in range(n_batch):
        acc = jnp.zeros((8, w_ref.shape[1]), F32)
        for i in range(d_model // rows):
            w3 = w_ref[pl.ds(i * rows, rows), :].reshape(rows // 8, 8, -1)
            c3 = cond[i * rows:(i + 1) * rows, b:b + 1].reshape(rows // 8, 8, 1)
            acc = acc + jnp.sum(w3 * c3, axis=0)
        o_ref[pl.ds(b, 1), :] = jnp.sum(acc, axis=0, keepdims=True) + b_ref[...]


def _ada_mod(c, w_ada, b_ada):
    n_batch, d = c.shape
    n = w_ada.shape[1]
    tn = 1024
    return pl.pallas_call(
        functools.partial(_ada_kernel, n_batch=n_batch, d_model=d),
        out_shape=jax.ShapeDtypeStruct((n_batch, n), F32),
        grid=(n // tn,),
        in_specs=[pl.BlockSpec((d, n_batch), lambda j: (0, 0)),
                  pl.BlockSpec((d, tn), lambda j: (0, j)),
                  pl.BlockSpec((1, tn), lambda j: (0, j))],
        out_specs=pl.BlockSpec((n_batch, tn), lambda j: (0, j)),
        compiler_params=_params("arbitrary"),
        name="ada_mod",
    )(c.T, w_ada, b_ada.reshape(1, n))


def _inproj_kernel(x_ref, mod_ref, g_ref, w_ref, wf_ref, bf_ref, proj_ref, cum_ref, h_sc, carry_sc,
                   *, tiles_per_batch, sub):
    i = pl.program_id(0)
    j = pl.program_id(1)

    @pl.when(j == 0)
    def _():
        x = x_ref[...]
        h = _rms(x) * (g_ref[...] * (1.0 + mod_ref[0, 1:2, :])) + mod_ref[0, 0:1, :]
        hb = h.astype(BF16)
        h_sc[...] = hb
        z = jnp.dot(hb, wf_ref[...], preferred_element_type=F32) + bf_ref[...]
        logf = jnp.minimum(z, 0.0) - jnp.log(1.0 + jnp.exp(-jnp.abs(z)))

        @pl.when(i % tiles_per_batch == 0)
        def _():
            carry_sc[...] = jnp.zeros_like(carry_sc)

        r = lax.broadcasted_iota(jnp.int32, (sub, sub), 0)
        c = lax.broadcasted_iota(jnp.int32, (sub, sub), 1)
        tri = (r >= c).astype(BF16)
        carry = carry_sc[...]
        for s in range(x.shape[0] // sub):
            blk = logf[s * sub:(s + 1) * sub, :]
            hi = blk.astype(BF16)
            rest = blk - hi.astype(F32)
            mid = rest.astype(BF16)
            low = (rest - mid.astype(F32)).astype(BF16)
            two = jnp.dot(tri, jnp.concatenate([hi, mid], axis=1), preferred_element_type=F32)
            cs = two[:, :LANES] + two[:, LANES:] + jnp.dot(tri, low, preferred_element_type=F32) + carry
            cum_ref[pl.ds(s * sub, sub), :] = cs
            carry = cs[sub - 1:sub, :]
        carry_sc[...] = carry

    proj_ref[...] = jnp.dot(h_sc[...], w_ref[...], preferred_element_type=F32).astype(BF16)


def _inproj(x2d, mod3, g1, w_main, w_fg, b_fg, seq):
    t, d = x2d.shape
    n = w_main.shape[1]
    tm = min(1024, seq)
    tn = next(c for c in (1024, 512, 256, 128) if n % c == 0)
    return pl.pallas_call(
        functools.partial(_inproj_kernel, tiles_per_batch=seq // tm, sub=256),
        out_shape=(jax.ShapeDtypeStruct((t, n), BF16), jax.ShapeDtypeStruct((t, LANES), F32)),
        grid=(t // tm, n // tn),
        in_specs=[pl.BlockSpec((tm, d), lambda i, j: (i, 0)),
                  pl.BlockSpec((1, mod3.shape[1], d), lambda i, j: (i // (seq // tm), 0, 0)),
                  pl.BlockSpec((1, d), lambda i, j: (0, 0)),
                  pl.BlockSpec((d, tn), lambda i, j: (0, j)),
                  pl.BlockSpec((d, LANES), lambda i, j: (0, 0)),
                  pl.BlockSpec((1, LANES), lambda i, j: (0, 0))],
        out_specs=(pl.BlockSpec((tm, tn), lambda i, j: (i, j)),
                   pl.BlockSpec((tm, LANES), lambda i, j: (i, 0))),
        scratch_shapes=[pltpu.VMEM((tm, d), BF16), pltpu.VMEM((1, LANES), F32)],
        compiler_params=_params("arbitrary", "arbitrary"),
        name="inproj",
    )(x2d, mod3, g1, w_main, w_fg, b_fg)


N_BIAS = 3
LOG2E = 1.4426950408889634


def _attn_kernel(q_ref, k_ref, v_ref, cum_ref, o_ref, kaug_sc, vt_sc, qt_sc, sa_sc, sb_sc, m_sc, l_sc, acc_sc,
                 *, blk, n_kv):
    h = pl.program_id(1)
    i = pl.program_id(2)

    @pl.when(i == 0)
    def _():
        lane = lax.broadcasted_iota(jnp.int32, (blk, LANES), 1)

        def prep(j, carry):
            start = pl.multiple_of(j * blk, blk)
            cum = cum_ref[0, pl.ds(start, blk), :]
            rest = jnp.sum(jnp.where(lane == h, cum, 0.0), axis=1, keepdims=True) * LOG2E
            bias = jnp.zeros((blk, LANES), F32)
            for piece in range(N_BIAS):
                part = rest.astype(BF16).astype(F32)
                bias = jnp.where(lane == piece, part, bias)
                rest = rest - part
            kaug_sc[j, :, 0:HEAD_DIM] = k_ref[0, pl.ds(start, blk), :]
            kaug_sc[j, :, HEAD_DIM:2 * HEAD_DIM] = bias.astype(BF16)
            vt_sc[j] = v_ref[0, pl.ds(start, blk), :].astype(F32).T.astype(BF16)
            return carry

        lax.fori_loop(0, n_kv, prep, 0)

    qs = q_ref[0].astype(F32) * (HEAD_DIM ** -0.5 * LOG2E)
    qt_sc[0:HEAD_DIM, :] = qs.T.astype(BF16)
    row = lax.broadcasted_iota(jnp.int32, (HEAD_DIM, blk), 0)
    qt_sc[HEAD_DIM:2 * HEAD_DIM, :] = jnp.where(row < N_BIAS, -1.0, 0.0).astype(BF16)
    m_sc[...] = jnp.full_like(m_sc, -jnp.inf)
    l_sc[...] = jnp.zeros_like(l_sc)
    acc_sc[...] = jnp.zeros_like(acc_sc)

    def scores(j):
        return jnp.dot(kaug_sc[j], qt_sc[...], preferred_element_type=F32)

    def update(j, s):
        m_prev = m_sc[...]
        m_new = jnp.maximum(m_prev, jnp.max(s, axis=0, keepdims=True))
        alpha = jnp.exp2(m_prev - m_new)
        p = jnp.exp2(s - m_new)
        l_sc[...] = alpha * l_sc[...] + jnp.sum(p, axis=0, keepdims=True)
        acc_sc[...] = alpha * acc_sc[...] + jnp.dot(vt_sc[j], p.astype(BF16), preferred_element_type=F32)
        m_sc[...] = m_new

    def causal(s):
        key = lax.broadcasted_iota(jnp.int32, (blk, blk), 0)
        qry = lax.broadcasted_iota(jnp.int32, (blk, blk), 1)
        return jnp.where(key <= qry, s, -jnp.inf)

    sa_sc[...] = scores(0)

    def pair_at(j):
        sb_sc[...] = scores(j + 1)
        update(j, sa_sc[...])
        sa_sc[...] = scores(j + 2)
        update(j + 1, sb_sc[...])

    def quad(jj, carry):
        pair_at(4 * jj)
        pair_at(4 * jj + 2)
        return carry

    def pair(jj, carry):
        pair_at(4 * (i // 4) + 2 * jj)
        return carry

    lax.fori_loop(0, i // 4, quad, 0)
    lax.fori_loop(0, (i % 4) // 2, pair, 0)

    @pl.when(i % 2 == 0)
    def _():
        update(i, causal(sa_sc[...]))

    @pl.when(i % 2 == 1)
    def _():
        sb_sc[...] = scores(i)
        update(i - 1, sa_sc[...])
        update(i, causal(sb_sc[...]))

    o_ref[0] = (acc_sc[...] / l_sc[...]).T.astype(BF16)


def _fox_attention(proj3, cum3, n_heads):
    n_batch, seq, _ = proj3.shape
    blk = min(512, seq)
    n_kv = seq // blk
    return pl.pallas_call(
        functools.partial(_attn_kernel, blk=blk, n_kv=n_kv),
        out_shape=jax.ShapeDtypeStruct((n_batch, seq, n_heads * HEAD_DIM), BF16),
        grid=(n_batch, n_heads, n_kv),
        in_specs=[pl.BlockSpec((1, blk, HEAD_DIM), lambda b, h, i: (b, i, h)),
                  pl.BlockSpec((1, seq, HEAD_DIM), lambda b, h, i: (b, 0, n_heads + h)),
                  pl.BlockSpec((1, seq, HEAD_DIM), lambda b, h, i: (b, 0, 2 * n_heads + h)),
                  pl.BlockSpec((1, seq, LANES), lambda b, h, i: (b, 0, 0))],
        out_specs=pl.BlockSpec((1, blk, HEAD_DIM), lambda b, h, i: (b, i, h)),
        scratch_shapes=[pltpu.VMEM((n_kv, blk, 2 * HEAD_DIM), BF16),
                        pltpu.VMEM((n_kv, HEAD_DIM, blk), BF16),
                        pltpu.VMEM((2 * HEAD_DIM, blk), BF16),
                        pltpu.VMEM((blk, blk), F32), pltpu.VMEM((blk, blk), F32),
                        pltpu.VMEM((1, blk), F32), pltpu.VMEM((1, blk), F32),
                        pltpu.VMEM((HEAD_DIM, blk), F32)],
        compiler_params=_params("arbitrary", "arbitrary", "arbitrary"),
        name="fox_attention",
    )(proj3, proj3, proj3, cum3)


def _hgrn_kernel(q_ref, f_ref, i_ref, g_ref, lbl_ref, gn_ref, o_ref, st_sc, *, rows, sub, layer):
    @pl.when(pl.program_id(2) == 0)
    def _():
        st_sc[...] = jnp.zeros_like(st_sc)

    lbl = lbl_ref[0]
    e = jnp.exp(lbl - jnp.max(lbl, axis=0, keepdims=True))
    lb = jnp.sum(e[0:layer + 1, :], axis=0, keepdims=True) / jnp.sum(e, axis=0, keepdims=True)
    f = lb + (1.0 - lb) * _sigmoid(f_ref[0].astype(F32))
    logf = jnp.log(f)
    kk = 1.0 - f
    qf = q_ref[0].astype(F32)
    qq = qf * _sigmoid(qf)

    n_ch = sub // HG_CHUNK
    r = lax.broadcasted_iota(jnp.int32, (sub, sub), 0)
    c = lax.broadcasted_iota(jnp.int32, (sub, sub), 1)
    within = (r >= c) & (r // HG_CHUNK == c // HG_CHUNK)
    tri = within.astype(BF16)
    rw = lax.broadcasted_iota(jnp.int32, (sub, n_ch * HEAD_DIM), 0)
    cw = lax.broadcasted_iota(jnp.int32, (sub, n_ch * HEAD_DIM), 1)
    own_block = rw // HG_CHUNK == cw // HEAD_DIM

    zero = jnp.zeros((), BF16)
    n_sub = rows // sub
    b_all, b_last_all = [], []
    for s in range(n_sub):
        lf = logf[s * sub:(s + 1) * sub, :]
        hi = lf.astype(BF16)
        rest = lf - hi.astype(F32)
        mid = rest.astype(BF16)
        low = (rest - mid.astype(F32)).astype(BF16)
        two = jnp.dot(tri, jnp.concatenate([hi, mid], axis=1), preferred_element_type=F32)
        b = two[:, :HEAD_DIM] + two[:, HEAD_DIM:] + jnp.dot(tri, low, preferred_element_type=F32)
        b_all.append(b)
        b_last_all.append([b[(n + 1) * HG_CHUNK - 1:(n + 1) * HG_CHUNK, :] for n in range(n_ch)])

    q_dec_all, o_all, upd_all = [], [], []
    for s in range(n_sub):
        rs = slice(s * sub, (s + 1) * sub)
        b = b_all[s]
        b_last_rows = jnp.concatenate([jnp.broadcast_to(bl, (HG_CHUNK, HEAD_DIM)) for bl in b_last_all[s]], axis=0)
        q_dec = (qq[rs, :] * jnp.exp(b)).astype(BF16)
        k_inv = (kk[rs, :] * jnp.exp(-b)).astype(BF16)
        k_tail = (kk[rs, :] * jnp.exp(b_last_rows - b)).astype(BF16)
        vv = i_ref[0, pl.ds(s * sub, sub), :]
        attn = lax.dot_general(q_dec, k_inv, NT_DIMS, preferred_element_type=F32)
        o_all.append(jnp.dot(jnp.where(within, attn, 0.0).astype(BF16), vv, preferred_element_type=F32))
        k_blocks = jnp.where(own_block, jnp.concatenate([k_tail] * n_ch, axis=1), zero)
        upd_all.append(lax.dot_general(vv, k_blocks, TN_DIMS, preferred_element_type=F32))
        q_dec_all.append(q_dec)

    st = st_sc[...]
    for s in range(n_sub):
        states = []
        for n in range(n_ch):
            states.append(st.astype(BF16))
            st = jnp.exp(b_last_all[s][n]) * st + upd_all[s][:, n * HEAD_DIM:(n + 1) * HEAD_DIM]
        q_blocks = jnp.where(own_block, jnp.concatenate([q_dec_all[s]] * n_ch, axis=1), zero)
        o = o_all[s] + lax.dot_general(q_blocks, jnp.concatenate(states, axis=1), NT_DIMS,
                                       preferred_element_type=F32)
        y = _rms(o) * gn_ref[0]
        gf = g_ref[0, pl.ds(s * sub, sub), :].astype(F32)
        o_ref[0, pl.ds(s * sub, sub), :] = (y * (gf * _sigmoid(gf))).astype(BF16)
    st_sc[...] = st


def _hgrn2(proj3, lb_logits3, g_hg3, n_heads, col0, layer):
    n_batch, seq, _ = proj3.shape
    rows = min(1024, seq)
    sub = min(256, rows)
    spec = lambda off: pl.BlockSpec((1, rows, HEAD_DIM), lambda b, h, r: (b, r, col0 + off * n_heads + h))
    return pl.pallas_call(
        functools.partial(_hgrn_kernel, rows=rows, sub=sub, layer=layer),
        out_shape=jax.ShapeDtypeStruct((n_batch, seq, n_heads * HEAD_DIM), BF16),
        grid=(n_batch, n_heads, seq // rows),
        in_specs=[spec(0), spec(1), spec(2), spec(3),
                  pl.BlockSpec((1, lb_logits3.shape[1], HEAD_DIM), lambda b, h, r: (h, 0, 0)),
                  pl.BlockSpec((1, 1, HEAD_DIM), lambda b, h, r: (h, 0, 0))],
        out_specs=pl.BlockSpec((1, rows, HEAD_DIM), lambda b, h, r: (b, r, h)),
        scratch_shapes=[pltpu.VMEM((HEAD_DIM, HEAD_DIM), F32)],
        compiler_params=_params("arbitrary", "arbitrary", "arbitrary"),
        name="hgrn2",
    )(proj3, proj3, proj3, proj3, lb_logits3, g_hg3)


def _pack_halves(v):
    n = v.shape[1] // 2
    lo = lax.bitcast_convert_type(v[:, :n].astype(BF16).astype(F32), jnp.uint32) >> 16
    hi = lax.bitcast_convert_type(v[:, n:].astype(BF16).astype(F32), jnp.uint32) & jnp.uint32(0xFFFF0000)
    return lo | hi


def _unpack_halves(p):
    lo = lax.bitcast_convert_type(p << 16, F32).astype(BF16)
    hi = lax.bitcast_convert_type(p & jnp.uint32(0xFFFF0000), F32).astype(BF16)
    return lo, hi


def _outproj_kernel(fox_ref, hg_ref, x_ref, mod_ref, gfox_ref, g2_ref, w_ref, wr_ref, br_ref,
                    x1_ref, h2_ref, lg_ref, *, n_exp, group):
    groups = [pl.ds(k * group, group) for k in range(x_ref.shape[0] // group)]
    mixed = []
    for rows in groups:
        fox = _rms(fox_ref[rows, :].astype(F32)) * gfox_ref[...]
        mixed.append(jnp.concatenate([fox.astype(BF16), hg_ref[rows, :]], axis=1))
    mix = [jnp.dot(m, w_ref[...], preferred_element_type=F32) for m in mixed]
    h2s = []
    for rows, m in zip(groups, mix):
        x1 = x_ref[rows, :] + mod_ref[0, 2:3, :] * m
        x1_ref[rows, :] = x1
        h2 = _rms(x1) * (g2_ref[...] * (1.0 + mod_ref[0, 4:5, :])) + mod_ref[0, 3:4, :]
        h2_ref[rows, :] = _pack_halves(h2)
        h2s.append(h2)
    for rows, h2 in zip(groups, h2s):
        h_hi = h2.astype(BF16)
        h_lo = (h2 - h_hi.astype(F32)).astype(BF16)
        part = jnp.dot(h_hi, wr_ref[...], preferred_element_type=F32)
        part = part + jnp.dot(h_lo, wr_ref[...], preferred_element_type=F32)
        logits = part + pltpu.roll(part, LANES - n_exp, axis=1)
        lg_ref[:, rows] = logits.T[0:n_exp, :] + br_ref[...]


def _outproj(fox2d, hg2d, x2d, mod3, g_fox, g2, w_out, w_router, b_router, seq):
    t, d = x2d.shape
    fox_w = fox2d.shape[1]
    n_exp = w_router.shape[1]
    assert 2 * n_exp <= LANES
    wr_hi = w_router.astype(BF16)
    wr_lo = (w_router - wr_hi.astype(F32)).astype(BF16)
    wr_cat = jnp.pad(jnp.concatenate([wr_hi, wr_lo], axis=1), ((0, 0), (0, LANES - 2 * n_exp)))
    tm = 512
    row = lambda i: (i, 0)
    const = lambda i: (0, 0)
    return pl.pallas_call(
        functools.partial(_outproj_kernel, n_exp=n_exp, group=256),
        out_shape=(jax.ShapeDtypeStruct((t, d), F32), jax.ShapeDtypeStruct((t, d // 2), jnp.uint32),
                   jax.ShapeDtypeStruct((n_exp, t), F32)),
        grid=(t // tm,),
        in_specs=[pl.BlockSpec((tm, fox_w), row),
                  pl.BlockSpec((tm, hg2d.shape[1]), row),
                  pl.BlockSpec((tm, d), row),
                  pl.BlockSpec((1, mod3.shape[1], d), lambda i: (i // (seq // tm), 0, 0)),
                  pl.BlockSpec((1, fox_w), const),
                  pl.BlockSpec((1, d), const),
                  pl.BlockSpec(w_out.shape, const),
                  pl.BlockSpec((d, LANES), const),
                  pl.BlockSpec((n_exp, 1), const)],
        out_specs=(pl.BlockSpec((tm, d), row), pl.BlockSpec((tm, d // 2), row),
                   pl.BlockSpec((n_exp, tm), lambda i: (0, i))),
        compiler_params=_params("arbitrary"),
        name="outproj_router",
    )(fox2d, hg2d, x2d, mod3, g_fox, g2, w_out, wr_cat, b_router)


def _route_kernel(lg_ref, pos_ref, gate_ref, cnt_ref, cnt_sc, run_sc, *, n_exp, tb):
    phase = pl.program_id(0)
    i = pl.program_id(1)

    @pl.when((phase == 0) & (i == 0))
    def _():
        cnt_sc[...] = jnp.zeros_like(cnt_sc)
        run_sc[...] = jnp.zeros_like(run_sc)

    logits = lg_ref[...]
    eidx = lax.broadcasted_iota(jnp.int32, (n_exp, tb), 0).astype(F32)
    work = logits
    vals, hots = [], []
    for _ in range(TOP_K):
        m = jnp.max(work, axis=0, keepdims=True)
        first = jnp.min(jnp.where(work == m, eidx, float(n_exp)), axis=0, keepdims=True)
        hot = eidx == first
        vals.append(m)
        hots.append(hot)
        work = jnp.where(hot, -jnp.inf, work)
    sel = hots[0] | hots[1] | hots[2] | hots[3]
    self32 = sel.astype(F32)

    @pl.when(phase == 0)
    def _():
        cnt_sc[...] += jnp.sum(self32, axis=1, keepdims=True)

    @pl.when(phase == 1)
    def _():
        cnt = cnt_sc[...]
        padded = jnp.ceil(cnt / MOE_BLOCK) * MOE_BLOCK
        r = lax.broadcasted_iota(jnp.int32, (n_exp, n_exp), 0)
        c = lax.broadcasted_iota(jnp.int32, (n_exp, n_exp), 1)
        strict = (c < r).astype(F32)
        pstart = jnp.dot(strict, jnp.broadcast_to(padded, (n_exp, LANES)), preferred_element_type=F32,
                         precision=HIGHEST)[:, 0:1]
        tr = lax.broadcasted_iota(jnp.int32, (tb, tb), 0)
        tc = lax.broadcasted_iota(jnp.int32, (tb, tb), 1)
        upper = (tr < tc).astype(BF16)
        rank = jnp.dot(sel.astype(BF16), upper, preferred_element_type=F32)
        base = pstart + run_sc[...] + rank
        exps = [jnp.exp(v - vals[0]) for v in vals]
        denom = exps[0] + exps[1] + exps[2] + exps[3]
        for k in range(TOP_K):
            pos_k = jnp.sum(jnp.where(hots[k], base, 0.0), axis=0, keepdims=True)
            pos_ref[pl.ds(k, 1), :] = pos_k.astype(jnp.int32)
            gate_ref[pl.ds(k, 1), :] = exps[k] / denom
        run_sc[...] += jnp.sum(self32, axis=1, keepdims=True)
        cnt_ref[...] = cnt


def _route(logits_t):
    n_exp, t = logits_t.shape
    tb = min(512, t)
    return pl.pallas_call(
        functools.partial(_route_kernel, n_exp=n_exp, tb=tb),
        out_shape=(jax.ShapeDtypeStruct((TOP_K, t), jnp.int32), jax.ShapeDtypeStruct((TOP_K, t), F32),
                   jax.ShapeDtypeStruct((n_exp, 1), F32)),
        grid=(2, t // tb),
        in_specs=[pl.BlockSpec((n_exp, tb), lambda p, i: (0, i))],
        out_specs=(pl.BlockSpec((TOP_K, tb), lambda p, i: (0, i * p)),
                   pl.BlockSpec((TOP_K, tb), lambda p, i: (0, i * p)),
                   pl.BlockSpec((n_exp, 1), lambda p, i: (0, 0))),
        scratch_shapes=[pltpu.VMEM((n_exp, 1), F32), pltpu.VMEM((n_exp, 1), F32)],
        compiler_params=_params("arbitrary", "arbitrary"),
        name="route_topk",
    )(logits_t)


N_DMA_QUEUES = 2


def _row_copy(src_hbm, row, dst_vmem, slot, sem):
    return pltpu.make_async_copy(src_hbm.at[pl.ds(row, 1), :], dst_vmem.at[pl.ds(slot, 1), :], sem)


def _gather_rows(src_hbm, idx_ref, idx0, n_rows, dst_vmem, sem):
    for r in range(n_rows):
        _row_copy(src_hbm, idx_ref[0, 0, idx0 + r], dst_vmem, r, sem).start(priority=r % N_DMA_QUEUES)


def _wait_rows(src_hbm, n_rows, dst_vmem, sem):
    pltpu.make_async_copy(src_hbm.at[pl.ds(0, n_rows), :], dst_vmem, sem).wait()


def _dispatch_kernel(pad0_ref, padn_ref, nused_ref, pos_ref, h_ref, xs_hbm, zeros, sem, zsem,
                     *, tb, n_exp, n_blocks):
    @pl.when(pl.program_id(0) == 0)
    def _():
        zeros[...] = jnp.zeros_like(zeros)

        def each_fill(fn):
            def per_expert(e, carry):
                def per_row(r, inner):
                    fn(pltpu.make_async_copy(zeros.at[pl.ds(0, 1), :], xs_hbm.at[pl.ds(pad0_ref[e] + r, 1), :], zsem))
                    return inner

                lax.fori_loop(0, padn_ref[e], per_row, 0)
                return carry

            lax.fori_loop(0, n_exp, per_expert, 0)

            def per_block(b, carry):
                row0 = pl.multiple_of(b * MOE_BLOCK, MOE_BLOCK)
                fn(pltpu.make_async_copy(zeros, xs_hbm.at[pl.ds(row0, MOE_BLOCK), :], zsem))
                return carry

            lax.fori_loop(nused_ref[0], n_blocks, per_block, 0)

        each_fill(lambda cp: cp.start())
        each_fill(lambda cp: cp.wait())

    for r in range(tb):
        for k in range(TOP_K):
            pltpu.make_async_copy(h_ref.at[pl.ds(r, 1), :], xs_hbm.at[pl.ds(pos_ref[0, 0, k * tb + r], 1), :],
                                  sem).start(priority=k % N_DMA_QUEUES)
    for k in range(TOP_K):
        pltpu.make_async_copy(h_ref, xs_hbm.at[pl.ds(0, tb), :], sem).wait()


def _dispatch(h2p, pos3, pad_start, pad_count, n_used, n_rows):
    t, dp = h2p.shape
    n_tiles = pos3.shape[0]
    tb = pos3.shape[2] // TOP_K
    return pl.pallas_call(
        functools.partial(_dispatch_kernel, tb=tb, n_exp=pad_start.shape[0], n_blocks=n_rows // MOE_BLOCK),
        out_shape=jax.ShapeDtypeStruct((n_rows, dp), h2p.dtype),
        grid_spec=pltpu.PrefetchScalarGridSpec(
            num_scalar_prefetch=3,
            grid=(n_tiles,),
            in_specs=[pl.BlockSpec((1, 1, TOP_K * tb), lambda i, p0, pn, nu: (i, 0, 0), memory_space=pltpu.SMEM),
                      pl.BlockSpec((tb, dp), lambda i, p0, pn, nu: (i, 0))],
            out_specs=pl.BlockSpec(memory_space=pl.ANY),
            scratch_shapes=[pltpu.VMEM((MOE_BLOCK, dp), h2p.dtype), pltpu.SemaphoreType.DMA(()),
                            pltpu.SemaphoreType.DMA(())]),
        compiler_params=_params("arbitrary"),
        name="moe_dispatch",
    )(pad_start, pad_count, n_used, pos3, h2p)


def _new_expert(be_ref, b):
    return (b == 0) | (be_ref[b] != be_ref[jnp.maximum(b - 1, 0)])


W_PIECES = 4


def _weight_copies(w_hbm, e, col0, stage, sem):
    rows = stage.shape[0] // W_PIECES
    return [pltpu.make_async_copy(w_hbm.at[e, pl.ds(i * rows, rows), pl.ds(col0, stage.shape[1])],
                                  stage.at[pl.ds(i * rows, rows), :], sem) for i in range(W_PIECES)]


def _convert_rows(src_f32, dst_bf16):
    rows = 16

    def body(i, carry):
        r0 = pl.multiple_of(i * rows, rows)
        dst_bf16[pl.ds(r0, rows), :] = src_f32[pl.ds(r0, rows), :].astype(BF16)
        return carry

    lax.fori_loop(0, src_f32.shape[0] // rows, body, 0, unroll=4)


def _gateup_kernel(be_ref, nxt_ref, last_ref, nused_ref, x_ref, w_hbm, bg_ref, bl_ref, o_ref,
                   stg_g, stg_l, wg_sc, wl_sc, sem, *, nf, tf):
    c = pl.program_id(0)
    b = pl.program_id(1)

    def slab(e, chunk):
        col = pl.multiple_of(chunk * tf, tf)
        return _weight_copies(w_hbm, e, col, stg_g, sem) + _weight_copies(w_hbm, e, col + nf * tf, stg_l, sem)

    @pl.when((c == 0) & (b == 0))
    def _():
        for cp in slab(be_ref[0], 0):
            cp.start()

    @pl.when(_new_expert(be_ref, b))
    def _():
        for cp in slab(0, 0):
            cp.wait()
        _convert_rows(stg_g, wg_sc)
        _convert_rows(stg_l, wl_sc)
        last_run = last_ref[b] == 1

        @pl.when(jnp.logical_not(last_run))
        def _():
            for cp in slab(nxt_ref[b], c):
                cp.start()

        @pl.when(last_run & (c + 1 < nf))
        def _():
            for cp in slab(be_ref[0], c + 1):
                cp.start()

    @pl.when(b < nused_ref[0])
    def _():
        x = jnp.concatenate(_unpack_halves(x_ref[...]), axis=1)
        glu = jnp.dot(x, wg_sc[...], preferred_element_type=F32) + bg_ref[0]
        lin = jnp.dot(x, wl_sc[...], preferred_element_type=F32) + bl_ref[0]
        glu = jnp.minimum(glu, SWIGLU_LIMIT)
        lin = jnp.clip(lin, -SWIGLU_LIMIT, SWIGLU_LIMIT)
        o_ref[...] = (glu * _sigmoid(SWIGLU_ALPHA * glu) * (lin + 1.0)).astype(BF16)

    @pl.when(b >= nused_ref[0])
    def _():
        o_ref[...] = jnp.zeros_like(o_ref)


def _gateup(runs, n_used, xs, w_gu, b_gu3):
    n_rows, dp = xs.shape
    d = 2 * dp
    n_blocks = n_rows // MOE_BLOCK
    d_ff = w_gu.shape[2] // 2
    tf = min(1024, d_ff)
    nf = d_ff // tf
    return pl.pallas_call(
        functools.partial(_gateup_kernel, nf=nf, tf=tf),
        out_shape=jax.ShapeDtypeStruct((n_rows, d_ff), BF16),
        grid_spec=pltpu.PrefetchScalarGridSpec(
            num_scalar_prefetch=4,
            grid=(nf, n_blocks),
            in_specs=[pl.BlockSpec((MOE_BLOCK, dp), lambda c, b, be, nx, la, nu: (jnp.minimum(b, nu[0] - 1), 0)),
                      pl.BlockSpec(memory_space=pl.ANY),
                      pl.BlockSpec((1, 1, tf), lambda c, b, be, nx, la, nu: (be[b], 0, c)),
                      pl.BlockSpec((1, 1, tf), lambda c, b, be, nx, la, nu: (be[b], 0, nf + c))],
            out_specs=pl.BlockSpec((MOE_BLOCK, tf), lambda c, b, be, nx, la, nu: (b, c)),
            scratch_shapes=[pltpu.VMEM((d, tf), F32), pltpu.VMEM((d, tf), F32),
                            pltpu.VMEM((d, tf), BF16), pltpu.VMEM((d, tf), BF16),
                            pltpu.SemaphoreType.DMA(())]),
        compiler_params=_params("arbitrary", "arbitrary"),
        name="moe_gateup",
    )(*runs, n_used, xs, w_gu, b_gu3, b_gu3)


def _down_kernel(be_ref, nxt_ref, last_ref, nused_ref, a_ref, w_hbm, b_ref, o_ref, stg, w_sc, sem):
    b = pl.program_id(0)

    @pl.when(b == 0)
    def _():
        for cp in _weight_copies(w_hbm, be_ref[0], 0, stg, sem):
            cp.start()

    @pl.when(_new_expert(be_ref, b))
    def _():
        for cp in _weight_copies(w_hbm, 0, 0, stg, sem):
            cp.wait()
        _convert_rows(stg, w_sc)

        @pl.when(last_ref[b] == 0)
        def _():
            for cp in _weight_copies(w_hbm, nxt_ref[b], 0, stg, sem):
                cp.start()

    @pl.when(b < nused_ref[0])
    def _():
        y = jnp.dot(a_ref[...], w_sc[...], preferred_element_type=F32) + b_ref[0]
        o_ref[...] = _pack_halves(y)

    @pl.when(b >= nused_ref[0])
    def _():
        o_ref[...] = jnp.zeros_like(o_ref)


def _down(runs, n_used, act, w_down, b_down3):
    n_rows, d_ff = act.shape
    n_blocks = n_rows // MOE_BLOCK
    d = w_down.shape[2]
    return pl.pallas_call(
        _down_kernel,
        out_shape=jax.ShapeDtypeStruct((n_rows, d // 2), jnp.uint32),
        grid_spec=pltpu.PrefetchScalarGridSpec(
            num_scalar_prefetch=4,
            grid=(n_blocks,),
            in_specs=[pl.BlockSpec((MOE_BLOCK, d_ff), lambda b, be, nx, la, nu: (b, 0)),
                      pl.BlockSpec(memory_space=pl.ANY),
                      pl.BlockSpec((1, 1, d), lambda b, be, nx, la, nu: (be[b], 0, 0))],
            out_specs=pl.BlockSpec((MOE_BLOCK, d // 2), lambda b, be, nx, la, nu: (b, 0)),
            scratch_shapes=[pltpu.VMEM((d_ff, d), F32), pltpu.VMEM((d_ff, d), BF16),
                            pltpu.SemaphoreType.DMA(())]),
        compiler_params=_params("arbitrary"),
        name="moe_down",
    )(*runs, n_used, act, w_down, b_down3)


COMBINE_TILES = 4


def _combine_kernel(pos_ref, pos_next_ref, y_hbm, gate_ref, x1_ref, mod_ref, gfin_ref, o_ref, b0, b1, b2, b3, sem,
                    *, tb, n_steps, last_layer):
    i = pl.program_id(0)
    bufs = (b0, b1, b2, b3)
    n_idx = TOP_K * tb

    def request(idx_ref, u_src, u_dst):
        _gather_rows(y_hbm, idx_ref, u_src * n_idx, n_idx, bufs[u_dst], sem.at[u_dst])

    def combine(u):
        rows = pl.ds(u * tb, tb)
        gates = gate_ref[rows, :]
        moe_lo = moe_hi = None
        for k in range(TOP_K):
            lo, hi = _unpack_halves(bufs[u][pl.ds(k * tb, tb), :])
            g = gates[:, k:k + 1]
            moe_lo = g * lo.astype(F32) if k == 0 else moe_lo + g * lo.astype(F32)
            moe_hi = g * hi.astype(F32) if k == 0 else moe_hi + g * hi.astype(F32)
        moe = jnp.concatenate([moe_lo, moe_hi], axis=1)
        x2 = x1_ref[rows, :] + mod_ref[0, 5:6, :] * moe
        o_ref[rows, :] = _rms(x2) * gfin_ref[...] if last_layer else x2

    @pl.when(i == 0)
    def _():
        request(pos_ref, 0, 0)
        request(pos_ref, 1, 1)

    for u in range(COMBINE_TILES):
        _wait_rows(y_hbm, n_idx, bufs[u], sem.at[u])
        if u + 2 < COMBINE_TILES:
            request(pos_ref, u + 2, u + 2)
        else:
            request(pos_next_ref, u + 2 - COMBINE_TILES, u + 2 - COMBINE_TILES)
        combine(u)

    @pl.when(i == n_steps - 1)
    def _():
        _wait_rows(y_hbm, n_idx, b0, sem.at[0])
        _wait_rows(y_hbm, n_idx, b1, sem.at[1])


def _combine(pos3, y_buf, gates, x1, mod3, g_final, seq, last_layer):
    t, d = x1.shape
    n_steps = pos3.shape[0]
    tb = pos3.shape[2] // (COMBINE_TILES * TOP_K)
    rows = COMBINE_TILES * tb
    idx_spec = lambda step: pl.BlockSpec((1, 1, pos3.shape[2]), lambda i: (jnp.minimum(i + step, n_steps - 1), 0, 0),
                                         memory_space=pltpu.SMEM)
    buf = pltpu.VMEM((TOP_K * tb, y_buf.shape[1]), y_buf.dtype)
    return pl.pallas_call(
        functools.partial(_combine_kernel, tb=tb, n_steps=n_steps, last_layer=last_layer),
        out_shape=jax.ShapeDtypeStruct((t, d), F32),
        grid=(n_steps,),
        in_specs=[idx_spec(0), idx_spec(1),
                  pl.BlockSpec(memory_space=pl.ANY),
                  pl.BlockSpec((rows, TOP_K), lambda i: (i, 0)),
                  pl.BlockSpec((rows, d), lambda i: (i, 0)),
                  pl.BlockSpec((1, mod3.shape[1], d), lambda i: (i // (seq // rows), 0, 0)),
                  pl.BlockSpec((1, d), lambda i: (0, 0))],
        out_specs=pl.BlockSpec((rows, d), lambda i: (i, 0)),
        scratch_shapes=[buf] * COMBINE_TILES + [pltpu.SemaphoreType.DMA((COMBINE_TILES,))],
        compiler_params=_params("arbitrary"),
        name="moe_combine",
    )(pos3, pos3, y_buf, gates, x1, mod3, g_final)


def kernel(x, c, w_ada, b_ada, g_norm1, g_norm2, w_in, b_fgate, g_fox_out, lb_logits, g_hg_out,
           w_out, w_router, b_router, w_gu, b_gu, w_down, b_down, g_final):
    n_batch, seq, d = x.shape
    t = n_batch * seq
    depth = w_ada.shape[0]
    fox_heads = b_fgate.shape[1]
    fox_w = g_fox_out.shape[1]
    hg_w = g_hg_out.shape[1]
    hg_heads = hg_w // HEAD_DIM
    n_exp = w_router.shape[2]
    assert fox_w == fox_heads * HEAD_DIM and fox_heads <= LANES
    n_blocks = -(-t * TOP_K // MOE_BLOCK) + n_exp
    n_mod = w_ada.shape[2] // d

    x2d = x.reshape(t, d)
    for l in range(depth):
        mod3 = _ada_mod(c, w_ada[l], b_ada[l]).reshape(n_batch, n_mod, d)

        w_l = w_in[l].astype(BF16)
        w_main = jnp.concatenate([w_l[:, :3 * fox_w], w_l[:, 3 * fox_w + fox_heads:]], axis=1)
        w_fg = jnp.pad(w_l[:, 3 * fox_w:3 * fox_w + fox_heads], ((0, 0), (0, LANES - fox_heads)))
        b_fg = jnp.pad(b_fgate[l], (0, LANES - fox_heads)).reshape(1, LANES)
        proj, cum = _inproj(x2d, mod3, g_norm1[l].reshape(1, d), w_main, w_fg, b_fg, seq)
        proj3 = proj.reshape(n_batch, seq, -1)

        fox = _fox_attention(proj3, cum.reshape(n_batch, seq, LANES), fox_heads)

        lb3 = lb_logits.reshape(lb_logits.shape[0], hg_heads, HEAD_DIM).transpose(1, 0, 2)
        hg = _hgrn2(proj3, lb3, g_hg_out[l].reshape(hg_heads, 1, HEAD_DIM), hg_heads, 3 * fox_heads, l)

        x1, h2p, logits_t = _outproj(
            fox.reshape(t, fox_w), hg.reshape(t, hg_w), x2d, mod3, g_fox_out[l].reshape(1, fox_w),
            g_norm2[l].reshape(1, d), w_out[l].astype(BF16), w_router[l], b_router[l].reshape(n_exp, 1), seq)

        pos_t, gates_t, counts = _route(logits_t)

        cnt = counts[:, 0].astype(jnp.int32)
        padded = (cnt + MOE_BLOCK - 1) // MOE_BLOCK * MOE_BLOCK
        padded_end = jnp.cumsum(padded)
        block_row0 = jnp.arange(n_blocks, dtype=jnp.int32) * MOE_BLOCK
        block_e = jnp.sum((block_row0[:, None] >= padded_end[None, :]).astype(jnp.int32), axis=1)
        n_used = (padded_end[-1:] // MOE_BLOCK).astype(jnp.int32)
        block_id = jnp.arange(n_blocks, dtype=jnp.int32)
        block_e = jnp.minimum(block_e, jnp.sum(jnp.where(block_id == n_used - 1, block_e, 0)))
        change_at = jnp.where(block_e != jnp.roll(block_e, 1), block_id, n_blocks).at[0].set(n_blocks)
        next_change = jnp.flip(lax.cummin(jnp.flip(jnp.roll(change_at, -1).at[-1].set(n_blocks))))
        last_run = (next_change >= n_blocks).astype(jnp.int32)
        next_e = block_e[jnp.minimum(next_change, n_blocks - 1)]
        runs = (block_e, next_e, last_run)

        def tile_major(tb):
            return pos_t.reshape(TOP_K, t // tb, tb).transpose(1, 0, 2).reshape(t // tb, 1, TOP_K * tb)

        pad_count = padded - cnt
        xs = _dispatch(h2p, tile_major(min(256, seq)), padded_end - pad_count, pad_count, n_used,
                       n_blocks * MOE_BLOCK)
        act = _gateup(runs, n_used, xs, w_gu[l], b_gu[l].reshape(n_exp, 1, -1))
        y_buf = _down(runs, n_used, act, w_down[l], b_down[l].reshape(n_exp, 1, d))
        tb = min(128, seq // COMBINE_TILES)
        pos_steps = tile_major(tb).reshape(t // (COMBINE_TILES * tb), 1, COMBINE_TILES * TOP_K * tb)
        x2d = _combine(pos_steps, y_buf, gates_t.T, x1, mod3, g_final.reshape(1, d), seq, l == depth - 1)
    return x2d.reshape(n_batch, seq, d)
```

```python
import functools

import jax
import jax.numpy as jnp
from jax import lax
from jax.experimental import pallas as pl
from jax.experimental.pallas import tpu as pltpu

HEAD_DIM = 128
TOP_K = 4
MOE_BLOCK = 256
HG_CHUNK = 64
NORM_EPS = 1e-6
SWIGLU_ALPHA = 1.702
SWIGLU_LIMIT = 7.0
LANES = 128
VMEM_LIMIT = 56 * 1024 * 1024

F32 = jnp.float32
BF16 = jnp.bfloat16
HIGHEST = lax.Precision.HIGHEST
NT_DIMS = (((1,), (1,)), ((), ()))
TN_DIMS = (((0,), (0,)), ((), ()))


def _params(*semantics):
    return pltpu.CompilerParams(dimension_semantics=semantics, vmem_limit_bytes=VMEM_LIMIT)


def _sigmoid(z):
    return 1.0 / (1.0 + jnp.exp(-z))


def _rms(v):
    return v * lax.rsqrt(jnp.mean(v * v, axis=-1, keepdims=True) + NORM_EPS)


def _ada_kernel(ct_ref, w_ref, b_ref, o_ref, *, n_batch, d_model):
    ct = ct_ref[...]
    cond = ct * _sigmoid(ct)
    rows = 256
    for b in range(n_batch):
        acc = jnp.zeros((8, w_ref.shape[1]), F32)
        for i in range(d_model // rows):
            w3 = w_ref[pl.ds(i * rows, rows), :].reshape(rows // 8, 8, -1)
            c3 = cond[i * rows:(i + 1) * rows, b:b + 1].reshape(rows // 8, 8, 1)
            acc = acc + jnp.sum(w3 * c3, axis=0)
        o_ref[pl.ds(b, 1), :] = jnp.sum(acc, axis=0, keepdims=True) + b_ref[...]


def _ada_mod(c, w_ada, b_ada):
    n_batch, d = c.shape
    n = w_ada.shape[1]
    tn = 1024
    return pl.pallas_call(
        functools.partial(_ada_kernel, n_batch=n_batch, d_model=d),
        out_shape=jax.ShapeDtypeStruct((n_batch, n), F32),
        grid=(n // tn,),
        in_specs=[pl.BlockSpec((d, n_batch), lambda j: (0, 0)),
                  pl.BlockSpec((d, tn), lambda j: (0, j)),
                  pl.BlockSpec((1, tn), lambda j: (0, j))],
        out_specs=pl.BlockSpec((n_batch, tn), lambda j: (0, j)),
        compiler_params=_params("arbitrary"),
        name="ada_mod",
    )(c.T, w_ada, b_ada.reshape(1, n))


def _inproj_kernel(x_ref, mod_ref, g_ref, w_ref, wf_ref, bf_ref, proj_ref, cum_ref, h_sc, carry_sc,
                   *, tiles_per_batch, sub):
    i = pl.program_id(0)
    j = pl.program_id(1)

    @pl.when(j == 0)
    def _():
        x = x_ref[...]
        h = _rms(x) * (g_ref[...] * (1.0 + mod_ref[0, 1:2, :])) + mod_ref[0, 0:1, :]
        hb = h.astype(BF16)
        h_sc[...] = hb
        z = jnp.dot(hb, wf_ref[...], preferred_element_type=F32) + bf_ref[...]
        logf = jnp.minimum(z, 0.0) - jnp.log(1.0 + jnp.exp(-jnp.abs(z)))

        @pl.when(i % tiles_per_batch == 0)
        def _():
            carry_sc[...] = jnp.zeros_like(carry_sc)

        r = lax.broadcasted_iota(jnp.int32, (sub, sub), 0)
        c = lax.broadcasted_iota(jnp.int32, (sub, sub), 1)
        tri = (r >= c).astype(BF16)
        carry = carry_sc[...]
        for s in range(x.shape[0] // sub):
            blk = logf[s * sub:(s + 1) * sub, :]
            hi = blk.astype(BF16)
            rest = blk - hi.astype(F32)
            mid = rest.astype(BF16)
            low = (rest - mid.astype(F32)).astype(BF16)
            two = jnp.dot(tri, jnp.concatenate([hi, mid], axis=1), preferred_element_type=F32)
            cs = two[:, :LANES] + two[:, LANES:] + jnp.dot(tri, low, preferred_element_type=F32) + carry
            cum_ref[pl.ds(s * sub, sub), :] = cs
            carry = cs[sub - 1:sub, :]
        carry_sc[...] = carry

    proj_ref[...] = jnp.dot(h_sc[...], w_ref[...], preferred_element_type=F32).astype(BF16)


def _inproj(x2d, mod3, g1, w_main, w_fg, b_fg, seq):
    t, d = x2d.shape
    n = w_main.shape[1]
    tm = min(1024, seq)
    tn = next(c for c in (1024, 512, 256, 128) if n % c == 0)
    return pl.pallas_call(
        functools.partial(_inproj_kernel, tiles_per_batch=seq // tm, sub=256),
        out_shape=(jax.ShapeDtypeStruct((t, n), BF16), jax.ShapeDtypeStruct((t, LANES), F32)),
        grid=(t // tm, n // tn),
        in_specs=[pl.BlockSpec((tm, d), lambda i, j: (i, 0)),
                  pl.BlockSpec((1, mod3.shape[1], d), lambda i, j: (i // (seq // tm), 0, 0)),
                  pl.BlockSpec((1, d), lambda i, j: (0, 0)),
                  pl.BlockSpec((d, tn), lambda i, j: (0, j)),
                  pl.BlockSpec((d, LANES), lambda i, j: (0, 0)),
                  pl.BlockSpec((1, LANES), lambda i, j: (0, 0))],
        out_specs=(pl.BlockSpec((tm, tn), lambda i, j: (i, j)),
                   pl.BlockSpec((tm, LANES), lambda i, j: (i, 0))),
        scratch_shapes=[pltpu.VMEM((tm, d), BF16), pltpu.VMEM((1, LANES), F32)],
        compiler_params=_params("arbitrary", "arbitrary"),
        name="inproj",
    )(x2d, mod3, g1, w_main, w_fg, b_fg)


N_BIAS = 3
LOG2E = 1.4426950408889634


def _attn_kernel(q_ref, k_ref, v_ref, cum_ref, o_ref, kaug_sc, vt_sc, qt_sc, sa_sc, sb_sc, m_sc, l_sc, acc_sc,
                 *, blk, n_kv):
    h = pl.program_id(1)
    i = pl.program_id(2)

    @pl.when(i == 0)
    def _():
        lane = lax.broadcasted_iota(jnp.int32, (blk, LANES), 1)

        def prep(j, carry):
            start = pl.multiple_of(j * blk, blk)
            cum = cum_ref[0, pl.ds(start, blk), :]
            rest = jnp.sum(jnp.where(lane == h, cum, 0.0), axis=1, keepdims=True) * LOG2E
            bias = jnp.zeros((blk, LANES), F32)
            for piece in range(N_BIAS):
                part = rest.astype(BF16).astype(F32)
                bias = jnp.where(lane == piece, part, bias)
                rest = rest - part
            kaug_sc[j, :, 0:HEAD_DIM] = k_ref[0, pl.ds(start, blk), :]
            kaug_sc[j, :, HEAD_DIM:2 * HEAD_DIM] = bias.astype(BF16)
            vt_sc[j] = v_ref[0, pl.ds(start, blk), :].astype(F32).T.astype(BF16)
            return carry

        lax.fori_loop(0, n_kv, prep, 0)

    qs = q_ref[0].astype(F32) * (HEAD_DIM ** -0.5 * LOG2E)
    qt_sc[0:HEAD_DIM, :] = qs.T.astype(BF16)
    row = lax.broadcasted_iota(jnp.int32, (HEAD_DIM, blk), 0)
    qt_sc[HEAD_DIM:2 * HEAD_DIM, :] = jnp.where(row < N_BIAS, -1.0, 0.0).astype(BF16)
    m_sc[...] = jnp.full_like(m_sc, -jnp.inf)
    l_sc[...] = jnp.zeros_like(l_sc)
    acc_sc[...] = jnp.zeros_like(acc_sc)

    def scores(j):
        return jnp.dot(kaug_sc[j], qt_sc[...], preferred_element_type=F32)

    def update(j, s):
        m_prev = m_sc[...]
        m_new = jnp.maximum(m_prev, jnp.max(s, axis=0, keepdims=True))
        alpha = jnp.exp2(m_prev - m_new)
        p = jnp.exp2(s - m_new)
        l_sc[...] = alpha * l_sc[...] + jnp.sum(p, axis=0, keepdims=True)
        acc_sc[...] = alpha * acc_sc[...] + jnp.dot(vt_sc[j], p.astype(BF16), preferred_element_type=F32)
        m_sc[...] = m_new

    def causal(s):
        key = lax.broadcasted_iota(jnp.int32, (blk, blk), 0)
        qry = lax.broadcasted_iota(jnp.int32, (blk, blk), 1)
        return jnp.where(key <= qry, s, -jnp.inf)

    sa_sc[...] = scores(0)

    def pair_at(j):
        sb_sc[...] = scores(j + 1)
        update(j, sa_sc[...])
        sa_sc[...] = scores(j + 2)
        update(j + 1, sb_sc[...])

    def quad(jj, carry):
        pair_at(4 * jj)
        pair_at(4 * jj + 2)
        return carry

    def pair(jj, carry):
        pair_at(4 * (i // 4) + 2 * jj)
        return carry

    lax.fori_loop(0, i // 4, quad, 0)
    lax.fori_loop(0, (i % 4) // 2, pair, 0)

    @pl.when(i % 2 == 0)
    def _():
        update(i, causal(sa_sc[...]))

    @pl.when(i % 2 == 1)
    def _():
        sb_sc[...] = scores(i)
        update(i - 1, sa_sc[...])
        update(i, causal(sb_sc[...]))

    o_ref[0] = (acc_sc[...] / l_sc[...]).T.astype(BF16)


def _fox_attention(proj3, cum3, n_heads):
    n_batch, seq, _ = proj3.shape
    blk = min(512, seq)
    n_kv = seq // blk
    return pl.pallas_call(
        functools.partial(_attn_kernel, blk=blk, n_kv=n_kv),
        out_shape=jax.ShapeDtypeStruct((n_batch, seq, n_heads * HEAD_DIM), BF16),
        grid=(n_batch, n_heads, n_kv),
        in_specs=[pl.BlockSpec((1, blk, HEAD_DIM), lambda b, h, i: (b, i, h)),
                  pl.BlockSpec((1, seq, HEAD_DIM), lambda b, h, i: (b, 0, n_heads + h)),
                  pl.BlockSpec((1, seq, HEAD_DIM), lambda b, h, i: (b, 0, 2 * n_heads + h)),
                  pl.BlockSpec((1, seq, LANES), lambda b, h, i: (b, 0, 0))],
        out_specs=pl.BlockSpec((1, blk, HEAD_DIM), lambda b, h, i: (b, i, h)),
        scratch_shapes=[pltpu.VMEM((n_kv, blk, 2 * HEAD_DIM), BF16),
                        pltpu.VMEM((n_kv, HEAD_DIM, blk), BF16),
                        pltpu.VMEM((2 * HEAD_DIM, blk), BF16),
                        pltpu.VMEM((blk, blk), F32), pltpu.VMEM((blk, blk), F32),
                        pltpu.VMEM((1, blk), F32), pltpu.VMEM((1, blk), F32),
                        pltpu.VMEM((HEAD_DIM, blk), F32)],
        compiler_params=_params("arbitrary", "arbitrary", "arbitrary"),
        name="fox_attention",
    )(proj3, proj3, proj3, cum3)


def _hgrn_kernel(q_ref, f_ref, i_ref, g_ref, lbl_ref, gn_ref, o_ref, st_sc, *, rows, sub, layer):
    @pl.when(pl.program_id(2) == 0)
    def _():
        st_sc[...] = jnp.zeros_like(st_sc)

    lbl = lbl_ref[0]
    e = jnp.exp(lbl - jnp.max(lbl, axis=0, keepdims=True))
    lb = jnp.sum(e[0:layer + 1, :], axis=0, keepdims=True) / jnp.sum(e, axis=0, keepdims=True)
    f = lb + (1.0 - lb) * _sigmoid(f_ref[0].astype(F32))
    logf = jnp.log(f)
    kk = 1.0 - f
    qf = q_ref[0].astype(F32)
    qq = qf * _sigmoid(qf)

    n_ch = sub // HG_CHUNK
    r = lax.broadcasted_iota(jnp.int32, (sub, sub), 0)
    c = lax.broadcasted_iota(jnp.int32, (sub, sub), 1)
    within = (r >= c) & (r // HG_CHUNK == c // HG_CHUNK)
    tri = within.astype(BF16)
    rw = lax.broadcasted_iota(jnp.int32, (sub, n_ch * HEAD_DIM), 0)
    cw = lax.broadcasted_iota(jnp.int32, (sub, n_ch * HEAD_DIM), 1)
    own_block = rw // HG_CHUNK == cw // HEAD_DIM

    zero = jnp.zeros((), BF16)
    n_sub = rows // sub
    b_all, b_last_all = [], []
    for s in range(n_sub):
        lf = logf[s * sub:(s + 1) * sub, :]
        hi = lf.astype(BF16)
        rest = lf - hi.astype(F32)
        mid = rest.astype(BF16)
        low = (rest - mid.astype(F32)).astype(BF16)
        two = jnp.dot(tri, jnp.concatenate([hi, mid], axis=1), preferred_element_type=F32)
        b = two[:, :HEAD_DIM] + two[:, HEAD_DIM:] + jnp.dot(tri, low, preferred_element_type=F32)
        b_all.append(b)
        b_last_all.append([b[(n + 1) * HG_CHUNK - 1:(n + 1) * HG_CHUNK, :] for n in range(n_ch)])

    q_dec_all, o_all, upd_all = [], [], []
    for s in range(n_sub):
        rs = slice(s * sub, (s + 1) * sub)
        b = b_all[s]
        b_last_rows = jnp.concatenate([jnp.broadcast_to(bl, (HG_CHUNK, HEAD_DIM)) for bl in b_last_all[s]], axis=0)
        q_dec = (qq[rs, :] * jnp.exp(b)).astype(BF16)
        k_inv = (kk[rs, :] * jnp.exp(-b)).astype(BF16)
        k_tail = (kk[rs, :] * jnp.exp(b_last_rows - b)).astype(BF16)
        vv = i_ref[0, pl.ds(s * sub, sub), :]
        attn = lax.dot_general(q_dec, k_inv, NT_DIMS, preferred_element_type=F32)
        o_all.append(jnp.dot(jnp.where(within, attn, 0.0).astype(BF16), vv, preferred_element_type=F32))
        k_blocks = jnp.where(own_block, jnp.concatenate([k_tail] * n_ch, axis=1), zero)
        upd_all.append(lax.dot_general(vv, k_blocks, TN_DIMS, preferred_element_type=F32))
        q_dec_all.append(q_dec)

    st = st_sc[...]
    for s in range(n_sub):
        states = []
        for n in range(n_ch):
            states.append(st.astype(BF16))
            st = jnp.exp(b_last_all[s][n]) * st + upd_all[s][:, n * HEAD_DIM:(n + 1) * HEAD_DIM]
        q_blocks = jnp.where(own_block, jnp.concatenate([q_dec_all[s]] * n_ch, axis=1), zero)
        o = o_all[s] + lax.dot_general(q_blocks, jnp.concatenate(states, axis=1), NT_DIMS,
                                       preferred_element_type=F32)
        y = _rms(o) * gn_ref[0]
        gf = g_ref[0, pl.ds(s * sub, sub), :].astype(F32)
        o_ref[0, pl.ds(s * sub, sub), :] = (y * (gf * _sigmoid(gf))).astype(BF16)
    st_sc[...] = st


def _hgrn2(proj3, lb_logits3, g_hg3, n_heads, col0, layer):
    n_batch, seq, _ = proj3.shape
    rows = min(1024, seq)
    sub = min(256, rows)
    spec = lambda off: pl.BlockSpec((1, rows, HEAD_DIM), lambda b, h, r: (b, r, col0 + off * n_heads + h))
    return pl.pallas_call(
        functools.partial(_hgrn_kernel, rows=rows, sub=sub, layer=layer),
        out_shape=jax.ShapeDtypeStruct((n_batch, seq, n_heads * HEAD_DIM), BF16),
        grid=(n_batch, n_heads, seq // rows),
        in_specs=[spec(0), spec(1), spec(2), spec(3),
                  pl.BlockSpec((1, lb_logits3.shape[1], HEAD_DIM), lambda b, h, r: (h, 0, 0)),
                  pl.BlockSpec((1, 1, HEAD_DIM), lambda b, h, r: (h, 0, 0))],
        out_specs=pl.BlockSpec((1, rows, HEAD_DIM), lambda b, h, r: (b, r, h)),
        scratch_shapes=[pltpu.VMEM((HEAD_DIM, HEAD_DIM), F32)],
        compiler_params=_params("arbitrary", "arbitrary", "arbitrary"),
        name="hgrn2",
    )(proj3, proj3, proj3, proj3, lb_logits3, g_hg3)


def _pack_halves(v):
    n = v.shape[1] // 2
    lo = lax.bitcast_convert_type(v[:, :n].astype(BF16).astype(F32), jnp.uint32) >> 16
    hi = lax.bitcast_convert_type(v[:, n:].astype(BF16).astype(F32), jnp.uint32) & jnp.uint32(0xFFFF0000)
    return lo | hi


def _unpack_halves(p):
    lo = lax.bitcast_convert_type(p << 16, F32).astype(BF16)
    hi = lax.bitcast_convert_type(p & jnp.uint32(0xFFFF0000), F32).astype(BF16)
    return lo, hi


def _outproj_kernel(fox_ref, hg_ref, x_ref, mod_ref, gfox_ref, g2_ref, w_ref, wr_ref, br_ref,
                    x1_ref, h2_ref, lg_ref, *, n_exp, group):
    groups = [pl.ds(k * group, group) for k in range(x_ref.shape[0] // group)]
    mixed = []
    for rows in groups:
        fox = _rms(fox_ref[rows, :].astype(F32)) * gfox_ref[...]
        mixed.append(jnp.concatenate([fox.astype(BF16), hg_ref[rows, :]], axis=1))
    mix = [jnp.dot(m, w_ref[...], preferred_element_type=F32) for m in mixed]
    h2s = []
    for rows, m in zip(groups, mix):
        x1 = x_ref[rows, :] + mod_ref[0, 2:3, :] * m
        x1_ref[rows, :] = x1
        h2 = _rms(x1) * (g2_ref[...] * (1.0 + mod_ref[0, 4:5, :])) + mod_ref[0, 3:4, :]
        h2_ref[rows, :] = _pack_halves(h2)
        h2s.append(h2)
    for rows, h2 in zip(groups, h2s):
        h_hi = h2.astype(BF16)
        h_lo = (h2 - h_hi.astype(F32)).astype(BF16)
        part = jnp.dot(h_hi, wr_ref[...], preferred_element_type=F32)
        part = part + jnp.dot(h_lo, wr_ref[...], preferred_element_type=F32)
        logits = part + pltpu.roll(part, LANES - n_exp, axis=1)
        lg_ref[:, rows] = logits.T[0:n_exp, :] + br_ref[...]


def _outproj(fox2d, hg2d, x2d, mod3, g_fox, g2, w_out, w_router, b_router, seq):
    t, d = x2d.shape
    fox_w = fox2d.shape[1]
    n_exp = w_router.shape[1]
    assert 2 * n_exp <= LANES
    wr_hi = w_router.astype(BF16)
    wr_lo = (w_router - wr_hi.astype(F32)).astype(BF16)
    wr_cat = jnp.pad(jnp.concatenate([wr_hi, wr_lo], axis=1), ((0, 0), (0, LANES - 2 * n_exp)))
    tm = 512
    row = lambda i: (i, 0)
    const = lambda i: (0, 0)
    return pl.pallas_call(
        functools.partial(_outproj_kernel, n_exp=n_exp, group=256),
        out_shape=(jax.ShapeDtypeStruct((t, d), F32), jax.ShapeDtypeStruct((t, d // 2), jnp.uint32),
                   jax.ShapeDtypeStruct((n_exp, t), F32)),
        grid=(t // tm,),
        in_specs=[pl.BlockSpec((tm, fox_w), row),
                  pl.BlockSpec((tm, hg2d.shape[1]), row),
                  pl.BlockSpec((tm, d), row),
                  pl.BlockSpec((1, mod3.shape[1], d), lambda i: (i // (seq // tm), 0, 0)),
                  pl.BlockSpec((1, fox_w), const),
                  pl.BlockSpec((1, d), const),
                  pl.BlockSpec(w_out.shape, const),
                  pl.BlockSpec((d, LANES), const),
                  pl.BlockSpec((n_exp, 1), const)],
        out_specs=(pl.BlockSpec((tm, d), row), pl.BlockSpec((tm, d // 2), row),
                   pl.BlockSpec((n_exp, tm), lambda i: (0, i))),
        compiler_params=_params("arbitrary"),
        name="outproj_router",
    )(fox2d, hg2d, x2d, mod3, g_fox, g2, w_out, wr_cat, b_router)


def _route_kernel(lg_ref, pos_ref, gate_ref, cnt_ref, cnt_sc, run_sc, *, n_exp, tb):
    phase = pl.program_id(0)
    i = pl.program_id(1)

    @pl.when((phase == 0) & (i == 0))
    def _():
        cnt_sc[...] = jnp.zeros_like(cnt_sc)
        run_sc[...] = jnp.zeros_like(run_sc)

    logits = lg_ref[...]
    eidx = lax.broadcasted_iota(jnp.int32, (n_exp, tb), 0).astype(F32)
    work = logits
    vals, hots = [], []
    for _ in range(TOP_K):
        m = jnp.max(work, axis=0, keepdims=True)
        first = jnp.min(jnp.where(work == m, eidx, float(n_exp)), axis=0, keepdims=True)
        hot = eidx == first
        vals.append(m)
        hots.append(hot)
        work = jnp.where(hot, -jnp.inf, work)
    sel = hots[0] | hots[1] | hots[2] | hots[3]
    self32 = sel.astype(F32)

    @pl.when(phase == 0)
    def _():
        cnt_sc[...] += jnp.sum(self32, axis=1, keepdims=True)

    @pl.when(phase == 1)
    def _():
        cnt = cnt_sc[...]
        padded = jnp.ceil(cnt / MOE_BLOCK) * MOE_BLOCK
        r = lax.broadcasted_iota(jnp.int32, (n_exp, n_exp), 0)
        c = lax.broadcasted_iota(jnp.int32, (n_exp, n_exp), 1)
        strict = (c < r).astype(F32)
        pstart = jnp.dot(strict, jnp.broadcast_to(padded, (n_exp, LANES)), preferred_element_type=F32,
                         precision=HIGHEST)[:, 0:1]
        tr = lax.broadcasted_iota(jnp.int32, (tb, tb), 0)
        tc = lax.broadcasted_iota(jnp.int32, (tb, tb), 1)
        upper = (tr < tc).astype(BF16)
        rank = jnp.dot(sel.astype(BF16), upper, preferred_element_type=F32)
        base = pstart + run_sc[...] + rank
        exps = [jnp.exp(v - vals[0]) for v in vals]
        denom = exps[0] + exps[1] + exps[2] + exps[3]
        for k in range(TOP_K):
            pos_k = jnp.sum(jnp.where(hots[k], base, 0.0), axis=0, keepdims=True)
            pos_ref[pl.ds(k, 1), :] = pos_k.astype(jnp.int32)
            gate_ref[pl.ds(k, 1), :] = exps[k] / denom
        run_sc[...] += jnp.sum(self32, axis=1, keepdims=True)
        cnt_ref[...] = cnt


def _route(logits_t):
    n_exp, t = logits_t.shape
    tb = min(512, t)
    return pl.pallas_call(
        functools.partial(_route_kernel, n_exp=n_exp, tb=tb),
        out_shape=(jax.ShapeDtypeStruct((TOP_K, t), jnp.int32), jax.ShapeDtypeStruct((TOP_K, t), F32),
                   jax.ShapeDtypeStruct((n_exp, 1), F32)),
        grid=(2, t // tb),
        in_specs=[pl.BlockSpec((n_exp, tb), lambda p, i: (0, i))],
        out_specs=(pl.BlockSpec((TOP_K, tb), lambda p, i: (0, i * p)),
                   pl.BlockSpec((TOP_K, tb), lambda p, i: (0, i * p)),
                   pl.BlockSpec((n_exp, 1), lambda p, i: (0, 0))),
        scratch_shapes=[pltpu.VMEM((n_exp, 1), F32), pltpu.VMEM((n_exp, 1), F32)],
        compiler_params=_params("arbitrary", "arbitrary"),
        name="route_topk",
    )(logits_t)


N_DMA_QUEUES = 2


def _row_copy(src_hbm, row, dst_vmem, slot, sem):
    return pltpu.make_async_copy(src_hbm.at[pl.ds(row, 1), :], dst_vmem.at[pl.ds(slot, 1), :], sem)


def _gather_rows(src_hbm, idx_ref, idx0, n_rows, dst_vmem, sem):
    for r in range(n_rows):
        _row_copy(src_hbm, idx_ref[0, 0, idx0 + r], dst_vmem, r, sem).start(priority=r % N_DMA_QUEUES)


def _wait_rows(src_hbm, n_rows, dst_vmem, sem):
    pltpu.make_async_copy(src_hbm.at[pl.ds(0, n_rows), :], dst_vmem, sem).wait()


def _dispatch_kernel(pad0_ref, padn_ref, nused_ref, pos_ref, h_ref, xs_hbm, zeros, sem, zsem,
                     *, tb, n_exp, n_blocks):
    @pl.when(pl.program_id(0) == 0)
    def _():
        zeros[...] = jnp.zeros_like(zeros)

        def each_fill(fn):
            def per_expert(e, carry):
                def per_row(r, inner):
                    fn(pltpu.make_async_copy(zeros.at[pl.ds(0, 1), :], xs_hbm.at[pl.ds(pad0_ref[e] + r, 1), :], zsem))
                    return inner

                lax.fori_loop(0, padn_ref[e], per_row, 0)
                return carry

            lax.fori_loop(0, n_exp, per_expert, 0)

            def per_block(b, carry):
                row0 = pl.multiple_of(b * MOE_BLOCK, MOE_BLOCK)
                fn(pltpu.make_async_copy(zeros, xs_hbm.at[pl.ds(row0, MOE_BLOCK), :], zsem))
                return carry

            lax.fori_loop(nused_ref[0], n_blocks, per_block, 0)

        each_fill(lambda cp: cp.start())
        each_fill(lambda cp: cp.wait())

    for r in range(tb):
        for k in range(TOP_K):
            pltpu.make_async_copy(h_ref.at[pl.ds(r, 1), :], xs_hbm.at[pl.ds(pos_ref[0, 0, k * tb + r], 1), :],
                                  sem).start(priority=k % N_DMA_QUEUES)
    for k in range(TOP_K):
        pltpu.make_async_copy(h_ref, xs_hbm.at[pl.ds(0, tb), :], sem).wait()


def _dispatch(h2p, pos3, pad_start, pad_count, n_used, n_rows):
    t, dp = h2p.shape
    n_tiles = pos3.shape[0]
    tb = pos3.shape[2] // TOP_K
    return pl.pallas_call(
        functools.partial(_dispatch_kernel, tb=tb, n_exp=pad_start.shape[0], n_blocks=n_rows // MOE_BLOCK),
        out_shape=jax.ShapeDtypeStruct((n_rows, dp), h2p.dtype),
        grid_spec=pltpu.PrefetchScalarGridSpec(
            num_scalar_prefetch=3,
            grid=(n_tiles,),
            in_specs=[pl.BlockSpec((1, 1, TOP_K * tb), lambda i, p0, pn, nu: (i, 0, 0), memory_space=pltpu.SMEM),
                      pl.BlockSpec((tb, dp), lambda i, p0, pn, nu: (i, 0))],
            out_specs=pl.BlockSpec(memory_space=pl.ANY),
            scratch_shapes=[pltpu.VMEM((MOE_BLOCK, dp), h2p.dtype), pltpu.SemaphoreType.DMA(()),
                            pltpu.SemaphoreType.DMA(())]),
        compiler_params=_params("arbitrary"),
        name="moe_dispatch",
    )(pad_start, pad_count, n_used, pos3, h2p)


def _new_expert(be_ref, b):
    return (b == 0) | (be_ref[b] != be_ref[jnp.maximum(b - 1, 0)])


W_PIECES = 4


def _weight_copies(w_hbm, e, col0, stage, sem):
    rows = stage.shape[0] // W_PIECES
    return [pltpu.make_async_copy(w_hbm.at[e, pl.ds(i * rows, rows), pl.ds(col0, stage.shape[1])],
                                  stage.at[pl.ds(i * rows, rows), :], sem) for i in range(W_PIECES)]


def _convert_rows(src_f32, dst_bf16):
    rows = 16

    def body(i, carry):
        r0 = pl.multiple_of(i * rows, rows)
        dst_bf16[pl.ds(r0, rows), :] = src_f32[pl.ds(r0, rows), :].astype(BF16)
        return carry

    lax.fori_loop(0, src_f32.shape[0] // rows, body, 0, unroll=4)


def _gateup_kernel(be_ref, nxt_ref, last_ref, nused_ref, x_ref, w_hbm, bg_ref, bl_ref, o_ref,
                   stg_g, stg_l, wg_sc, wl_sc, sem, *, nf, tf):
    c = pl.program_id(0)
    b = pl.program_id(1)

    def slab(e, chunk):
        col = pl.multiple_of(chunk * tf, tf)
        return _weight_copies(w_hbm, e, col, stg_g, sem) + _weight_copies(w_hbm, e, col + nf * tf, stg_l, sem)

    @pl.when((c == 0) & (b == 0))
    def _():
        for cp in slab(be_ref[0], 0):
            cp.start()

    @pl.when(_new_expert(be_ref, b))
    def _():
        for cp in slab(0, 0):
            cp.wait()
        _convert_rows(stg_g, wg_sc)
        _convert_rows(stg_l, wl_sc)
        last_run = last_ref[b] == 1

        @pl.when(jnp.logical_not(last_run))
        def _():
            for cp in slab(nxt_ref[b], c):
                cp.start()

        @pl.when(last_run & (c + 1 < nf))
        def _():
            for cp in slab(be_ref[0], c + 1):
                cp.start()

    @pl.when(b < nused_ref[0])
    def _():
        x = jnp.concatenate(_unpack_halves(x_ref[...]), axis=1)
        glu = jnp.dot(x, wg_sc[...], preferred_element_type=F32) + bg_ref[0]
        lin = jnp.dot(x, wl_sc[...], preferred_element_type=F32) + bl_ref[0]
        glu = jnp.minimum(glu, SWIGLU_LIMIT)
        lin = jnp.clip(lin, -SWIGLU_LIMIT, SWIGLU_LIMIT)
        o_ref[...] = (glu * _sigmoid(SWIGLU_ALPHA * glu) * (lin + 1.0)).astype(BF16)

    @pl.when(b >= nused_ref[0])
    def _():
        o_ref[...] = jnp.zeros_like(o_ref)


def _gateup(runs, n_used, xs, w_gu, b_gu3):
    n_rows, dp = xs.shape
    d = 2 * dp
    n_blocks = n_rows // MOE_BLOCK
    d_ff = w_gu.shape[2] // 2
    tf = min(1024, d_ff)
    nf = d_ff // tf
    return pl.pallas_call(
        functools.partial(_gateup_kernel, nf=nf, tf=tf),
        out_shape=jax.ShapeDtypeStruct((n_rows, d_ff), BF16),
        grid_spec=pltpu.PrefetchScalarGridSpec(
            num_scalar_prefetch=4,
            grid=(nf, n_blocks),
            in_specs=[pl.BlockSpec((MOE_BLOCK, dp), lambda c, b, be, nx, la, nu: (jnp.minimum(b, nu[0] - 1), 0)),
                      pl.BlockSpec(memory_space=pl.ANY),
                      pl.BlockSpec((1, 1, tf), lambda c, b, be, nx, la, nu: (be[b], 0, c)),
                      pl.BlockSpec((1, 1, tf), lambda c, b, be, nx, la, nu: (be[b], 0, nf + c))],
            out_specs=pl.BlockSpec((MOE_BLOCK, tf), lambda c, b, be, nx, la, nu: (b, c)),
            scratch_shapes=[pltpu.VMEM((d, tf), F32), pltpu.VMEM((d, tf), F32),
                            pltpu.VMEM((d, tf), BF16), pltpu.VMEM((d, tf), BF16),
                            pltpu.SemaphoreType.DMA(())]),
        compiler_params=_params("arbitrary", "arbitrary"),
        name="moe_gateup",
    )(*runs, n_used, xs, w_gu, b_gu3, b_gu3)


def _down_kernel(be_ref, nxt_ref, last_ref, nused_ref, a_ref, w_hbm, b_ref, o_ref, stg, w_sc, sem):
    b = pl.program_id(0)

    @pl.when(b == 0)
    def _():
        for cp in _weight_copies(w_hbm, be_ref[0], 0, stg, sem):
            cp.start()

    @pl.when(_new_expert(be_ref, b))
    def _():
        for cp in _weight_copies(w_hbm, 0, 0, stg, sem):
            cp.wait()
        _convert_rows(stg, w_sc)

        @pl.when(last_ref[b] == 0)
        def _():
            for cp in _weight_copies(w_hbm, nxt_ref[b], 0, stg, sem):
                cp.start()

    @pl.when(b < nused_ref[0])
    def _():
        y = jnp.dot(a_ref[...], w_sc[...], preferred_element_type=F32) + b_ref[0]
        o_ref[...] = _pack_halves(y)

    @pl.when(b >= nused_ref[0])
    def _():
        o_ref[...] = jnp.zeros_like(o_ref)


def _down(runs, n_used, act, w_down, b_down3):
    n_rows, d_ff = act.shape
    n_blocks = n_rows // MOE_BLOCK
    d = w_down.shape[2]
    return pl.pallas_call(
        _down_kernel,
        out_shape=jax.ShapeDtypeStruct((n_rows, d // 2), jnp.uint32),
        grid_spec=pltpu.PrefetchScalarGridSpec(
            num_scalar_prefetch=4,
            grid=(n_blocks,),
            in_specs=[pl.BlockSpec((MOE_BLOCK, d_ff), lambda b, be, nx, la, nu: (b, 0)),
                      pl.BlockSpec(memory_space=pl.ANY),
                      pl.BlockSpec((1, 1, d), lambda b, be, nx, la, nu: (be[b], 0, 0))],
            out_specs=pl.BlockSpec((MOE_BLOCK, d // 2), lambda b, be, nx, la, nu: (b, 0)),
            scratch_shapes=[pltpu.VMEM((d_ff, d), F32), pltpu.VMEM((d_ff, d), BF16),
                            pltpu.SemaphoreType.DMA(())]),
        compiler_params=_params("arbitrary"),
        name="moe_down",
    )(*runs, n_used, act, w_down, b_down3)


COMBINE_TILES = 4


def _combine_kernel(pos_ref, pos_next_ref, y_hbm, gate_ref, x1_ref, mod_ref, gfin_ref, o_ref, b0, b1, b2, b3, sem,
                    *, tb, n_steps, last_layer):
    i = pl.program_id(0)
    bufs = (b0, b1, b2, b3)
    n_idx = TOP_K * tb

    def request(idx_ref, u_src, u_dst):
        _gather_rows(y_hbm, idx_ref, u_src * n_idx, n_idx, bufs[u_dst], sem.at[u_dst])

    def combine(u):
        rows = pl.ds(u * tb, tb)
        gates = gate_ref[rows, :]
        moe_lo = moe_hi = None
        for k in range(TOP_K):
            lo, hi = _unpack_halves(bufs[u][pl.ds(k * tb, tb), :])
            g = gates[:, k:k + 1]
            moe_lo = g * lo.astype(F32) if k == 0 else moe_lo + g * lo.astype(F32)
            moe_hi = g * hi.astype(F32) if k == 0 else moe_hi + g * hi.astype(F32)
        moe = jnp.concatenate([moe_lo, moe_hi], axis=1)
        x2 = x1_ref[rows, :] + mod_ref[0, 5:6, :] * moe
        o_ref[rows, :] = _rms(x2) * gfin_ref[...] if last_layer else x2

    @pl.when(i == 0)
    def _():
        request(pos_ref, 0, 0)
        request(pos_ref, 1, 1)

    for u in range(COMBINE_TILES):
        _wait_rows(y_hbm, n_idx, bufs[u], sem.at[u])
        if u + 2 < COMBINE_TILES:
            request(pos_ref, u + 2, u + 2)
        else:
            request(pos_next_ref, u + 2 - COMBINE_TILES, u + 2 - COMBINE_TILES)
        combine(u)

    @pl.when(i == n_steps - 1)
    def _():
        _wait_rows(y_hbm, n_idx, b0, sem.at[0])
        _wait_rows(y_hbm, n_idx, b1, sem.at[1])


def _combine(pos3, y_buf, gates, x1, mod3, g_final, seq, last_layer):
    t, d = x1.shape
    n_steps = pos3.shape[0]
    tb = pos3.shape[2] // (COMBINE_TILES * TOP_K)
    rows = COMBINE_TILES * tb
    idx_spec = lambda step: pl.BlockSpec((1, 1, pos3.shape[2]), lambda i: (jnp.minimum(i + step, n_steps - 1), 0, 0),
                                         memory_space=pltpu.SMEM)
    buf = pltpu.VMEM((TOP_K * tb, y_buf.shape[1]), y_buf.dtype)
    return pl.pallas_call(
        functools.partial(_combine_kernel, tb=tb, n_steps=n_steps, last_layer=last_layer),
        out_shape=jax.ShapeDtypeStruct((t, d), F32),
        grid=(n_steps,),
        in_specs=[idx_spec(0), idx_spec(1),
                  pl.BlockSpec(memory_space=pl.ANY),
                  pl.BlockSpec((rows, TOP_K), lambda i: (i, 0)),
                  pl.BlockSpec((rows, d), lambda i: (i, 0)),
                  pl.BlockSpec((1, mod3.shape[1], d), lambda i: (i // (seq // rows), 0, 0)),
                  pl.BlockSpec((1, d), lambda i: (0, 0))],
        out_specs=pl.BlockSpec((rows, d), lambda i: (i, 0)),
        scratch_shapes=[buf] * COMBINE_TILES + [pltpu.SemaphoreType.DMA((COMBINE_TILES,))],
        compiler_params=_params("arbitrary"),
        name="moe_combine",
    )(pos3, pos3, y_buf, gates, x1, mod3, g_final)


def kernel(x, c, w_ada, b_ada, g_norm1, g_norm2, w_in, b_fgate, g_fox_out, lb_logits, g_hg_out,
           w_out, w_router, b_router, w_gu, b_gu, w_down, b_down, g_final):
    n_batch, seq, d = x.shape
    t = n_batch * seq
    depth = w_ada.shape[0]
    fox_heads = b_fgate.shape[1]
    fox_w = g_fox_out.shape[1]
    hg_w = g_hg_out.shape[1]
    hg_heads = hg_w // HEAD_DIM
    n_exp = w_router.shape[2]
    assert fox_w == fox_heads * HEAD_DIM and fox_heads <= LANES
    n_blocks = -(-t * TOP_K // MOE_BLOCK) + n_exp
    n_mod = w_ada.shape[2] // d

    x2d = x.reshape(t, d)
    for l in range(depth):
        mod3 = _ada_mod(c, w_ada[l], b_ada[l]).reshape(n_batch, n_mod, d)

        w_l = w_in[l].astype(BF16)
        w_main = jnp.concatenate([w_l[:, :3 * fox_w], w_l[:, 3 * fox_w + fox_heads:]], axis=1)
        w_fg = jnp.pad(w_l[:, 3 * fox_w:3 * fox_w + fox_heads], ((0, 0), (0, LANES - fox_heads)))
        b_fg = jnp.pad(b_fgate[l], (0, LANES - fox_heads)).reshape(1, LANES)
        proj, cum = _inproj(x2d, mod3, g_norm1[l].reshape(1, d), w_main, w_fg, b_fg, seq)
        proj3 = proj.reshape(n_batch, seq, -1)

        fox = _fox_attention(proj3, cum.reshape(n_batch, seq, LANES), fox_heads)

        lb3 = lb_logits.reshape(lb_logits.shape[0], hg_heads, HEAD_DIM).transpose(1, 0, 2)
        hg = _hgrn2(proj3, lb3, g_hg_out[l].reshape(hg_heads, 1, HEAD_DIM), hg_heads, 3 * fox_heads, l)

        x1, h2p, logits_t = _outproj(
            fox.reshape(t, fox_w), hg.reshape(t, hg_w), x2d, mod3, g_fox_out[l].reshape(1, fox_w),
            g_norm2[l].reshape(1, d), w_out[l].astype(BF16), w_router[l], b_router[l].reshape(n_exp, 1), seq)

        pos_t, gates_t, counts = _route(logits_t)

        cnt = counts[:, 0].astype(jnp.int32)
        padded = (cnt + MOE_BLOCK - 1) // MOE_BLOCK * MOE_BLOCK
        padded_end = jnp.cumsum(padded)
        block_row0 = jnp.arange(n_blocks, dtype=jnp.int32) * MOE_BLOCK
        block_e = jnp.sum((block_row0[:, None] >= padded_end[None, :]).astype(jnp.int32), axis=1)
        n_used = (padded_end[-1:] // MOE_BLOCK).astype(jnp.int32)
        block_id = jnp.arange(n_blocks, dtype=jnp.int32)
        block_e = jnp.minimum(block_e, jnp.sum(jnp.where(block_id == n_used - 1, block_e, 0)))
        change_at = jnp.where(block_e != jnp.roll(block_e, 1), block_id, n_blocks).at[0].set(n_blocks)
        next_change = jnp.flip(lax.cummin(jnp.flip(jnp.roll(change_at, -1).at[-1].set(n_blocks))))
        last_run = (next_change >= n_blocks).astype(jnp.int32)
        next_e = block_e[jnp.minimum(next_change, n_blocks - 1)]
        runs = (block_e, next_e, last_run)

        def tile_major(tb):
            return pos_t.reshape(TOP_K, t // tb, tb).transpose(1, 0, 2).reshape(t // tb, 1, TOP_K * tb)

        pad_count = padded - cnt
        xs = _dispatch(h2p, tile_major(min(512, seq)), padded_end - pad_count, pad_count, n_used,
                       n_blocks * MOE_BLOCK)
        act = _gateup(runs, n_used, xs, w_gu[l], b_gu[l].reshape(n_exp, 1, -1))
        y_buf = _down(runs, n_used, act, w_down[l], b_down[l].reshape(n_exp, 1, d))
        tb = min(128, seq // COMBINE_TILES)
        pos_steps = tile_major(tb).reshape(t // (COMBINE_TILES * tb), 1, COMBINE_TILES * TOP_K * tb)
        x2d = _combine(pos_steps, y_buf, gates_t.T, x1, mod3, g_final.reshape(1, d), seq, l == depth - 1)
    return x2d.reshape(n_batch, seq, d)
```

```python
import functools

import jax
import jax.numpy as jnp
from jax import lax
from jax.experimental import pallas as pl
from jax.experimental.pallas import tpu as pltpu

HEAD_DIM = 128
TOP_K = 4
MOE_BLOCK = 256
HG_CHUNK = 64
NORM_EPS = 1e-6
SWIGLU_ALPHA = 1.702
SWIGLU_LIMIT = 7.0
LANES = 128
VMEM_LIMIT = 56 * 1024 * 1024

F32 = jnp.float32
BF16 = jnp.bfloat16
HIGHEST = lax.Precision.HIGHEST
NT_DIMS = (((1,), (1,)), ((), ()))
TN_DIMS = (((0,), (0,)), ((), ()))


def _params(*semantics):
    return pltpu.CompilerParams(dimension_semantics=semantics, vmem_limit_bytes=VMEM_LIMIT)


def _sigmoid(z):
    return 1.0 / (1.0 + jnp.exp(-z))


def _rms(v):
    return v * lax.rsqrt(jnp.mean(v * v, axis=-1, keepdims=True) + NORM_EPS)


def _ada_kernel(ct_ref, w_ref, b_ref, o_ref, *, n_batch, d_model):
    ct = ct_ref[...]
    cond = ct * _sigmoid(ct)
    rows = 256
    for b in range(n_batch):
        acc = jnp.zeros((8, w_ref.shape[1]), F32)
        for i in range(d_model // rows):
            w3 = w_ref[pl.ds(i * rows, rows), :].reshape(rows // 8, 8, -1)
            c3 = cond[i * rows:(i + 1) * rows, b:b + 1].reshape(rows // 8, 8, 1)
            acc = acc + jnp.sum(w3 * c3, axis=0)
        o_ref[pl.ds(b, 1), :] = jnp.sum(acc, axis=0, keepdims=True) + b_ref[...]


def _ada_mod(c, w_ada, b_ada):
    n_batch, d = c.shape
    n = w_ada.shape[1]
    tn = 1024
    return pl.pallas_call(
        functools.partial(_ada_kernel, n_batch=n_batch, d_model=d),
        out_shape=jax.ShapeDtypeStruct((n_batch, n), F32),
        grid=(n // tn,),
        in_specs=[pl.BlockSpec((d, n_batch), lambda j: (0, 0)),
                  pl.BlockSpec((d, tn), lambda j: (0, j)),
                  pl.BlockSpec((1, tn), lambda j: (0, j))],
        out_specs=pl.BlockSpec((n_batch, tn), lambda j: (0, j)),
        compiler_params=_params("arbitrary"),
        name="ada_mod",
    )(c.T, w_ada, b_ada.reshape(1, n))


def _inproj_kernel(x_ref, mod_ref, g_ref, w_ref, wf_ref, bf_ref, proj_ref, cum_ref, h_sc, carry_sc,
                   *, tiles_per_batch, sub):
    i = pl.program_id(0)
    j = pl.program_id(1)

    @pl.when(j == 0)
    def _():
        x = x_ref[...]
        h = _rms(x) * (g_ref[...] * (1.0 + mod_ref[0, 1:2, :])) + mod_ref[0, 0:1, :]
        hb = h.astype(BF16)
        h_sc[...] = hb
        z = jnp.dot(hb, wf_ref[...], preferred_element_type=F32) + bf_ref[...]
        logf = jnp.minimum(z, 0.0) - jnp.log(1.0 + jnp.exp(-jnp.abs(z)))

        @pl.when(i % tiles_per_batch == 0)
        def _():
            carry_sc[...] = jnp.zeros_like(carry_sc)

        r = lax.broadcasted_iota(jnp.int32, (sub, sub), 0)
        c = lax.broadcasted_iota(jnp.int32, (sub, sub), 1)
        tri = (r >= c).astype(BF16)
        carry = carry_sc[...]
        for s in range(x.shape[0] // sub):
            blk = logf[s * sub:(s + 1) * sub, :]
            hi = blk.astype(BF16)
            rest = blk - hi.astype(F32)
            mid = rest.astype(BF16)
            low = (rest - mid.astype(F32)).astype(BF16)
            two = jnp.dot(tri, jnp.concatenate([hi, mid], axis=1), preferred_element_type=F32)
            cs = two[:, :LANES] + two[:, LANES:] + jnp.dot(tri, low, preferred_element_type=F32) + carry
            cum_ref[pl.ds(s * sub, sub), :] = cs
            carry = cs[sub - 1:sub, :]
        carry_sc[...] = carry

    proj_ref[...] = jnp.dot(h_sc[...], w_ref[...], preferred_element_type=F32).astype(BF16)


def _inproj(x2d, mod3, g1, w_main, w_fg, b_fg, seq):
    t, d = x2d.shape
    n = w_main.shape[1]
    tm = min(1024, seq)
    tn = next(c for c in (1792, 1024, 512, 256, 128) if n % c == 0)
    return pl.pallas_call(
        functools.partial(_inproj_kernel, tiles_per_batch=seq // tm, sub=256),
        out_shape=(jax.ShapeDtypeStruct((t, n), BF16), jax.ShapeDtypeStruct((t, LANES), F32)),
        grid=(t // tm, n // tn),
        in_specs=[pl.BlockSpec((tm, d), lambda i, j: (i, 0)),
                  pl.BlockSpec((1, mod3.shape[1], d), lambda i, j: (i // (seq // tm), 0, 0)),
                  pl.BlockSpec((1, d), lambda i, j: (0, 0)),
                  pl.BlockSpec((d, tn), lambda i, j: (0, j)),
                  pl.BlockSpec((d, LANES), lambda i, j: (0, 0)),
                  pl.BlockSpec((1, LANES), lambda i, j: (0, 0))],
        out_specs=(pl.BlockSpec((tm, tn), lambda i, j: (i, j)),
                   pl.BlockSpec((tm, LANES), lambda i, j: (i, 0))),
        scratch_shapes=[pltpu.VMEM((tm, d), BF16), pltpu.VMEM((1, LANES), F32)],
        compiler_params=_params("arbitrary", "arbitrary"),
        name="inproj",
    )(x2d, mod3, g1, w_main, w_fg, b_fg)


N_BIAS = 3
LOG2E = 1.4426950408889634


def _attn_kernel(q_ref, k_ref, v_ref, cum_ref, o_ref, kaug_sc, vt_sc, qt_sc, sa_sc, sb_sc, m_sc, l_sc, acc_sc,
                 *, blk, n_kv):
    h = pl.program_id(1)
    i = pl.program_id(2)

    @pl.when(i == 0)
    def _():
        lane = lax.broadcasted_iota(jnp.int32, (blk, LANES), 1)

        def prep(j, carry):
            start = pl.multiple_of(j * blk, blk)
            cum = cum_ref[0, pl.ds(start, blk), :]
            rest = jnp.sum(jnp.where(lane == h, cum, 0.0), axis=1, keepdims=True) * LOG2E
            bias = jnp.zeros((blk, LANES), F32)
            for piece in range(N_BIAS):
                part = rest.astype(BF16).astype(F32)
                bias = jnp.where(lane == piece, part, bias)
                rest = rest - part
            kaug_sc[j, :, 0:HEAD_DIM] = k_ref[0, pl.ds(start, blk), :]
            kaug_sc[j, :, HEAD_DIM:2 * HEAD_DIM] = bias.astype(BF16)
            vt_sc[j] = v_ref[0, pl.ds(start, blk), :].astype(F32).T.astype(BF16)
            return carry

        lax.fori_loop(0, n_kv, prep, 0)

    qs = q_ref[0].astype(F32) * (HEAD_DIM ** -0.5 * LOG2E)
    qt_sc[0:HEAD_DIM, :] = qs.T.astype(BF16)
    row = lax.broadcasted_iota(jnp.int32, (HEAD_DIM, blk), 0)
    qt_sc[HEAD_DIM:2 * HEAD_DIM, :] = jnp.where(row < N_BIAS, -1.0, 0.0).astype(BF16)
    m_sc[...] = jnp.full_like(m_sc, -jnp.inf)
    l_sc[...] = jnp.zeros_like(l_sc)
    acc_sc[...] = jnp.zeros_like(acc_sc)

    def scores(j):
        return jnp.dot(kaug_sc[j], qt_sc[...], preferred_element_type=F32)

    def update(j, s):
        m_prev = m_sc[...]
        m_new = jnp.maximum(m_prev, jnp.max(s, axis=0, keepdims=True))
        alpha = jnp.exp2(m_prev - m_new)
        p = jnp.exp2(s - m_new)
        l_sc[...] = alpha * l_sc[...] + jnp.sum(p, axis=0, keepdims=True)
        acc_sc[...] = alpha * acc_sc[...] + jnp.dot(vt_sc[j], p.astype(BF16), preferred_element_type=F32)
        m_sc[...] = m_new

    def causal(s):
        key = lax.broadcasted_iota(jnp.int32, (blk, blk), 0)
        qry = lax.broadcasted_iota(jnp.int32, (blk, blk), 1)
        return jnp.where(key <= qry, s, -jnp.inf)

    sa_sc[...] = scores(0)

    def pair_at(j):
        sb_sc[...] = scores(j + 1)
        update(j, sa_sc[...])
        sa_sc[...] = scores(j + 2)
        update(j + 1, sb_sc[...])

    def quad(jj, carry):
        pair_at(4 * jj)
        pair_at(4 * jj + 2)
        return carry

    def pair(jj, carry):
        pair_at(4 * (i // 4) + 2 * jj)
        return carry

    lax.fori_loop(0, i // 4, quad, 0)
    lax.fori_loop(0, (i % 4) // 2, pair, 0)

    @pl.when(i % 2 == 0)
    def _():
        update(i, causal(sa_sc[...]))

    @pl.when(i % 2 == 1)
    def _():
        sb_sc[...] = scores(i)
        update(i - 1, sa_sc[...])
        update(i, causal(sb_sc[...]))

    o_ref[0] = (acc_sc[...] / l_sc[...]).T.astype(BF16)


def _fox_attention(proj3, cum3, n_heads):
    n_batch, seq, _ = proj3.shape
    blk = min(512, seq)
    n_kv = seq // blk
    return pl.pallas_call(
        functools.partial(_attn_kernel, blk=blk, n_kv=n_kv),
        out_shape=jax.ShapeDtypeStruct((n_batch, seq, n_heads * HEAD_DIM), BF16),
        grid=(n_batch, n_heads, n_kv),
        in_specs=[pl.BlockSpec((1, blk, HEAD_DIM), lambda b, h, i: (b, i, h)),
                  pl.BlockSpec((1, seq, HEAD_DIM), lambda b, h, i: (b, 0, n_heads + h)),
                  pl.BlockSpec((1, seq, HEAD_DIM), lambda b, h, i: (b, 0, 2 * n_heads + h)),
                  pl.BlockSpec((1, seq, LANES), lambda b, h, i: (b, 0, 0))],
        out_specs=pl.BlockSpec((1, blk, HEAD_DIM), lambda b, h, i: (b, i, h)),
        scratch_shapes=[pltpu.VMEM((n_kv, blk, 2 * HEAD_DIM), BF16),
                        pltpu.VMEM((n_kv, HEAD_DIM, blk), BF16),
                        pltpu.VMEM((2 * HEAD_DIM, blk), BF16),
                        pltpu.VMEM((blk, blk), F32), pltpu.VMEM((blk, blk), F32),
                        pltpu.VMEM((1, blk), F32), pltpu.VMEM((1, blk), F32),
                        pltpu.VMEM((HEAD_DIM, blk), F32)],
        compiler_params=_params("arbitrary", "arbitrary", "arbitrary"),
        name="fox_attention",
    )(proj3, proj3, proj3, cum3)


def _hgrn_kernel(q_ref, f_ref, i_ref, g_ref, lbl_ref, gn_ref, o_ref, st_sc, *, rows, sub, layer):
    @pl.when(pl.program_id(2) == 0)
    def _():
        st_sc[...] = jnp.zeros_like(st_sc)

    lbl = lbl_ref[0]
    e = jnp.exp(lbl - jnp.max(lbl, axis=0, keepdims=True))
    lb = jnp.sum(e[0:layer + 1, :], axis=0, keepdims=True) / jnp.sum(e, axis=0, keepdims=True)
    f = lb + (1.0 - lb) * _sigmoid(f_ref[0].astype(F32))
    logf = jnp.log(f)
    kk = 1.0 - f
    qf = q_ref[0].astype(F32)
    qq = qf * _sigmoid(qf)

    n_ch = sub // HG_CHUNK
    r = lax.broadcasted_iota(jnp.int32, (sub, sub), 0)
    c = lax.broadcasted_iota(jnp.int32, (sub, sub), 1)
    within = (r >= c) & (r // HG_CHUNK == c // HG_CHUNK)
    tri = within.astype(BF16)
    rw = lax.broadcasted_iota(jnp.int32, (sub, n_ch * HEAD_DIM), 0)
    cw = lax.broadcasted_iota(jnp.int32, (sub, n_ch * HEAD_DIM), 1)
    own_block = rw // HG_CHUNK == cw // HEAD_DIM

    zero = jnp.zeros((), BF16)
    n_sub = rows // sub
    b_all, b_last_all = [], []
    for s in range(n_sub):
        lf = logf[s * sub:(s + 1) * sub, :]
        hi = lf.astype(BF16)
        rest = lf - hi.astype(F32)
        mid = rest.astype(BF16)
        low = (rest - mid.astype(F32)).astype(BF16)
        two = jnp.dot(tri, jnp.concatenate([hi, mid], axis=1), preferred_element_type=F32)
        b = two[:, :HEAD_DIM] + two[:, HEAD_DIM:] + jnp.dot(tri, low, preferred_element_type=F32)
        b_all.append(b)
        b_last_all.append([b[(n + 1) * HG_CHUNK - 1:(n + 1) * HG_CHUNK, :] for n in range(n_ch)])

    q_dec_all, o_all, upd_all = [], [], []
    for s in range(n_sub):
        rs = slice(s * sub, (s + 1) * sub)
        b = b_all[s]
        b_last_rows = jnp.concatenate([jnp.broadcast_to(bl, (HG_CHUNK, HEAD_DIM)) for bl in b_last_all[s]], axis=0)
        q_dec = (qq[rs, :] * jnp.exp(b)).astype(BF16)
        k_inv = (kk[rs, :] * jnp.exp(-b)).astype(BF16)
        k_tail = (kk[rs, :] * jnp.exp(b_last_rows - b)).astype(BF16)
        vv = i_ref[0, pl.ds(s * sub, sub), :]
        attn = lax.dot_general(q_dec, k_inv, NT_DIMS, preferred_element_type=F32)
        o_all.append(jnp.dot(jnp.where(within, attn, 0.0).astype(BF16), vv, preferred_element_type=F32))
        k_blocks = jnp.where(own_block, jnp.concatenate([k_tail] * n_ch, axis=1), zero)
        upd_all.append(lax.dot_general(vv, k_blocks, TN_DIMS, preferred_element_type=F32))
        q_dec_all.append(q_dec)

    st = st_sc[...]
    for s in range(n_sub):
        states = []
        for n in range(n_ch):
            states.append(st.astype(BF16))
            st = jnp.exp(b_last_all[s][n]) * st + upd_all[s][:, n * HEAD_DIM:(n + 1) * HEAD_DIM]
        q_blocks = jnp.where(own_block, jnp.concatenate([q_dec_all[s]] * n_ch, axis=1), zero)
        o = o_all[s] + lax.dot_general(q_blocks, jnp.concatenate(states, axis=1), NT_DIMS,
                                       preferred_element_type=F32)
        y = _rms(o) * gn_ref[0]
        gf = g_ref[0, pl.ds(s * sub, sub), :].astype(F32)
        o_ref[0, pl.ds(s * sub, sub), :] = (y * (gf * _sigmoid(gf))).astype(BF16)
    st_sc[...] = st


def _hgrn2(proj3, lb_logits3, g_hg3, n_heads, col0, layer):
    n_batch, seq, _ = proj3.shape
    rows = min(1024, seq)
    sub = min(256, rows)
    spec = lambda off: pl.BlockSpec((1, rows, HEAD_DIM), lambda b, h, r: (b, r, col0 + off * n_heads + h))
    return pl.pallas_call(
        functools.partial(_hgrn_kernel, rows=rows, sub=sub, layer=layer),
        out_shape=jax.ShapeDtypeStruct((n_batch, seq, n_heads * HEAD_DIM), BF16),
        grid=(n_batch, n_heads, seq // rows),
        in_specs=[spec(0), spec(1), spec(2), spec(3),
                  pl.BlockSpec((1, lb_logits3.shape[1], HEAD_DIM), lambda b, h, r: (h, 0, 0)),
                  pl.BlockSpec((1, 1, HEAD_DIM), lambda b, h, r: (h, 0, 0))],
        out_specs=pl.BlockSpec((1, rows, HEAD_DIM), lambda b, h, r: (b, r, h)),
        scratch_shapes=[pltpu.VMEM((HEAD_DIM, HEAD_DIM), F32)],
        compiler_params=_params("arbitrary", "arbitrary", "arbitrary"),
        name="hgrn2",
    )(proj3, proj3, proj3, proj3, lb_logits3, g_hg3)


def _pack_halves(v):
    n = v.shape[1] // 2
    lo = lax.bitcast_convert_type(v[:, :n].astype(BF16).astype(F32), jnp.uint32) >> 16
    hi = lax.bitcast_convert_type(v[:, n:].astype(BF16).astype(F32), jnp.uint32) & jnp.uint32(0xFFFF0000)
    return lo | hi


def _unpack_halves(p):
    lo = lax.bitcast_convert_type(p << 16, F32).astype(BF16)
    hi = lax.bitcast_convert_type(p & jnp.uint32(0xFFFF0000), F32).astype(BF16)
    return lo, hi


def _outproj_kernel(fox_ref, hg_ref, x_ref, mod_ref, gfox_ref, g2_ref, w_ref, wr_ref, br_ref,
                    x1_ref, h2_ref, lg_ref, *, n_exp, group):
    groups = [pl.ds(k * group, group) for k in range(x_ref.shape[0] // group)]
    mixed = []
    for rows in groups:
        fox = _rms(fox_ref[rows, :].astype(F32)) * gfox_ref[...]
        mixed.append(jnp.concatenate([fox.astype(BF16), hg_ref[rows, :]], axis=1))
    mix = [jnp.dot(m, w_ref[...], preferred_element_type=F32) for m in mixed]
    h2s = []
    for rows, m in zip(groups, mix):
        x1 = x_ref[rows, :] + mod_ref[0, 2:3, :] * m
        x1_ref[rows, :] = x1
        h2 = _rms(x1) * (g2_ref[...] * (1.0 + mod_ref[0, 4:5, :])) + mod_ref[0, 3:4, :]
        h2_ref[rows, :] = _pack_halves(h2)
        h2s.append(h2)
    for rows, h2 in zip(groups, h2s):
        h_hi = h2.astype(BF16)
        h_lo = (h2 - h_hi.astype(F32)).astype(BF16)
        part = jnp.dot(h_hi, wr_ref[...], preferred_element_type=F32)
        part = part + jnp.dot(h_lo, wr_ref[...], preferred_element_type=F32)
        logits = part + pltpu.roll(part, LANES - n_exp, axis=1)
        lg_ref[:, rows] = logits.T[0:n_exp, :] + br_ref[...]


def _outproj(fox2d, hg2d, x2d, mod3, g_fox, g2, w_out, w_router, b_router, seq):
    t, d = x2d.shape
    fox_w = fox2d.shape[1]
    n_exp = w_router.shape[1]
    assert 2 * n_exp <= LANES
    wr_hi = w_router.astype(BF16)
    wr_lo = (w_router - wr_hi.astype(F32)).astype(BF16)
    wr_cat = jnp.pad(jnp.concatenate([wr_hi, wr_lo], axis=1), ((0, 0), (0, LANES - 2 * n_exp)))
    tm = 512
    row = lambda i: (i, 0)
    const = lambda i: (0, 0)
    return pl.pallas_call(
        functools.partial(_outproj_kernel, n_exp=n_exp, group=256),
        out_shape=(jax.ShapeDtypeStruct((t, d), F32), jax.ShapeDtypeStruct((t, d // 2), jnp.uint32),
                   jax.ShapeDtypeStruct((n_exp, t), F32)),
        grid=(t // tm,),
        in_specs=[pl.BlockSpec((tm, fox_w), row),
                  pl.BlockSpec((tm, hg2d.shape[1]), row),
                  pl.BlockSpec((tm, d), row),
                  pl.BlockSpec((1, mod3.shape[1], d), lambda i: (i // (seq // tm), 0, 0)),
                  pl.BlockSpec((1, fox_w), const),
                  pl.BlockSpec((1, d), const),
                  pl.BlockSpec(w_out.shape, const),
                  pl.BlockSpec((d, LANES), const),
                  pl.BlockSpec((n_exp, 1), const)],
        out_specs=(pl.BlockSpec((tm, d), row), pl.BlockSpec((tm, d // 2), row),
                   pl.BlockSpec((n_exp, tm), lambda i: (0, i))),
        compiler_params=_params("arbitrary"),
        name="outproj_router",
    )(fox2d, hg2d, x2d, mod3, g_fox, g2, w_out, wr_cat, b_router)


def _route_kernel(lg_ref, pos_ref, gate_ref, cnt_ref, cnt_sc, run_sc, *, n_exp, tb):
    phase = pl.program_id(0)
    i = pl.program_id(1)

    @pl.when((phase == 0) & (i == 0))
    def _():
        cnt_sc[...] = jnp.zeros_like(cnt_sc)
        run_sc[...] = jnp.zeros_like(run_sc)

    logits = lg_ref[...]
    eidx = lax.broadcasted_iota(jnp.int32, (n_exp, tb), 0).astype(F32)
    work = logits
    vals, hots = [], []
    for _ in range(TOP_K):
        m = jnp.max(work, axis=0, keepdims=True)
        first = jnp.min(jnp.where(work == m, eidx, float(n_exp)), axis=0, keepdims=True)
        hot = eidx == first
        vals.append(m)
        hots.append(hot)
        work = jnp.where(hot, -jnp.inf, work)
    sel = hots[0] | hots[1] | hots[2] | hots[3]
    self32 = sel.astype(F32)

    @pl.when(phase == 0)
    def _():
        cnt_sc[...] += jnp.sum(self32, axis=1, keepdims=True)

    @pl.when(phase == 1)
    def _():
        cnt = cnt_sc[...]
        padded = jnp.ceil(cnt / MOE_BLOCK) * MOE_BLOCK
        r = lax.broadcasted_iota(jnp.int32, (n_exp, n_exp), 0)
        c = lax.broadcasted_iota(jnp.int32, (n_exp, n_exp), 1)
        strict = (c < r).astype(F32)
        pstart = jnp.dot(strict, jnp.broadcast_to(padded, (n_exp, LANES)), preferred_element_type=F32,
                         precision=HIGHEST)[:, 0:1]
        tr = lax.broadcasted_iota(jnp.int32, (tb, tb), 0)
        tc = lax.broadcasted_iota(jnp.int32, (tb, tb), 1)
        upper = (tr < tc).astype(BF16)
        rank = jnp.dot(sel.astype(BF16), upper, preferred_element_type=F32)
        base = pstart + run_sc[...] + rank
        exps = [jnp.exp(v - vals[0]) for v in vals]
        denom = exps[0] + exps[1] + exps[2] + exps[3]
        for k in range(TOP_K):
            pos_k = jnp.sum(jnp.where(hots[k], base, 0.0), axis=0, keepdims=True)
            pos_ref[pl.ds(k, 1), :] = pos_k.astype(jnp.int32)
            gate_ref[pl.ds(k, 1), :] = exps[k] / denom
        run_sc[...] += jnp.sum(self32, axis=1, keepdims=True)
        cnt_ref[...] = cnt


def _route(logits_t):
    n_exp, t = logits_t.shape
    tb = min(512, t)
    return pl.pallas_call(
        functools.partial(_route_kernel, n_exp=n_exp, tb=tb),
        out_shape=(jax.ShapeDtypeStruct((TOP_K, t), jnp.int32), jax.ShapeDtypeStruct((TOP_K, t), F32),
                   jax.ShapeDtypeStruct((n_exp, 1), F32)),
        grid=(2, t // tb),
        in_specs=[pl.BlockSpec((n_exp, tb), lambda p, i: (0, i))],
        out_specs=(pl.BlockSpec((TOP_K, tb), lambda p, i: (0, i * p)),
                   pl.BlockSpec((TOP_K, tb), lambda p, i: (0, i * p)),
                   pl.BlockSpec((n_exp, 1), lambda p, i: (0, 0))),
        scratch_shapes=[pltpu.VMEM((n_exp, 1), F32), pltpu.VMEM((n_exp, 1), F32)],
        compiler_params=_params("arbitrary", "arbitrary"),
        name="route_topk",
    )(logits_t)


N_DMA_QUEUES = 2


def _row_copy(src_hbm, row, dst_vmem, slot, sem):
    return pltpu.make_async_copy(src_hbm.at[pl.ds(row, 1), :], dst_vmem.at[pl.ds(slot, 1), :], sem)


def _gather_rows(src_hbm, idx_ref, idx0, n_rows, dst_vmem, sem):
    for r in range(n_rows):
        _row_copy(src_hbm, idx_ref[0, 0, idx0 + r], dst_vmem, r, sem).start(priority=r % N_DMA_QUEUES)


def _wait_rows(src_hbm, n_rows, dst_vmem, sem):
    pltpu.make_async_copy(src_hbm.at[pl.ds(0, n_rows), :], dst_vmem, sem).wait()


def _dispatch_kernel(pad0_ref, padn_ref, nused_ref, pos_ref, h_ref, xs_hbm, zeros, sem, zsem,
                     *, tb, n_exp, n_blocks):
    @pl.when(pl.program_id(0) == 0)
    def _():
        zeros[...] = jnp.zeros_like(zeros)

        def each_fill(fn):
            def per_expert(e, carry):
                def per_row(r, inner):
                    fn(pltpu.make_async_copy(zeros.at[pl.ds(0, 1), :], xs_hbm.at[pl.ds(pad0_ref[e] + r, 1), :], zsem))
                    return inner

                lax.fori_loop(0, padn_ref[e], per_row, 0)
                return carry

            lax.fori_loop(0, n_exp, per_expert, 0)

            def per_block(b, carry):
                row0 = pl.multiple_of(b * MOE_BLOCK, MOE_BLOCK)
                fn(pltpu.make_async_copy(zeros, xs_hbm.at[pl.ds(row0, MOE_BLOCK), :], zsem))
                return carry

            lax.fori_loop(nused_ref[0], n_blocks, per_block, 0)

        each_fill(lambda cp: cp.start())
        each_fill(lambda cp: cp.wait())

    for r in range(tb):
        for k in range(TOP_K):
            pltpu.make_async_copy(h_ref.at[pl.ds(r, 1), :], xs_hbm.at[pl.ds(pos_ref[0, 0, k * tb + r], 1), :],
                                  sem).start(priority=k % N_DMA_QUEUES)
    for k in range(TOP_K):
        pltpu.make_async_copy(h_ref, xs_hbm.at[pl.ds(0, tb), :], sem).wait()


def _dispatch(h2p, pos3, pad_start, pad_count, n_used, n_rows):
    t, dp = h2p.shape
    n_tiles = pos3.shape[0]
    tb = pos3.shape[2] // TOP_K
    return pl.pallas_call(
        functools.partial(_dispatch_kernel, tb=tb, n_exp=pad_start.shape[0], n_blocks=n_rows // MOE_BLOCK),
        out_shape=jax.ShapeDtypeStruct((n_rows, dp), h2p.dtype),
        grid_spec=pltpu.PrefetchScalarGridSpec(
            num_scalar_prefetch=3,
            grid=(n_tiles,),
            in_specs=[pl.BlockSpec((1, 1, TOP_K * tb), lambda i, p0, pn, nu: (i, 0, 0), memory_space=pltpu.SMEM),
                      pl.BlockSpec((tb, dp), lambda i, p0, pn, nu: (i, 0))],
            out_specs=pl.BlockSpec(memory_space=pl.ANY),
            scratch_shapes=[pltpu.VMEM((MOE_BLOCK, dp), h2p.dtype), pltpu.SemaphoreType.DMA(()),
                            pltpu.SemaphoreType.DMA(())]),
        compiler_params=_params("arbitrary"),
        name="moe_dispatch",
    )(pad_start, pad_count, n_used, pos3, h2p)


def _new_expert(be_ref, b):
    return (b == 0) | (be_ref[b] != be_ref[jnp.maximum(b - 1, 0)])


W_PIECES = 4


def _weight_copies(w_hbm, e, col0, stage, sem):
    rows = stage.shape[0] // W_PIECES
    return [pltpu.make_async_copy(w_hbm.at[e, pl.ds(i * rows, rows), pl.ds(col0, stage.shape[1])],
                                  stage.at[pl.ds(i * rows, rows), :], sem) for i in range(W_PIECES)]


def _convert_rows(src_f32, dst_bf16):
    rows = 16

    def body(i, carry):
        r0 = pl.multiple_of(i * rows, rows)
        dst_bf16[pl.ds(r0, rows), :] = src_f32[pl.ds(r0, rows), :].astype(BF16)
        return carry

    lax.fori_loop(0, src_f32.shape[0] // rows, body, 0, unroll=4)


def _gateup_kernel(be_ref, nxt_ref, last_ref, nused_ref, x_ref, w_hbm, bg_ref, bl_ref, o_ref,
                   stg_g, stg_l, wg_sc, wl_sc, sem, *, nf, tf):
    c = pl.program_id(0)
    b = pl.program_id(1)

    def slab(e, chunk):
        col = pl.multiple_of(chunk * tf, tf)
        return _weight_copies(w_hbm, e, col, stg_g, sem) + _weight_copies(w_hbm, e, col + nf * tf, stg_l, sem)

    @pl.when((c == 0) & (b == 0))
    def _():
        for cp in slab(be_ref[0], 0):
            cp.start()

    @pl.when(_new_expert(be_ref, b))
    def _():
        for cp in slab(0, 0):
            cp.wait()
        _convert_rows(stg_g, wg_sc)
        _convert_rows(stg_l, wl_sc)
        last_run = last_ref[b] == 1

        @pl.when(jnp.logical_not(last_run))
        def _():
            for cp in slab(nxt_ref[b], c):
                cp.start()

        @pl.when(last_run & (c + 1 < nf))
        def _():
            for cp in slab(be_ref[0], c + 1):
                cp.start()

    @pl.when(b < nused_ref[0])
    def _():
        x = jnp.concatenate(_unpack_halves(x_ref[...]), axis=1)
        glu = jnp.dot(x, wg_sc[...], preferred_element_type=F32) + bg_ref[0]
        lin = jnp.dot(x, wl_sc[...], preferred_element_type=F32) + bl_ref[0]
        glu = jnp.minimum(glu, SWIGLU_LIMIT)
        lin = jnp.clip(lin, -SWIGLU_LIMIT, SWIGLU_LIMIT)
        o_ref[...] = (glu * _sigmoid(SWIGLU_ALPHA * glu) * (lin + 1.0)).astype(BF16)

    @pl.when(b >= nused_ref[0])
    def _():
        o_ref[...] = jnp.zeros_like(o_ref)


def _gateup(runs, n_used, xs, w_gu, b_gu3):
    n_rows, dp = xs.shape
    d = 2 * dp
    n_blocks = n_rows // MOE_BLOCK
    d_ff = w_gu.shape[2] // 2
    tf = min(1024, d_ff)
    nf = d_ff // tf
    return pl.pallas_call(
        functools.partial(_gateup_kernel, nf=nf, tf=tf),
        out_shape=jax.ShapeDtypeStruct((n_rows, d_ff), BF16),
        grid_spec=pltpu.PrefetchScalarGridSpec(
            num_scalar_prefetch=4,
            grid=(nf, n_blocks),
            in_specs=[pl.BlockSpec((MOE_BLOCK, dp), lambda c, b, be, nx, la, nu: (jnp.minimum(b, nu[0] - 1), 0)),
                      pl.BlockSpec(memory_space=pl.ANY),
                      pl.BlockSpec((1, 1, tf), lambda c, b, be, nx, la, nu: (be[b], 0, c)),
                      pl.BlockSpec((1, 1, tf), lambda c, b, be, nx, la, nu: (be[b], 0, nf + c))],
            out_specs=pl.BlockSpec((MOE_BLOCK, tf), lambda c, b, be, nx, la, nu: (b, c)),
            scratch_shapes=[pltpu.VMEM((d, tf), F32), pltpu.VMEM((d, tf), F32),
                            pltpu.VMEM((d, tf), BF16), pltpu.VMEM((d, tf), BF16),
                            pltpu.SemaphoreType.DMA(())]),
        compiler_params=_params("arbitrary", "arbitrary"),
        name="moe_gateup",
    )(*runs, n_used, xs, w_gu, b_gu3, b_gu3)


def _down_kernel(be_ref, nxt_ref, last_ref, nused_ref, a_ref, w_hbm, b_ref, o_ref, stg, w_sc, sem):
    b = pl.program_id(0)

    @pl.when(b == 0)
    def _():
        for cp in _weight_copies(w_hbm, be_ref[0], 0, stg, sem):
            cp.start()

    @pl.when(_new_expert(be_ref, b))
    def _():
        for cp in _weight_copies(w_hbm, 0, 0, stg, sem):
            cp.wait()
        _convert_rows(stg, w_sc)

        @pl.when(last_ref[b] == 0)
        def _():
            for cp in _weight_copies(w_hbm, nxt_ref[b], 0, stg, sem):
                cp.start()

    @pl.when(b < nused_ref[0])
    def _():
        y = jnp.dot(a_ref[...], w_sc[...], preferred_element_type=F32) + b_ref[0]
        o_ref[...] = _pack_halves(y)

    @pl.when(b >= nused_ref[0])
    def _():
        o_ref[...] = jnp.zeros_like(o_ref)


def _down(runs, n_used, act, w_down, b_down3):
    n_rows, d_ff = act.shape
    n_blocks = n_rows // MOE_BLOCK
    d = w_down.shape[2]
    return pl.pallas_call(
        _down_kernel,
        out_shape=jax.ShapeDtypeStruct((n_rows, d // 2), jnp.uint32),
        grid_spec=pltpu.PrefetchScalarGridSpec(
            num_scalar_prefetch=4,
            grid=(n_blocks,),
            in_specs=[pl.BlockSpec((MOE_BLOCK, d_ff), lambda b, be, nx, la, nu: (b, 0)),
                      pl.BlockSpec(memory_space=pl.ANY),
                      pl.BlockSpec((1, 1, d), lambda b, be, nx, la, nu: (be[b], 0, 0))],
            out_specs=pl.BlockSpec((MOE_BLOCK, d // 2), lambda b, be, nx, la, nu: (b, 0)),
            scratch_shapes=[pltpu.VMEM((d_ff, d), F32), pltpu.VMEM((d_ff, d), BF16),
                            pltpu.SemaphoreType.DMA(())]),
        compiler_params=_params("arbitrary"),
        name="moe_down",
    )(*runs, n_used, act, w_down, b_down3)


COMBINE_TILES = 4


def _combine_kernel(pos_ref, pos_next_ref, y_hbm, gate_ref, x1_ref, mod_ref, gfin_ref, o_ref, b0, b1, b2, b3, sem,
                    *, tb, n_steps, last_layer):
    i = pl.program_id(0)
    bufs = (b0, b1, b2, b3)
    n_idx = TOP_K * tb

    def request(idx_ref, u_src, u_dst):
        _gather_rows(y_hbm, idx_ref, u_src * n_idx, n_idx, bufs[u_dst], sem.at[u_dst])

    def combine(u):
        rows = pl.ds(u * tb, tb)
        gates = gate_ref[rows, :]
        moe_lo = moe_hi = None
        for k in range(TOP_K):
            lo, hi = _unpack_halves(bufs[u][pl.ds(k * tb, tb), :])
            g = gates[:, k:k + 1]
            moe_lo = g * lo.astype(F32) if k == 0 else moe_lo + g * lo.astype(F32)
            moe_hi = g * hi.astype(F32) if k == 0 else moe_hi + g * hi.astype(F32)
        moe = jnp.concatenate([moe_lo, moe_hi], axis=1)
        x2 = x1_ref[rows, :] + mod_ref[0, 5:6, :] * moe
        o_ref[rows, :] = _rms(x2) * gfin_ref[...] if last_layer else x2

    @pl.when(i == 0)
    def _():
        request(pos_ref, 0, 0)
        request(pos_ref, 1, 1)

    for u in range(COMBINE_TILES):
        _wait_rows(y_hbm, n_idx, bufs[u], sem.at[u])
        if u + 2 < COMBINE_TILES:
            request(pos_ref, u + 2, u + 2)
        else:
            request(pos_next_ref, u + 2 - COMBINE_TILES, u + 2 - COMBINE_TILES)
        combine(u)

    @pl.when(i == n_steps - 1)
    def _():
        _wait_rows(y_hbm, n_idx, b0, sem.at[0])
        _wait_rows(y_hbm, n_idx, b1, sem.at[1])


def _combine(pos3, y_buf, gates, x1, mod3, g_final, seq, last_layer):
    t, d = x1.shape
    n_steps = pos3.shape[0]
    tb = pos3.shape[2] // (COMBINE_TILES * TOP_K)
    rows = COMBINE_TILES * tb
    idx_spec = lambda step: pl.BlockSpec((1, 1, pos3.shape[2]), lambda i: (jnp.minimum(i + step, n_steps - 1), 0, 0),
                                         memory_space=pltpu.SMEM)
    buf = pltpu.VMEM((TOP_K * tb, y_buf.shape[1]), y_buf.dtype)
    return pl.pallas_call(
        functools.partial(_combine_kernel, tb=tb, n_steps=n_steps, last_layer=last_layer),
        out_shape=jax.ShapeDtypeStruct((t, d), F32),
        grid=(n_steps,),
        in_specs=[idx_spec(0), idx_spec(1),
                  pl.BlockSpec(memory_space=pl.ANY),
                  pl.BlockSpec((rows, TOP_K), lambda i: (i, 0)),
                  pl.BlockSpec((rows, d), lambda i: (i, 0)),
                  pl.BlockSpec((1, mod3.shape[1], d), lambda i: (i // (seq // rows), 0, 0)),
                  pl.BlockSpec((1, d), lambda i: (0, 0))],
        out_specs=pl.BlockSpec((rows, d), lambda i: (i, 0)),
        scratch_shapes=[buf] * COMBINE_TILES + [pltpu.SemaphoreType.DMA((COMBINE_TILES,))],
        compiler_params=_params("arbitrary"),
        name="moe_combine",
    )(pos3, pos3, y_buf, gates, x1, mod3, g_final)


def kernel(x, c, w_ada, b_ada, g_norm1, g_norm2, w_in, b_fgate, g_fox_out, lb_logits, g_hg_out,
           w_out, w_router, b_router, w_gu, b_gu, w_down, b_down, g_final):
    n_batch, seq, d = x.shape
    t = n_batch * seq
    depth = w_ada.shape[0]
    fox_heads = b_fgate.shape[1]
    fox_w = g_fox_out.shape[1]
    hg_w = g_hg_out.shape[1]
    hg_heads = hg_w // HEAD_DIM
    n_exp = w_router.shape[2]
    assert fox_w == fox_heads * HEAD_DIM and fox_heads <= LANES
    n_blocks = -(-t * TOP_K // MOE_BLOCK) + n_exp
    n_mod = w_ada.shape[2] // d

    x2d = x.reshape(t, d)
    for l in range(depth):
        mod3 = _ada_mod(c, w_ada[l], b_ada[l]).reshape(n_batch, n_mod, d)

        w_l = w_in[l].astype(BF16)
        w_main = jnp.concatenate([w_l[:, :3 * fox_w], w_l[:, 3 * fox_w + fox_heads:]], axis=1)
        w_fg = jnp.pad(w_l[:, 3 * fox_w:3 * fox_w + fox_heads], ((0, 0), (0, LANES - fox_heads)))
        b_fg = jnp.pad(b_fgate[l], (0, LANES - fox_heads)).reshape(1, LANES)
        proj, cum = _inproj(x2d, mod3, g_norm1[l].reshape(1, d), w_main, w_fg, b_fg, seq)
        proj3 = proj.reshape(n_batch, seq, -1)

        fox = _fox_attention(proj3, cum.reshape(n_batch, seq, LANES), fox_heads)

        lb3 = lb_logits.reshape(lb_logits.shape[0], hg_heads, HEAD_DIM).transpose(1, 0, 2)
        hg = _hgrn2(proj3, lb3, g_hg_out[l].reshape(hg_heads, 1, HEAD_DIM), hg_heads, 3 * fox_heads, l)

        x1, h2p, logits_t = _outproj(
            fox.reshape(t, fox_w), hg.reshape(t, hg_w), x2d, mod3, g_fox_out[l].reshape(1, fox_w),
            g_norm2[l].reshape(1, d), w_out[l].astype(BF16), w_router[l], b_router[l].reshape(n_exp, 1), seq)

        pos_t, gates_t, counts = _route(logits_t)

        cnt = counts[:, 0].astype(jnp.int32)
        padded = (cnt + MOE_BLOCK - 1) // MOE_BLOCK * MOE_BLOCK
        padded_end = jnp.cumsum(padded)
        block_row0 = jnp.arange(n_blocks, dtype=jnp.int32) * MOE_BLOCK
        block_e = jnp.sum((block_row0[:, None] >= padded_end[None, :]).astype(jnp.int32), axis=1)
        n_used = (padded_end[-1:] // MOE_BLOCK).astype(jnp.int32)
        block_id = jnp.arange(n_blocks, dtype=jnp.int32)
        block_e = jnp.minimum(block_e, jnp.sum(jnp.where(block_id == n_used - 1, block_e, 0)))
        change_at = jnp.where(block_e != jnp.roll(block_e, 1), block_id, n_blocks).at[0].set(n_blocks)
        next_change = jnp.flip(lax.cummin(jnp.flip(jnp.roll(change_at, -1).at[-1].set(n_blocks))))
        last_run = (next_change >= n_blocks).astype(jnp.int32)
        next_e = block_e[jnp.minimum(next_change, n_blocks - 1)]
        runs = (block_e, next_e, last_run)

        def tile_major(tb):
            return pos_t.reshape(TOP_K, t // tb, tb).transpose(1, 0, 2).reshape(t // tb, 1, TOP_K * tb)

        pad_count = padded - cnt
        xs = _dispatch(h2p, tile_major(min(512, seq)), padded_end - pad_count, pad_count, n_used,
                       n_blocks * MOE_BLOCK)
        act = _gateup(runs, n_used, xs, w_gu[l], b_gu[l].reshape(n_exp, 1, -1))
        y_buf = _down(runs, n_used, act, w_down[l], b_down[l].reshape(n_exp, 1, d))
        tb = min(128, seq // COMBINE_TILES)
        pos_steps = tile_major(tb).reshape(t // (COMBINE_TILES * tb), 1, COMBINE_TILES * TOP_K * tb)
        x2d = _combine(pos_steps, y_buf, gates_t.T, x1, mod3, g_final.reshape(1, d), seq, l == depth - 1)
    return x2d.reshape(n_batch, seq, d)
```
